```python
import jax, jax.numpy as jnp
from jax import lax
import numpy as np

D_MODEL = 4096
BATCH = 2
SEQ = 8192
DEPTH = 2

HG_WIDTH = D_MODEL // 4
HG_DK = 128
HG_HEADS = HG_WIDTH // HG_DK
HG_DV = HG_WIDTH // HG_HEADS
HG_CHUNK = 64
MIN_FORGET = 1e-30
POOL_WIDTH = D_MODEL // 4
POOL_WINDOWS = (2, 4, 8, 16)
POOL_GROUPS = len(POOL_WINDOWS)
POOL_GDIM = POOL_WIDTH // POOL_GROUPS
MLA_HEADS = D_MODEL // 256
MLA_Q_RANK = D_MODEL // 4
MLA_KV_RANK = 512
MLA_NOPE = 128
MLA_ROPE = 64
MLA_V = 128
MLA_WIDTH = MLA_HEADS * MLA_V
ROPE_THETA = 10000.0
ATTN_BLOCK = 128
MASK_VALUE = -1e30
N_BRANCH = 3
IN_SIZES = (HG_WIDTH, HG_WIDTH, HG_WIDTH, HG_WIDTH, POOL_WIDTH, MLA_Q_RANK, MLA_KV_RANK, MLA_ROPE, N_BRANCH * D_MODEL)
IN_COLS = sum(IN_SIZES)
IN_SPLITS = [int(v) for v in np.cumsum(IN_SIZES)[:-1]]
D_FF = 256 * (-(-(8 * D_MODEL // 3) // 256))
N_EXPERTS = 8
TOP_K = 2
D_FF_EXPERT = 7 * D_MODEL // 8
N_DENSE = (DEPTH + 1) // 2
N_MOE = DEPTH // 2
N_MOD = 6
NORM_EPS = 1e-6

kernel_name = 'hybrid_hgrn2_pool_mla_moe_adaln'

F32 = jnp.float32


def rms_norm(x, gain):
    xf = x.astype(F32)
    y = xf * lax.rsqrt(jnp.mean(xf * xf, axis=-1, keepdims=True) + NORM_EPS)
    return (y * gain.astype(F32)).astype(x.dtype)


def modulate(x, gain, shift, scale):
    h = rms_norm(x, gain)
    return h * (1 + scale[:, None, :]) + shift[:, None, :]


def hgrn2_branch(q_in, f_in, i_in, g_in, lower_bound, out_gain):
    B, S, _ = q_in.shape
    dt = q_in.dtype
    z = f_in.astype(F32)
    lb = lower_bound.astype(F32)
    f = lb + (1 - lb) * jax.nn.sigmoid(z)
    log_f = jnp.log(jnp.maximum(f, MIN_FORGET))
    k = 1 - f
    q = jax.nn.silu(q_in.astype(F32))
    v = i_in.astype(F32)
    n = S // HG_CHUNK

    def to_chunks(t, d):
        return t.reshape(B, n, HG_CHUNK, HG_HEADS, d).transpose(1, 0, 3, 2, 4)

    xs = (to_chunks(q, HG_DK), to_chunks(k, HG_DK), to_chunks(v, HG_DV), to_chunks(log_f, HG_DK))
    causal = jnp.tril(jnp.ones((HG_CHUNK, HG_CHUNK), dtype=bool))[:, :, None]

    def step(state, inp):
        qc, kc, vc, lfc = inp
        b = jnp.cumsum(lfc, axis=-2)
        diff = b[:, :, :, None, :] - b[:, :, None, :, :]
        decay = jnp.where(causal, jnp.exp(jnp.minimum(diff, 0.0)), 0.0)
        scores = jnp.einsum('bhtd,bhsd,bhtsd->bhts', qc, kc, decay)
        o = jnp.einsum('bhts,bhse->bhte', scores, vc) + jnp.einsum('bhtd,bhde->bhte', qc * jnp.exp(b), state)
        b_last = b[:, :, -1:, :]
        new_state = jnp.exp(b_last[:, :, 0, :])[..., None] * state + jnp.einsum('bhsd,bhse->bhde', kc * jnp.exp(b_last - b), vc)
        return new_state, o

    state0 = jnp.zeros((B, HG_HEADS, HG_DK, HG_DV), F32)
    _, o = lax.scan(step, state0, xs)
    o = o.transpose(1, 0, 3, 2, 4).reshape(B, S, HG_HEADS, HG_DV)
    g = g_in.astype(F32).reshape(B, S, HG_HEADS, HG_DV)
    o = rms_norm(o, out_gain) * jax.nn.silu(g)
    return o.reshape(B, S, HG_WIDTH).astype(dt)


def pool_branch(u, w_pool, scale):
    B, S, _ = u.shape
    uf = u.astype(F32).reshape(B, S, POOL_GROUPS, POOL_GDIM)
    cs = jnp.cumsum(uf, axis=1)
    count = jnp.arange(1, S + 1, dtype=F32)
    outs = []
    for g, w in enumerate(POOL_WINDOWS):
        csg = cs[:, :, g]
        lagged = jnp.pad(csg[:, :S - w], ((0, 0), (w, 0), (0, 0)))
        mean = (csg - lagged) / jnp.minimum(count, float(w))[None, :, None]
        outs.append(mean - uf[:, :, g])
    pooled = jnp.stack(outs, axis=2)
    y = jnp.einsum('bsgc,gcd->bsgd', pooled, w_pool.astype(F32))
    return (y.reshape(B, S, POOL_WIDTH) * scale.astype(F32)).astype(u.dtype)


def rope(x, cos, sin):
    x1, x2 = jnp.split(x, 2, axis=-1)
    return jnp.concatenate([x1 * cos - x2 * sin, x1 * sin + x2 * cos], axis=-1)


def mla_branch(cq, ckv, kpe, positions, q_norm, w_uq, kv_norm, w_ukv, g_qn, g_qr, g_kn, g_kr):
    B, S, _ = cq.shape
    dt = cq.dtype
    q = (rms_norm(cq, q_norm) @ w_uq).reshape(B, S, MLA_HEADS, MLA_NOPE + MLA_ROPE)
    q_nope = rms_norm(q[..., :MLA_NOPE], g_qn)
    q_pe = rms_norm(q[..., MLA_NOPE:], g_qr)
    kv = (rms_norm(ckv, kv_norm) @ w_ukv).reshape(B, S, MLA_HEADS, MLA_NOPE + MLA_V)
    k_nope = rms_norm(kv[..., :MLA_NOPE], g_kn)
    v = kv[..., MLA_NOPE:]
    k_pe = rms_norm(kpe, g_kr)
    inv_freq = ROPE_THETA ** (-jnp.arange(0, MLA_ROPE, 2, dtype=F32) / MLA_ROPE)
    ang = positions.astype(F32)[..., None] * inv_freq
    cos, sin = jnp.cos(ang), jnp.sin(ang)
    q_pe = rope(q_pe.astype(F32), cos[:, :, None], sin[:, :, None]).astype(dt)
    k_pe = rope(k_pe.astype(F32), cos, sin).astype(dt)

    kn = k_nope.transpose(0, 2, 1, 3)
    vh = v.transpose(0, 2, 1, 3)
    nb = S // ATTN_BLOCK
    qn_b = q_nope.reshape(B, nb, ATTN_BLOCK, MLA_HEADS, MLA_NOPE).transpose(1, 0, 3, 2, 4)
    qp_b = q_pe.reshape(B, nb, ATTN_BLOCK, MLA_HEADS, MLA_ROPE).transpose(1, 0, 3, 2, 4)
    sm_scale = (MLA_NOPE + MLA_ROPE) ** -0.5
    key_idx = jnp.arange(S)

    def attend(blk):
        qnb, qpb, i = blk
        s = (jnp.einsum('bhqd,bhkd->bhqk', qnb, kn, preferred_element_type=F32)
             + jnp.einsum('bhqd,bkd->bhqk', qpb, k_pe, preferred_element_type=F32))
        q_idx = i * ATTN_BLOCK + jnp.arange(ATTN_BLOCK)
        s = jnp.where(key_idx[None, :] <= q_idx[:, None], s * sm_scale, MASK_VALUE)
        p = jax.nn.softmax(s, axis=-1)
        return jnp.einsum('bhqk,bhkd->bhqd', p.astype(vh.dtype), vh)

    o = lax.map(attend, (qn_b, qp_b, jnp.arange(nb)))
    return o.transpose(1, 0, 3, 2, 4).reshape(B, S, MLA_WIDTH)


def mixer_sublayer(h, positions, w_in, lower_bound, hg_out_norm, pool_w, pool_scale,
                   q_norm, w_uq, kv_norm, w_ukv, g_qn, g_qr, g_kn, g_kr,
                   w_ba, w_bb, w_bc, w_o):
    B, S, D = h.shape
    proj = h @ w_in
    hq, hf, hi, hg, pu, cq, ckv, kpe, gate_logits = jnp.split(proj, IN_SPLITS, axis=-1)
    o_a = hgrn2_branch(hq, hf, hi, hg, lower_bound, hg_out_norm)
    o_b = pool_branch(pu, pool_w, pool_scale)
    o_c = mla_branch(cq, ckv, kpe, positions, q_norm, w_uq, kv_norm, w_ukv, g_qn, g_qr, g_kn, g_kr)
    gates = jax.nn.sigmoid(gate_logits.astype(F32)).reshape(B, S, N_BRANCH, D).astype(h.dtype)
    merged = gates[:, :, 0] * (o_a @ w_ba) + gates[:, :, 1] * (o_b @ w_bb) + gates[:, :, 2] * (o_c @ w_bc)
    return merged @ w_o


def dense_ffn(h, wg, wu, wd):
    return (jax.nn.silu(h @ wg) * (h @ wu)) @ wd


def moe_ffn(h, router, wg, wu, wd):
    B, S, D = h.shape
    t = h.reshape(B * S, D)
    logits = (t @ router).astype(F32)
    top_v, top_i = lax.top_k(logits, TOP_K)
    top_w = jax.nn.softmax(top_v, axis=-1)
    combine = jnp.sum(jax.nn.one_hot(top_i, N_EXPERTS, dtype=F32) * top_w[..., None], axis=1)
    y = jnp.zeros((B * S, D), F32)
    for e in range(N_EXPERTS):
        he = jax.nn.silu(t @ wg[e]) * (t @ wu[e])
        y = y + combine[:, e:e + 1] * (he @ wd[e]).astype(F32)
    return y.astype(h.dtype).reshape(B, S, D)


def setup_inputs(seed: int = 0) -> dict:
    key = jax.random.key(seed)
    ks = iter(jax.random.split(key, 40))

    def nrm(shape, scale):
        return jax.random.normal(next(ks), shape, F32) * scale

    def gain(shape):
        return 1.0 + nrm(shape, 0.1)

    D = D_MODEL
    positions = (jnp.arange(SEQ, dtype=jnp.int32)[None, :]
                 + jax.random.randint(next(ks), (BATCH, 1), 0, 1024, dtype=jnp.int32))
    return {
        'x': nrm((BATCH, SEQ, D), 1.0),
        'c': nrm((BATCH, D), 1.0),
        'positions': positions,
        'ada_w': nrm((D, N_MOD * D), D ** -0.5),
        'ada_b': nrm((N_MOD * D,), 0.02),
        'ada_layer': nrm((DEPTH, N_MOD, D), D ** -0.5),
        'mix_norm': gain((DEPTH, D)),
        'ffn_norm': gain((DEPTH, D)),
        'w_in': nrm((DEPTH, D, IN_COLS), D ** -0.5),
        'hgrn_lower_bounds': nrm((DEPTH, HG_WIDTH), 1.0),
        'hgrn_out_norm': gain((DEPTH, HG_DV)),
        'pool_w': nrm((DEPTH, POOL_GROUPS, POOL_GDIM, POOL_GDIM), POOL_GDIM ** -0.5),
        'pool_scale': gain((DEPTH, POOL_WIDTH)),
        'mla_q_norm': gain((DEPTH, MLA_Q_RANK)),
        'mla_w_uq': nrm((DEPTH, MLA_Q_RANK, MLA_HEADS * (MLA_NOPE + MLA_ROPE)), MLA_Q_RANK ** -0.5),
        'mla_kv_norm': gain((DEPTH, MLA_KV_RANK)),
        'mla_w_ukv': nrm((DEPTH, MLA_KV_RANK, MLA_HEADS * (MLA_NOPE + MLA_V)), MLA_KV_RANK ** -0.5),
        'mla_qk_norm_q_nope': gain((DEPTH, MLA_NOPE)),
        'mla_qk_norm_q_rope': gain((DEPTH, MLA_ROPE)),
        'mla_qk_norm_k_nope': gain((DEPTH, MLA_NOPE)),
        'mla_qk_norm_k_rope': gain((DEPTH, MLA_ROPE)),
        'w_branch_a': nrm((DEPTH, HG_WIDTH, D), HG_WIDTH ** -0.5),
        'w_branch_b': nrm((DEPTH, POOL_WIDTH, D), POOL_WIDTH ** -0.5),
        'w_branch_c': nrm((DEPTH, MLA_WIDTH, D), MLA_WIDTH ** -0.5),
        'w_o': nrm((DEPTH, D, D), D ** -0.5),
        'ffn_w_gate': nrm((N_DENSE, D, D_FF), D ** -0.5),
        'ffn_w_up': nrm((N_DENSE, D, D_FF), D ** -0.5),
        'ffn_w_down': nrm((N_DENSE, D_FF, D), D_FF ** -0.5),
        'moe_router': nrm((N_MOE, D, N_EXPERTS), D ** -0.5),
        'moe_w_gate': nrm((N_MOE, N_EXPERTS, D, D_FF_EXPERT), D ** -0.5),
        'moe_w_up': nrm((N_MOE, N_EXPERTS, D, D_FF_EXPERT), D ** -0.5),
        'moe_w_down': nrm((N_MOE, N_EXPERTS, D_FF_EXPERT, D), D_FF_EXPERT ** -0.5),
    }


def reference(x, c, positions, ada_w, ada_b, ada_layer, mix_norm, ffn_norm, w_in,
              hgrn_lower_bounds, hgrn_out_norm, pool_w, pool_scale,
              mla_q_norm, mla_w_uq, mla_kv_norm, mla_w_ukv,
              mla_qk_norm_q_nope, mla_qk_norm_q_rope, mla_qk_norm_k_nope, mla_qk_norm_k_rope,
              w_branch_a, w_branch_b, w_branch_c, w_o,
              ffn_w_gate, ffn_w_up, ffn_w_down,
              moe_router, moe_w_gate, moe_w_up, moe_w_down):
    B = x.shape[0]
    dt = x.dtype
    mod_shared = (jax.nn.silu(c.astype(F32)) @ ada_w.astype(F32) + ada_b.astype(F32)).reshape(B, N_MOD, D_MODEL)
    lb_soft = jax.nn.softmax(hgrn_lower_bounds.astype(F32), axis=0)
    lower_bounds = jnp.cumsum(lb_soft, axis=0) - lb_soft[0]

    for l in range(DEPTH):
        mod = (mod_shared + ada_layer[l].astype(F32)[None]).astype(dt)
        shift1, scale1, gate1 = mod[:, 0], mod[:, 1], mod[:, 2]
        shift2, scale2, gate2 = mod[:, 3], mod[:, 4], mod[:, 5]

        h = modulate(x, mix_norm[l], shift1, scale1)
        y = mixer_sublayer(h, positions, w_in[l], lower_bounds[l], hgrn_out_norm[l],
                           pool_w[l], pool_scale[l],
                           mla_q_norm[l], mla_w_uq[l], mla_kv_norm[l], mla_w_ukv[l],
                           mla_qk_norm_q_nope[l], mla_qk_norm_q_rope[l],
                           mla_qk_norm_k_nope[l], mla_qk_norm_k_rope[l],
                           w_branch_a[l], w_branch_b[l], w_branch_c[l], w_o[l])
        x = x + gate1[:, None, :] * y

        h = modulate(x, ffn_norm[l], shift2, scale2)
        if l % 2 == 0:
            j = l // 2
            y = dense_ffn(h, ffn_w_gate[j], ffn_w_up[j], ffn_w_down[j])
        else:
            j = l // 2
            y = moe_ffn(h, moe_router[j], moe_w_gate[j], moe_w_up[j], moe_w_down[j])
        x = x + gate2[:, None, :] * y
    return x
```

```python
import functools
import math

import numpy as np
import jax
import jax.numpy as jnp
from jax import lax
from jax.experimental import pallas as pl
from jax.experimental.pallas import tpu as pltpu

F32 = jnp.float32
BF16 = jnp.bfloat16

HG_DK = 128
POOL_WINDOWS = (2, 4, 8, 16)
MLA_NOPE = 128
MLA_ROPE = 64
MLA_V = 128
ROPE_THETA = 10000.0
MIN_FORGET = 1e-30
NORM_EPS = 1e-6
N_MOD = 6
TOP_K = 2
NEG_BIG = -1e30

V7X_LANES = 128
V7X_SUBLANES = 8
V7X_VMEM_BYTES = 64 * 1024 * 1024
VMEM_CAP = V7X_VMEM_BYTES - 8 * 1024 * 1024

HG_CHUNK = 128
HG_SUB = 8
POOL_HALO = 16


def _pick(n, prefs):
    for p in prefs:
        if n % p == 0:
            return p
    raise ValueError(f"no tile in {prefs} divides {n}")


def _cparams(sem, vmem_bytes):
    limit = int(min(VMEM_CAP, max(32 * 1024 * 1024, vmem_bytes * 5 // 4)))
    return pltpu.CompilerParams(dimension_semantics=sem, vmem_limit_bytes=limit)


def _sigmoid(x):
    return 1.0 / (1.0 + jnp.exp(-x))


def _silu(x):
    return x * _sigmoid(x)


def _dot(a, b):
    return jnp.dot(a, b, preferred_element_type=F32)


def _dot_nt(a, b):
    return lax.dot_general(a, b, (((1,), (1,)), ((), ())), preferred_element_type=F32)


def _dot_tn(a, b):
    return lax.dot_general(a, b, (((0,), (0,)), ((), ())), preferred_element_type=F32)


def _split3(x):
    hi = x.astype(BF16)
    r1 = x - hi.astype(F32)
    mid = r1.astype(BF16)
    lo = (r1 - mid.astype(F32)).astype(BF16)
    return hi, mid, lo


def _split2(x):
    hi = x.astype(BF16)
    lo = (x - hi.astype(F32)).astype(BF16)
    return hi, lo


def _ada_body(ct_ref, w_ref, b_ref, lay_ref, o_ref, *, n_batch, depth):
    w = w_ref[...]
    ct = ct_ref[...]
    s = _silu(ct)
    for b in range(n_batch):
        r = jnp.sum(w * s[:, b:b + 1], axis=0, keepdims=True) + b_ref[...]
        for l in range(depth):
            o_ref[l, b:b + 1, :] = r + lay_ref[l:l + 1, :]


def _ada(c, ada_w, ada_b, ada_layer):
    n_batch, d = c.shape
    depth = ada_layer.shape[0]
    n = ada_w.shape[1]
    tn = _pick(n, (512, 256, 128))
    ct = c.T
    lay = ada_layer.reshape(depth, n)
    out = pl.pallas_call(
        functools.partial(_ada_body, n_batch=n_batch, depth=depth),
        grid=(n // tn,),
        in_specs=[
            pl.BlockSpec((d, n_batch), lambda j: (0, 0)),
            pl.BlockSpec((d, tn), lambda j: (0, j)),
            pl.BlockSpec((1, tn), lambda j: (0, j)),
            pl.BlockSpec((depth, tn), lambda j: (0, j)),
        ],
        out_specs=pl.BlockSpec((depth, n_batch, tn), lambda j: (0, 0, j)),
        out_shape=jax.ShapeDtypeStruct((depth, n_batch, n), F32),
        compiler_params=_cparams(("arbitrary",), 4 * d * tn * 4),
        name="ada",
    )(ct, ada_w, ada_b.reshape(1, n), lay)
    return out.reshape(depth, n_batch, N_MOD, d)


def _modulated(x_ref, g_ref, mod_ref, shift_idx, scale_idx):
    x = x_ref[...]
    ms = jnp.mean(x * x, axis=-1, keepdims=True)
    y = x * lax.rsqrt(ms + NORM_EPS) * g_ref[...]
    return y * (1.0 + mod_ref[scale_idx:scale_idx + 1, :]) + mod_ref[shift_idx:shift_idx + 1, :]


def _modulate_body(x_ref, g_ref, mod_ref, o_ref, *, shift_idx, scale_idx):
    o_ref[...] = _modulated(x_ref, g_ref, mod_ref, shift_idx, scale_idx).astype(o_ref.dtype)


def _modulate_route_body(x_ref, g_ref, mod_ref, r_ref, o_ref, comb_ref, *, shift_idx, scale_idx, n_experts):
    h = _modulated(x_ref, g_ref, mod_ref, shift_idx, scale_idx)
    o_ref[...] = h.astype(o_ref.dtype)
    h_hi, h_mid, h_lo = _split3(h)
    r = r_ref[...]
    r_hi, r_mid, r_lo = _split3(r)
    logits = (_dot(h_hi, r_hi) + _dot(h_hi, r_mid) + _dot(h_mid, r_hi)
              + _dot(h_hi, r_lo) + _dot(h_mid, r_mid) + _dot(h_lo, r_hi))
    lane = lax.broadcasted_iota(jnp.int32, logits.shape, 1).astype(F32)
    lg = jnp.where(lane < n_experts, logits, -jnp.inf)
    m1 = jnp.max(lg, axis=-1, keepdims=True)
    i1 = jnp.min(jnp.where(lg == m1, lane, float(V7X_LANES)), axis=-1, keepdims=True)
    lg2 = jnp.where(lane == i1, -jnp.inf, lg)
    m2 = jnp.max(lg2, axis=-1, keepdims=True)
    i2 = jnp.min(jnp.where(lg2 == m2, lane, float(V7X_LANES)), axis=-1, keepdims=True)
    e2 = jnp.exp(m2 - m1)
    w1 = 1.0 / (1.0 + e2)
    w2 = e2 / (1.0 + e2)
    comb_ref[...] = jnp.where(lane == i1, w1, 0.0) + jnp.where(lane == i2, w2, 0.0)


def _modulate(x3, gain, mod_l, shift_idx, scale_idx, router=None):
    n_batch, seq, d = x3.shape
    ts = _pick(seq, (512, 256, 128))
    grid = (n_batch, seq // ts)
    x_spec = pl.BlockSpec((None, ts, d), lambda b, i: (b, i, 0))
    g_spec = pl.BlockSpec((1, d), lambda b, i: (0, 0))
    mod_spec = pl.BlockSpec((None, N_MOD, d), lambda b, i: (b, 0, 0))
    h_spec = pl.BlockSpec((None, ts, d), lambda b, i: (b, i, 0))
    vmem = 2 * ts * d * (4 + 2) + 4 * ts * d * 4
    if router is None:
        h = pl.pallas_call(
            functools.partial(_modulate_body, shift_idx=shift_idx, scale_idx=scale_idx),
            grid=grid,
            in_specs=[x_spec, g_spec, mod_spec],
            out_specs=h_spec,
            out_shape=jax.ShapeDtypeStruct((n_batch, seq, d), BF16),
            compiler_params=_cparams(("parallel", "parallel"), vmem),
            name="modulate",
        )(x3, gain.reshape(1, d), mod_l)
        return h.reshape(n_batch * seq, d)
    n_experts = router.shape[1]
    assert n_experts <= V7X_LANES
    r_pad = jnp.zeros((d, V7X_LANES), F32).at[:, :n_experts].set(router)
    h, comb = pl.pallas_call(
        functools.partial(_modulate_route_body, shift_idx=shift_idx, scale_idx=scale_idx, n_experts=n_experts),
        grid=grid,
        in_specs=[x_spec, g_spec, mod_spec, pl.BlockSpec((d, V7X_LANES), lambda b, i: (0, 0))],
        out_specs=[h_spec, pl.BlockSpec((None, ts, V7X_LANES), lambda b, i: (b, i, 0))],
        out_shape=[jax.ShapeDtypeStruct((n_batch, seq, d), BF16),
                   jax.ShapeDtypeStruct((n_batch, seq, V7X_LANES), F32)],
        compiler_params=_cparams(("parallel", "parallel"), vmem + 6 * ts * d * 2),
        name="modulate_route",
    )(x3, gain.reshape(1, d), mod_l, r_pad)
    return h.reshape(n_batch * seq, d), comb.reshape(n_batch * seq, V7X_LANES)


def _mm_cast_body(a_ref, w_ref, o_ref):
    o_ref[...] = _dot(a_ref[...], w_ref[...]).astype(o_ref.dtype)


def _mm_cast(a, w, out_dtype=BF16, name="mm"):
    m, k = a.shape
    n = w.shape[1]
    tn = n if n <= 2048 else _pick(n, (1024, 512, 256, 128))
    need = lambda tm_: 2 * (tm_ * k * 2 + k * tn * 2 + tm_ * tn * 2) + tm_ * tn * 4
    tm = next(t_ for t_ in (1024, 512, 256, 128) if m % t_ == 0 and need(t_) * 5 // 4 <= VMEM_CAP)
    vmem = need(tm)
    return pl.pallas_call(
        _mm_cast_body,
        grid=(m // tm, n // tn),
        in_specs=[pl.BlockSpec((tm, k), lambda i, j: (i, 0)),
                  pl.BlockSpec((k, tn), lambda i, j: (0, j))],
        out_specs=pl.BlockSpec((tm, tn), lambda i, j: (i, j)),
        out_shape=jax.ShapeDtypeStruct((m, n), out_dtype),
        compiler_params=_cparams(("parallel", "arbitrary"), vmem),
        name=name,
    )(a, w)


def _mm_swiglu_body(a_ref, wg_ref, wu_ref, o_ref):
    a = a_ref[...]
    g = _dot(a, wg_ref[...])
    u = _dot(a, wu_ref[...])
    o_ref[...] = (_silu(g) * u).astype(o_ref.dtype)


def _mm_swiglu(a, wg, wu, name="swiglu"):
    m, k = a.shape
    n = wg.shape[1]
    tm = _pick(m, (1024, 512, 256, 128))
    tn = _pick(n, (512, 256, 128))
    vmem = 2 * (tm * k * 2 + 2 * k * tn * 2 + tm * tn * 2) + 3 * tm * tn * 4
    return pl.pallas_call(
        _mm_swiglu_body,
        grid=(m // tm, n // tn),
        in_specs=[pl.BlockSpec((tm, k), lambda i, j: (i, 0)),
                  pl.BlockSpec((k, tn), lambda i, j: (0, j)),
                  pl.BlockSpec((k, tn), lambda i, j: (0, j))],
        out_specs=pl.BlockSpec((tm, tn), lambda i, j: (i, j)),
        out_shape=jax.ShapeDtypeStruct((m, n), BF16),
        compiler_params=_cparams(("parallel", "arbitrary"), vmem),
        name=name,
    )(a, wg, wu)


def _mm_residual_body(a_ref, w_ref, x_ref, gate_ref, *rest, nk, expert):
    if expert is None:
        o_ref = rest[0]
        scale = gate_ref[...]
    else:
        comb_ref, o_ref = rest
        scale = gate_ref[...] * comb_ref[:, expert:expert + 1]
    part = _dot(a_ref[...], w_ref[...])
    if nk == 1:
        o_ref[...] = x_ref[...] + scale * part
    else:
        kk = pl.program_id(2)

        @pl.when(kk == 0)
        def _():
            o_ref[...] = part

        @pl.when(jnp.logical_and(kk > 0, kk < nk - 1))
        def _():
            o_ref[...] += part

        @pl.when(kk == nk - 1)
        def _():
            o_ref[...] = x_ref[...] + scale * (o_ref[...] + part)


def _mm_residual(a, w, x, gate, seq, comb=None, expert=None, name="mm_res"):
    m, k = a.shape
    n = w.shape[1]
    tm = _pick(seq, (1024, 512, 256, 128))
    tn = _pick(n, (1024, 512, 256, 128))
    if k <= 4096:
        tk = k
    else:
        tk = next(t for t in range(4096 // V7X_LANES * V7X_LANES, 0, -V7X_LANES) if k % t == 0)
        if tk < 512:
            tk = next(t for t in range(k // 2 // V7X_LANES * V7X_LANES, 0, -V7X_LANES) if k % t == 0)
    nk = k // tk
    if tk > 4096:
        tn = _pick(n, (512, 256, 128))
    per_batch = seq // tm
    in_specs = [pl.BlockSpec((tm, tk), lambda i, j, kk: (i, kk)),
                pl.BlockSpec((tk, tn), lambda i, j, kk: (kk, j)),
                pl.BlockSpec((tm, tn), lambda i, j, kk: (i, j)),
                pl.BlockSpec((None, 1, tn), lambda i, j, kk: (i // per_batch, 0, j))]
    args = [a, w, x, gate]
    if comb is not None:
        in_specs.append(pl.BlockSpec((tm, V7X_LANES), lambda i, j, kk: (i, 0)))
        args.append(comb)
    vmem = 2 * (tm * tk * 2 + tk * tn * 2 + 2 * tm * tn * 4) + 2 * tm * tn * 4
    return pl.pallas_call(
        functools.partial(_mm_residual_body, nk=nk, expert=expert),
        grid=(m // tm, n // tn, nk),
        in_specs=in_specs,
        out_specs=pl.BlockSpec((tm, tn), lambda i, j, kk: (i, j)),
        out_shape=jax.ShapeDtypeStruct((m, n), F32),
        compiler_params=_cparams(("parallel", "parallel", "arbitrary"), vmem),
        name=name,
    )(*args)


def _merge_body(a_ref, b_ref, c_ref, wa_ref, wb_ref, wc_ref, ga_ref, gb_ref, gc_ref, o_ref):
    ya = _dot(a_ref[...], wa_ref[...])
    yb = _dot(b_ref[...], wb_ref[...])
    yc = _dot(c_ref[...], wc_ref[...])
    out = (_sigmoid(ga_ref[...].astype(F32)) * ya + _sigmoid(gb_ref[...].astype(F32)) * yb
           + _sigmoid(gc_ref[...].astype(F32)) * yc)
    o_ref[...] = out.astype(o_ref.dtype)


def _merge(o_a, o_b, o_c, w_a, w_b, w_c, gates):
    m = o_a.shape[0]
    d = w_a.shape[1]
    tm = _pick(m, (1024, 512, 256, 128))
    tn = _pick(d, (512, 256, 128))
    nj = d // tn
    ka, kb, kc = o_a.shape[1], o_b.shape[1], o_c.shape[1]
    vmem = 2 * 2 * (tm * (ka + kb + kc) + (ka + kb + kc) * tn + 4 * tm * tn) + 6 * tm * tn * 4
    return pl.pallas_call(
        _merge_body,
        grid=(m // tm, nj),
        in_specs=[pl.BlockSpec((tm, ka), lambda i, j: (i, 0)),
                  pl.BlockSpec((tm, kb), lambda i, j: (i, 0)),
                  pl.BlockSpec((tm, kc), lambda i, j: (i, 0)),
                  pl.BlockSpec((ka, tn), lambda i, j: (0, j)),
                  pl.BlockSpec((kb, tn), lambda i, j: (0, j)),
                  pl.BlockSpec((kc, tn), lambda i, j: (0, j)),
                  pl.BlockSpec((tm, tn), lambda i, j: (i, j)),
                  pl.BlockSpec((tm, tn), lambda i, j: (i, nj + j)),
                  pl.BlockSpec((tm, tn), lambda i, j: (i, 2 * nj + j))],
        out_specs=pl.BlockSpec((tm, tn), lambda i, j: (i, j)),
        out_shape=jax.ShapeDtypeStruct((m, d), BF16),
        compiler_params=_cparams(("parallel", "arbitrary"), vmem),
        name="merge",
    )(o_a, o_b, o_c, w_a, w_b, w_c, gates, gates, gates)


_HG_LEVELS = (HG_CHUNK // HG_SUB).bit_length() - 1


def _hgrn_level_masks():
    ti = np.arange(HG_CHUNK)[:, None]
    si = np.arange(HG_CHUNK)[None, :]
    out = []
    for lvl in range(_HG_LEVELS):
        half = HG_SUB << lvl
        blk = 2 * half
        out.append((ti // blk == si // blk) & (ti % blk >= half) & (si % blk < half))
    return jnp.asarray(np.stack(out), F32)


def _hgrn_chunk(q_in, f_in, v, g_in, lb, gain, state_t, tri, ones, lvl_mask_ref):
    c = HG_CHUNK
    fg = lb + (1.0 - lb) * _sigmoid(f_in)
    log_f = jnp.log(jnp.maximum(fg, MIN_FORGET))
    k = 1.0 - fg
    q = _silu(q_in)
    lf_hi, lf_mid, lf_lo = _split3(log_f)
    b = _dot(tri, lf_hi) + _dot(tri, lf_mid) + _dot(tri, lf_lo)
    b_last = b[c - 1:c, :]

    o = _dot_nt((q * jnp.exp(b)).astype(BF16), state_t.astype(BF16))

    row = lax.broadcasted_iota(jnp.int32, (c, 1), 0)
    scores = jnp.zeros((c, c), F32)
    for lvl in range(_HG_LEVELS):
        half = HG_SUB << lvl
        blk = 2 * half
        bref = jnp.concatenate(
            [jnp.broadcast_to(b[p * blk + half - 1:p * blk + half, :], (blk, HG_DK)) for p in range(c // blk)],
            axis=0)
        is_q = (row & half) != 0
        e = jnp.exp(-jnp.abs(b - bref))
        xk = jnp.where(is_q, q, k) * e
        qd = jnp.where(is_q, xk, 0.0).astype(BF16)
        kd = jnp.where(is_q, 0.0, xk).astype(BF16)
        scores = scores + _dot_nt(qd, kd) * lvl_mask_ref[lvl]
    o = o + _dot(scores.astype(BF16), v.astype(BF16))

    nb = c // HG_SUB
    b3 = b.reshape(nb, HG_SUB, HG_DK)
    q3 = q.reshape(nb, HG_SUB, HG_DK)
    k3 = k.reshape(nb, HG_SUB, HG_DK)
    v3 = v.reshape(nb, HG_SUB, HG_DK)
    t_in = lax.broadcasted_iota(jnp.int32, (nb, HG_SUB, HG_DK), 1)
    for s in range(HG_SUB):
        diff = b3 - b3[:, s:s + 1, :]
        dec = jnp.where(t_in >= s, jnp.exp(jnp.minimum(diff, 0.0)), 0.0)
        m = (q3 * (k3[:, s:s + 1, :] * dec)).reshape(c, HG_DK)
        m_hi, m_lo = _split2(m)
        r = _dot(m_hi, ones) + _dot(m_lo, ones)
        o = o + r * jnp.broadcast_to(v3[:, s:s + 1, :], (nb, HG_SUB, HG_DK)).reshape(c, HG_DK)

    kdec = (k * jnp.exp(b_last - b)).astype(BF16)
    new_state_t = state_t * jnp.exp(b_last) + _dot_tn(v.astype(BF16), kdec)

    ms = jnp.mean(o * o, axis=-1, keepdims=True)
    out = o * lax.rsqrt(ms + NORM_EPS) * gain * _silu(g_in)
    return out, new_state_t


def _hgrn_body(q_ref, f_ref, i_ref, g_ref, lbraw_ref, gain_ref, lvl_mask_ref, o_ref, state_ref, *, layer, n_chunks):
    @pl.when(pl.program_id(2) == 0)
    def _():
        state_ref[...] = jnp.zeros_like(state_ref)

    lbr = lbraw_ref[...]
    ex = jnp.exp(lbr - jnp.max(lbr, axis=0, keepdims=True))
    soft = ex / jnp.sum(ex, axis=0, keepdims=True)
    lb = jnp.zeros((1, HG_DK), F32)
    for j in range(1, layer + 1):
        lb = lb + soft[j:j + 1, :]
    gain = gain_ref[...]
    c = HG_CHUNK
    tri = (lax.broadcasted_iota(jnp.int32, (c, c), 0) >= lax.broadcasted_iota(jnp.int32, (c, c), 1)).astype(BF16)
    ones = jnp.ones((HG_DK, HG_DK), BF16)

    def chunk(ci, carry):
        sl = pl.ds(pl.multiple_of(ci * c, c), c)
        out, new_state = _hgrn_chunk(q_ref[sl, :].astype(F32), f_ref[sl, :].astype(F32),
                                     i_ref[sl, :].astype(F32), g_ref[sl, :].astype(F32),
                                     lb, gain, state_ref[...], tri, ones, lvl_mask_ref)
        o_ref[sl, :] = out.astype(o_ref.dtype)
        state_ref[...] = new_state
        return carry

    lax.fori_loop(0, n_chunks, chunk, 0)


def _hgrn(proj, lb_raw, out_gain, layer, n_batch, seq, width):
    heads = width // HG_DK
    lc = _pick(seq, (512, 256, 128))
    per_batch = seq // lc
    depth = lb_raw.shape[0]

    def col(off):
        return pl.BlockSpec((lc, HG_DK), lambda b, h, i: (b * per_batch + i, off * heads + h))

    return pl.pallas_call(
        functools.partial(_hgrn_body, layer=layer, n_chunks=lc // HG_CHUNK),
        grid=(n_batch, heads, per_batch),
        in_specs=[col(0), col(1), col(2), col(3),
                  pl.BlockSpec((depth, HG_DK), lambda b, h, i: (0, h)),
                  pl.BlockSpec((1, HG_DK), lambda b, h, i: (0, 0)),
                  pl.BlockSpec((_HG_LEVELS, HG_CHUNK, HG_CHUNK), lambda b, h, i: (0, 0, 0))],
        out_specs=pl.BlockSpec((lc, HG_DK), lambda b, h, i: (b * per_batch + i, h)),
        out_shape=jax.ShapeDtypeStruct((n_batch * seq, width), BF16),
        scratch_shapes=[pltpu.VMEM((HG_DK, HG_DK), F32)],
        compiler_params=_cparams(("parallel", "parallel", "arbitrary"), 16 * 1024 * 1024),
        name="hgrn2",
    )(proj, proj, proj, proj, lb_raw, out_gain.reshape(1, HG_DK), _hgrn_level_masks())


def _pool_body(u_ref, halo_ref, w_ref, scale_ref, o_ref, ext_ref, *, ts, gdim):
    i = pl.program_id(1)
    halo = halo_ref[...].astype(F32)
    ext_ref[0:POOL_HALO, :] = jnp.where(i > 0, halo, 0.0)
    ext_ref[POOL_HALO:, :] = u_ref[...].astype(F32)
    pos = (i * ts + lax.broadcasted_iota(jnp.int32, (ts, 1), 0) + 1).astype(F32)
    for g, win in enumerate(POOL_WINDOWS):
        cs = slice(g * gdim, (g + 1) * gdim)
        acc = ext_ref[POOL_HALO:, cs]
        for j in range(1, win):
            acc = acc + ext_ref[POOL_HALO - j:POOL_HALO - j + ts, cs]
        pooled = acc / jnp.minimum(pos, float(win)) - ext_ref[POOL_HALO:, cs]
        y = _dot(pooled.astype(BF16), w_ref[g])
        o_ref[:, cs] = (y * scale_ref[:, cs]).astype(o_ref.dtype)


def _pool(proj, col_off, pool_w, pool_scale, n_batch, seq, width):
    groups = len(POOL_WINDOWS)
    gdim = width // groups
    assert gdim % V7X_LANES == 0 and col_off % width == 0 and max(POOL_WINDOWS) <= POOL_HALO
    ts = _pick(seq, (512, 256, 128))
    per_batch = seq // ts
    cb = col_off // width
    hb = ts // POOL_HALO
    return pl.pallas_call(
        functools.partial(_pool_body, ts=ts, gdim=gdim),
        grid=(n_batch, per_batch),
        in_specs=[pl.BlockSpec((ts, width), lambda b, i: (b * per_batch + i, cb)),
                  pl.BlockSpec((POOL_HALO, width),
                               lambda b, i: (jnp.maximum((b * per_batch + i) * hb - 1, 0), cb)),
                  pl.BlockSpec((groups, gdim, gdim), lambda b, i: (0, 0, 0)),
                  pl.BlockSpec((1, width), lambda b, i: (0, 0))],
        out_specs=pl.BlockSpec((ts, width), lambda b, i: (b * per_batch + i, 0)),
        out_shape=jax.ShapeDtypeStruct((n_batch * seq, width), BF16),
        scratch_shapes=[pltpu.VMEM((ts + POOL_HALO, width), F32)],
        compiler_params=_cparams(("parallel", "parallel"), 16 * 1024 * 1024),
        name="pool",
    )(proj, proj, pool_w, pool_scale.reshape(1, width))


def _rope_table_body(pos_ref, cos_ref, sin_ref):
    pos = pos_ref[...].astype(F32)
    lane = lax.broadcasted_iota(jnp.int32, (1, V7X_LANES), 1)
    j = lane % MLA_ROPE
    fidx = (j % (MLA_ROPE // 2)).astype(F32)
    inv_freq = jnp.exp(fidx * (-2.0 / MLA_ROPE * math.log(ROPE_THETA)))
    ang = pos * inv_freq
    cos_ref[...] = jnp.cos(ang)
    sin_ref[...] = jnp.where(j < MLA_ROPE // 2, -1.0, 1.0) * jnp.sin(ang)


def _rope_table(positions):
    t = positions.size
    ts = _pick(t, (512, 256, 128))
    return pl.pallas_call(
        _rope_table_body,
        grid=(t // ts,),
        in_specs=[pl.BlockSpec((ts, 1), lambda i: (i, 0))],
        out_specs=[pl.BlockSpec((ts, V7X_LANES), lambda i: (i, 0))] * 2,
        out_shape=[jax.ShapeDtypeStruct((t, V7X_LANES), F32)] * 2,
        compiler_params=_cparams(("parallel",), 4 * 1024 * 1024),
        name="rope_table",
    )(positions.reshape(t, 1))


def _rms(x, gain):
    return x * lax.rsqrt(jnp.mean(x * x, axis=-1, keepdims=True) + NORM_EPS) * gain


def _prep_q_body(cq_ref, qn_ref, w_ref, gn_ref, gr_ref, grs_ref, seg_ref, cos_ref, sin_ref, o_ref,
                 *, heads, sm_scale):
    hn = heads * MLA_NOPE
    hr = heads * MLA_ROPE
    hq = _rms(cq_ref[...].astype(F32), qn_ref[...]).astype(BF16)
    y = _dot(hq, w_ref[...])
    yr = y[:, hn:hn + hr]
    ys = y[:, hn + hr:]
    sq_hi, sq_lo = _split2(yr * yr)
    seg = seg_ref[...]
    ss = _dot(sq_hi, seg) + _dot(sq_lo, seg)
    inv = lax.rsqrt(ss * (1.0 / MLA_ROPE) + NORM_EPS)
    reps = hr // V7X_LANES
    cosf = jnp.concatenate([cos_ref[...]] * reps, axis=1)
    sinf = jnp.concatenate([sin_ref[...]] * reps, axis=1)
    qr = (yr * inv * gr_ref[...]) * cosf + (ys * inv * grs_ref[...]) * sinf
    gn = gn_ref[...] * sm_scale
    for h in range(heads):
        qn = _rms(y[:, h * MLA_NOPE:(h + 1) * MLA_NOPE], gn)
        o_ref[h, :, 0:MLA_NOPE] = qn.astype(o_ref.dtype)
        o_ref[h, :, MLA_NOPE:] = (qr[:, h * MLA_ROPE:(h + 1) * MLA_ROPE] * sm_scale).astype(o_ref.dtype)


def _prep_kv_body(ckv_ref, kpe_ref, kvn_ref, w_ref, gn_ref, gr2_ref, cos_ref, sin_ref, k_ref, v_ref, *, heads):
    hk = _rms(ckv_ref[...].astype(F32), kvn_ref[...]).astype(BF16)
    y = _dot(hk, w_ref[...])
    kp = kpe_ref[...].astype(F32)
    kpe = kp[:, :MLA_ROPE]
    inv = lax.rsqrt(jnp.mean(kpe * kpe, axis=-1, keepdims=True) + NORM_EPS)
    kn = kp * inv * gr2_ref[...]
    kr = (kn[:, :MLA_ROPE] * cos_ref[:, :MLA_ROPE] + kn[:, MLA_ROPE:] * sin_ref[:, :MLA_ROPE]).astype(k_ref.dtype)
    per = MLA_NOPE + MLA_V
    for h in range(heads):
        k_ref[h, :, 0:MLA_NOPE] = _rms(y[:, h * per:h * per + MLA_NOPE], gn_ref[...]).astype(k_ref.dtype)
        k_ref[h, :, MLA_NOPE:] = kr
        v_ref[h] = y[:, h * per + MLA_NOPE:(h + 1) * per].astype(v_ref.dtype)


def _flash_body(q_ref, k_ref, v_ref, o_ref, m_ref, l_ref, acc_ref, *, tq):
    qi = pl.program_id(2)
    q = q_ref[...]
    m_ref[...] = jnp.full_like(m_ref, NEG_BIG)
    l_ref[...] = jnp.zeros_like(l_ref)
    acc_ref[...] = jnp.zeros_like(acc_ref)

    def step(j, masked):
        sl = pl.ds(pl.multiple_of(j * tq, tq), tq)
        s = _dot_nt(q, k_ref[sl, :])
        if masked:
            r = lax.broadcasted_iota(jnp.int32, s.shape, 0)
            cidx = lax.broadcasted_iota(jnp.int32, s.shape, 1)
            s = jnp.where(cidx <= r, s, NEG_BIG)
        m_old = m_ref[...]
        m_new = jnp.maximum(m_old, jnp.max(s, axis=-1, keepdims=True))
        alpha = jnp.exp(m_old - m_new)
        p = jnp.exp(s - m_new)
        l_ref[...] = alpha * l_ref[...] + jnp.sum(p, axis=-1, keepdims=True)
        acc_ref[...] = alpha * acc_ref[...] + _dot(p.astype(BF16), v_ref[sl, :])
        m_ref[...] = m_new

    def off_diag(j, carry):
        step(j, False)
        return carry

    lax.fori_loop(0, qi, off_diag, 0)
    step(qi, True)
    o_ref[...] = (acc_ref[...] / l_ref[...]).astype(o_ref.dtype)


def _mla(proj_c, cos_t, sin_t, q_norm, w_q, kv_norm, w_kv, g_qn, g_qr, g_kn, g_kr,
         n_batch, seq, heads, q_rank, kv_rank):
    t = n_batch * seq
    half = MLA_ROPE // 2
    hn, hr = heads * MLA_NOPE, heads * MLA_ROPE
    dq = MLA_NOPE + MLA_ROPE
    sm_scale = float(dq) ** -0.5
    assert q_rank % kv_rank == 0 and (q_rank + kv_rank) % V7X_LANES == 0 and hr % V7X_LANES == 0
    tm = _pick(t, (512, 256, 128))

    def swap(g):
        return jnp.concatenate([g[half:], g[:half]])

    gr = jnp.tile(g_qr, heads).reshape(1, hr)
    grs = jnp.tile(swap(g_qr), heads).reshape(1, hr)
    lane = jnp.arange(hr) // MLA_ROPE
    seg = (lane[:, None] == lane[None, :]).astype(BF16)
    row = lambda i: (i, 0)
    const2 = lambda i: (0, 0)
    qcat = pl.pallas_call(
        functools.partial(_prep_q_body, heads=heads, sm_scale=sm_scale),
        grid=(t // tm,),
        in_specs=[pl.BlockSpec((tm, q_rank), row),
                  pl.BlockSpec((1, q_rank), const2),
                  pl.BlockSpec((q_rank, hn + 2 * hr), const2),
                  pl.BlockSpec((1, MLA_NOPE), const2),
                  pl.BlockSpec((1, hr), const2),
                  pl.BlockSpec((1, hr), const2),
                  pl.BlockSpec((hr, hr), const2),
                  pl.BlockSpec((tm, V7X_LANES), row),
                  pl.BlockSpec((tm, V7X_LANES), row)],
        out_specs=pl.BlockSpec((heads, tm, dq), lambda i: (0, i, 0)),
        out_shape=jax.ShapeDtypeStruct((heads, t, dq), BF16),
        compiler_params=_cparams(("parallel",), 40 * 1024 * 1024),
        name="mla_prep_q",
    )(proj_c, q_norm.reshape(1, q_rank), w_q, g_qn.reshape(1, MLA_NOPE), gr, grs, seg, cos_t, sin_t)

    gr2 = jnp.concatenate([g_kr, swap(g_kr)]).reshape(1, 2 * MLA_ROPE)
    kcat, v = pl.pallas_call(
        functools.partial(_prep_kv_body, heads=heads),
        grid=(t // tm,),
        in_specs=[pl.BlockSpec((tm, kv_rank), lambda i: (i, q_rank // kv_rank)),
                  pl.BlockSpec((tm, V7X_LANES), lambda i: (i, (q_rank + kv_rank) // V7X_LANES)),
                  pl.BlockSpec((1, kv_rank), const2),
                  pl.BlockSpec((kv_rank, heads * (MLA_NOPE + MLA_V)), const2),
                  pl.BlockSpec((1, MLA_NOPE), const2),
                  pl.BlockSpec((1, 2 * MLA_ROPE), const2),
                  pl.BlockSpec((tm, V7X_LANES), row),
                  pl.BlockSpec((tm, V7X_LANES), row)],
        out_specs=[pl.BlockSpec((heads, tm, dq), lambda i: (0, i, 0)),
                   pl.BlockSpec((heads, tm, MLA_V), lambda i: (0, i, 0))],
        out_shape=[jax.ShapeDtypeStruct((heads, t, dq), BF16),
                   jax.ShapeDtypeStruct((heads, t, MLA_V), BF16)],
        compiler_params=_cparams(("parallel",), 40 * 1024 * 1024),
        name="mla_prep_kv",
    )(proj_c, proj_c, kv_norm.reshape(1, kv_rank), w_kv, g_kn.reshape(1, MLA_NOPE), gr2, cos_t, sin_t)

    tq = _pick(seq, (512, 256, 128))
    nq = seq // tq
    return pl.pallas_call(
        functools.partial(_flash_body, tq=tq),
        grid=(heads, n_batch, nq),
        in_specs=[pl.BlockSpec((None, tq, dq), lambda h, b, i: (h, b * nq + i, 0)),
                  pl.BlockSpec((None, seq, dq), lambda h, b, i: (h, b, 0)),
                  pl.BlockSpec((None, seq, MLA_V), lambda h, b, i: (h, b, 0))],
        out_specs=pl.BlockSpec((tq, MLA_V), lambda h, b, i: (b * nq + i, h)),
        out_shape=jax.ShapeDtypeStruct((t, heads * MLA_V), BF16),
        scratch_shapes=[pltpu.VMEM((tq, 1), F32), pltpu.VMEM((tq, 1), F32), pltpu.VMEM((tq, MLA_V), F32)],
        compiler_params=_cparams(("parallel", "parallel", "arbitrary"), 32 * 1024 * 1024),
        name="mla_flash",
    )(qcat, kcat, v)


def kernel(x, c, positions, ada_w, ada_b, ada_layer, mix_norm, ffn_norm, w_in, hgrn_lower_bounds,
           hgrn_out_norm, pool_w, pool_scale, mla_q_norm, mla_w_uq, mla_kv_norm, mla_w_ukv,
           mla_qk_norm_q_nope, mla_qk_norm_q_rope, mla_qk_norm_k_nope, mla_qk_norm_k_rope,
           w_branch_a, w_branch_b, w_branch_c, w_o, ffn_w_gate, ffn_w_up, ffn_w_down,
           moe_router, moe_w_gate, moe_w_up, moe_w_down):
    n_batch, seq, d = x.shape
    depth = w_in.shape[0]
    t = n_batch * seq
    hg_width = hgrn_lower_bounds.shape[1]
    pool_width = pool_scale.shape[1]
    q_rank = mla_q_norm.shape[1]
    kv_rank = mla_kv_norm.shape[1]
    heads = mla_w_ukv.shape[2] // (MLA_NOPE + MLA_V)
    half = MLA_ROPE // 2
    n_a = 4 * hg_width + pool_width
    n_c = q_rank + kv_rank + MLA_ROPE
    assert w_in.shape[2] == n_a + n_c + 3 * d

    mod = _ada(c, ada_w, ada_b, ada_layer)
    cos_t, sin_t = _rope_table(positions)
    xf = x.reshape(t, d)

    for l in range(depth):
        mod_l = mod[l]
        gate1 = mod_l[:, 2:3, :]
        gate2 = mod_l[:, 5:6, :]

        w_l = w_in[l]
        w_a = w_l[:, :n_a].astype(BF16)
        kpe_w = w_l[:, n_a + q_rank + kv_rank:n_a + n_c]
        w_c = jnp.concatenate([w_l[:, n_a:n_a + n_c], kpe_w[:, half:], kpe_w[:, :half]], axis=1).astype(BF16)
        w_g = w_l[:, n_a + n_c:].astype(BF16)
        wq = mla_w_uq[l].reshape(q_rank, heads, MLA_NOPE + MLA_ROPE)
        wq_r = wq[:, :, MLA_NOPE:]
        w_q = jnp.concatenate([
            wq[:, :, :MLA_NOPE].reshape(q_rank, heads * MLA_NOPE),
            wq_r.reshape(q_rank, heads * MLA_ROPE),
            jnp.concatenate([wq_r[:, :, half:], wq_r[:, :, :half]], axis=2).reshape(q_rank, heads * MLA_ROPE),
        ], axis=1).astype(BF16)

        h = _modulate(xf.reshape(n_batch, seq, d), mix_norm[l], mod_l, 0, 1)
        proj_a = _mm_cast(h, w_a, name="proj_a")
        proj_c = _mm_cast(h, w_c, name="proj_c")
        gates = _mm_cast(h, w_g, name="proj_gates")

        o_a = _hgrn(proj_a, hgrn_lower_bounds, hgrn_out_norm[l], l, n_batch, seq, hg_width)
        o_b = _pool(proj_a, 4 * hg_width, pool_w[l].astype(BF16), pool_scale[l], n_batch, seq, pool_width)
        o_c = _mla(proj_c, cos_t, sin_t, mla_q_norm[l], w_q, mla_kv_norm[l], mla_w_ukv[l].astype(BF16),
                   mla_qk_norm_q_nope[l], mla_qk_norm_q_rope[l], mla_qk_norm_k_nope[l], mla_qk_norm_k_rope[l],
                   n_batch, seq, heads, q_rank, kv_rank)
        merged = _merge(o_a, o_b, o_c, w_branch_a[l].astype(BF16), w_branch_b[l].astype(BF16),
                        w_branch_c[l].astype(BF16), gates)
        xf = _mm_residual(merged, w_o[l].astype(BF16), xf, gate1, seq, name="out_proj")

        j = l // 2
        if l % 2 == 0:
            h = _modulate(xf.reshape(n_batch, seq, d), ffn_norm[l], mod_l, 3, 4)
            act = _mm_swiglu(h, ffn_w_gate[j].astype(BF16), ffn_w_up[j].astype(BF16), name="ffn_up")
            xf = _mm_residual(act, ffn_w_down[j].astype(BF16), xf, gate2, seq, name="ffn_down")
        else:
            h, comb = _modulate(xf.reshape(n_batch, seq, d), ffn_norm[l], mod_l, 3, 4, router=moe_router[j])
            for e in range(moe_w_gate.shape[1]):
                act = _mm_swiglu(h, moe_w_gate[j, e].astype(BF16), moe_w_up[j, e].astype(BF16), name="moe_up")
                xf = _mm_residual(act, moe_w_down[j, e].astype(BF16), xf, gate2, seq, comb=comb, expert=e,
                                  name="moe_down")
    return xf.reshape(n_batch, seq, d)
```

```python
import functools
import math

import numpy as np
import jax
import jax.numpy as jnp
from jax import lax
from jax.experimental import pallas as pl
from jax.experimental.pallas import tpu as pltpu

F32 = jnp.float32
BF16 = jnp.bfloat16

HG_DK = 128
POOL_WINDOWS = (2, 4, 8, 16)
MLA_NOPE = 128
MLA_ROPE = 64
MLA_V = 128
ROPE_THETA = 10000.0
MIN_FORGET = 1e-30
NORM_EPS = 1e-6
N_MOD = 6
TOP_K = 2
NEG_BIG = -1e30

V7X_LANES = 128
V7X_SUBLANES = 8
V7X_BF16_ROWS = 16
V7X_VMEM_BYTES = 64 * 1024 * 1024
VMEM_CAP = V7X_VMEM_BYTES - 8 * 1024 * 1024

HG_CHUNK = 128
HG_SUB = 8
POOL_HALO = 16
ATTN_TILE = 1024
ROUTE_E1, ROUTE_E2, ROUTE_W1, ROUTE_W2 = 0, 1, 2, 3
MOE_TILE = 512
GATHER_ROWS = 256


def _pick(n, prefs):
    for p in prefs:
        if n % p == 0:
            return p
    raise ValueError(f"no tile in {prefs} divides {n}")


def _cparams(sem, vmem_bytes):
    limit = int(min(VMEM_CAP, max(32 * 1024 * 1024, vmem_bytes * 5 // 4)))
    return pltpu.CompilerParams(dimension_semantics=sem, vmem_limit_bytes=limit)


def _sigmoid(x):
    return 1.0 / (1.0 + jnp.exp(-x))


def _silu(x):
    return x * _sigmoid(x)


def _dot(a, b):
    return jnp.dot(a, b, preferred_element_type=F32)


def _dot_nt(a, b):
    return lax.dot_general(a, b, (((1,), (1,)), ((), ())), preferred_element_type=F32)


def _dot_tn(a, b):
    return lax.dot_general(a, b, (((0,), (0,)), ((), ())), preferred_element_type=F32)


def _split3(x):
    hi = x.astype(BF16)
    r1 = x - hi.astype(F32)
    mid = r1.astype(BF16)
    lo = (r1 - mid.astype(F32)).astype(BF16)
    return hi, mid, lo


def _split2(x):
    hi = x.astype(BF16)
    lo = (x - hi.astype(F32)).astype(BF16)
    return hi, lo


def _ada_body(ct_ref, w_ref, b_ref, lay_ref, o_ref, *, n_batch, depth):
    w = w_ref[...]
    ct = ct_ref[...]
    s = _silu(ct)
    for b in range(n_batch):
        r = jnp.sum(w * s[:, b:b + 1], axis=0, keepdims=True) + b_ref[...]
        for l in range(depth):
            o_ref[l, b:b + 1, :] = r + lay_ref[l:l + 1, :]


def _ada(c, ada_w, ada_b, ada_layer):
    n_batch, d = c.shape
    depth = ada_layer.shape[0]
    n = ada_w.shape[1]
    tn = _pick(n, (512, 256, 128))
    ct = c.T
    lay = ada_layer.reshape(depth, n)
    out = pl.pallas_call(
        functools.partial(_ada_body, n_batch=n_batch, depth=depth),
        grid=(n // tn,),
        in_specs=[
            pl.BlockSpec((d, n_batch), lambda j: (0, 0)),
            pl.BlockSpec((d, tn), lambda j: (0, j)),
            pl.BlockSpec((1, tn), lambda j: (0, j)),
            pl.BlockSpec((depth, tn), lambda j: (0, j)),
        ],
        out_specs=pl.BlockSpec((depth, n_batch, tn), lambda j: (0, 0, j)),
        out_shape=jax.ShapeDtypeStruct((depth, n_batch, n), F32),
        compiler_params=_cparams(("arbitrary",), 4 * d * tn * 4),
        name="ada",
    )(ct, ada_w, ada_b.reshape(1, n), lay)
    return out.reshape(depth, n_batch, N_MOD, d)


def _modulated(x_ref, g_ref, mod_ref, shift_idx, scale_idx):
    x = x_ref[...]
    ms = jnp.mean(x * x, axis=-1, keepdims=True)
    y = x * lax.rsqrt(ms + NORM_EPS) * g_ref[...]
    return y * (1.0 + mod_ref[scale_idx:scale_idx + 1, :]) + mod_ref[shift_idx:shift_idx + 1, :]


def _modulate_body(x_ref, g_ref, mod_ref, o_ref, *, shift_idx, scale_idx):
    o_ref[...] = _modulated(x_ref, g_ref, mod_ref, shift_idx, scale_idx).astype(o_ref.dtype)


def _modulate_route_body(x_ref, g_ref, mod_ref, r_ref, o_ref, route_ref, *, shift_idx, scale_idx, n_experts):
    h = _modulated(x_ref, g_ref, mod_ref, shift_idx, scale_idx)
    o_ref[...] = h
    h_hi, h_mid, h_lo = _split3(h)
    r = r_ref[...]
    r_hi, r_mid, r_lo = _split3(r)
    logits = (_dot(h_hi, r_hi) + _dot(h_hi, r_mid) + _dot(h_mid, r_hi)
              + _dot(h_hi, r_lo) + _dot(h_mid, r_mid) + _dot(h_lo, r_hi))
    lane = lax.broadcasted_iota(jnp.int32, logits.shape, 1).astype(F32)
    lg = jnp.where(lane < n_experts, logits, -jnp.inf)
    m1 = jnp.max(lg, axis=-1, keepdims=True)
    i1 = jnp.min(jnp.where(lg == m1, lane, float(V7X_LANES)), axis=-1, keepdims=True)
    lg2 = jnp.where(lane == i1, -jnp.inf, lg)
    m2 = jnp.max(lg2, axis=-1, keepdims=True)
    i2 = jnp.min(jnp.where(lg2 == m2, lane, float(V7X_LANES)), axis=-1, keepdims=True)
    e2 = jnp.exp(m2 - m1)
    w1 = 1.0 / (1.0 + e2)
    w2 = e2 / (1.0 + e2)
    route_ref[...] = (jnp.where(lane == ROUTE_E1, i1, 0.0) + jnp.where(lane == ROUTE_E2, i2, 0.0)
                      + jnp.where(lane == ROUTE_W1, w1, 0.0) + jnp.where(lane == ROUTE_W2, w2, 0.0))


def _modulate(x3, gain, mod_l, shift_idx, scale_idx, router=None):
    n_batch, seq, d = x3.shape
    ts = _pick(seq, (512, 256, 128))
    grid = (n_batch, seq // ts)
    x_spec = pl.BlockSpec((None, ts, d), lambda b, i: (b, i, 0))
    g_spec = pl.BlockSpec((1, d), lambda b, i: (0, 0))
    mod_spec = pl.BlockSpec((None, N_MOD, d), lambda b, i: (b, 0, 0))
    h_spec = pl.BlockSpec((None, ts, d), lambda b, i: (b, i, 0))
    vmem = 2 * ts * d * (4 + 2) + 4 * ts * d * 4
    if router is None:
        h = pl.pallas_call(
            functools.partial(_modulate_body, shift_idx=shift_idx, scale_idx=scale_idx),
            grid=grid,
            in_specs=[x_spec, g_spec, mod_spec],
            out_specs=h_spec,
            out_shape=jax.ShapeDtypeStruct((n_batch, seq, d), BF16),
            compiler_params=_cparams(("parallel", "parallel"), vmem),
            name="modulate",
        )(x3, gain.reshape(1, d), mod_l)
        return h.reshape(n_batch * seq, d)
    n_experts = router.shape[1]
    assert n_experts <= V7X_LANES
    r_pad = jnp.zeros((d, V7X_LANES), F32).at[:, :n_experts].set(router)
    h, route = pl.pallas_call(
        functools.partial(_modulate_route_body, shift_idx=shift_idx, scale_idx=scale_idx, n_experts=n_experts),
        grid=grid,
        in_specs=[x_spec, g_spec, mod_spec, pl.BlockSpec((d, V7X_LANES), lambda b, i: (0, 0))],
        out_specs=[h_spec, pl.BlockSpec((None, ts, V7X_LANES), lambda b, i: (b, i, 0))],
        out_shape=[jax.ShapeDtypeStruct((n_batch, seq, d), F32),
                   jax.ShapeDtypeStruct((n_batch, seq, V7X_LANES), F32)],
        compiler_params=_cparams(("parallel", "parallel"), vmem + 2 * ts * d * 2 + 6 * ts * d * 2),
        name="modulate_route",
    )(x3, gain.reshape(1, d), mod_l, r_pad)
    return h.reshape(n_batch * seq, d), route.reshape(n_batch * seq, V7X_LANES)


def _mm_cast_body(a_ref, w_ref, o_ref):
    o_ref[...] = _dot(a_ref[...], w_ref[...]).astype(o_ref.dtype)


def _mm_cast(a, w, out_dtype=BF16, name="mm"):
    m, k = a.shape
    n = w.shape[1]
    tn = n if n <= 2048 else _pick(n, (1024, 512, 256, 128))
    need = lambda tm_: 2 * (tm_ * k * 2 + k * tn * 2 + tm_ * tn * 2) + tm_ * tn * 4
    tm = next(t_ for t_ in (1024, 512, 256, 128) if m % t_ == 0 and need(t_) * 5 // 4 <= VMEM_CAP)
    vmem = need(tm)
    return pl.pallas_call(
        _mm_cast_body,
        grid=(m // tm, n // tn),
        in_specs=[pl.BlockSpec((tm, k), lambda i, j: (i, 0)),
                  pl.BlockSpec((k, tn), lambda i, j: (0, j))],
        out_specs=pl.BlockSpec((tm, tn), lambda i, j: (i, j)),
        out_shape=jax.ShapeDtypeStruct((m, n), out_dtype),
        compiler_params=_cparams(("parallel", "arbitrary"), vmem),
        name=name,
    )(a, w)


def _mm_swiglu_body(a_ref, wg_ref, wu_ref, o_ref):
    a = a_ref[...]
    g = _dot(a, wg_ref[...])
    u = _dot(a, wu_ref[...])
    o_ref[...] = (_silu(g) * u).astype(o_ref.dtype)


def _mm_swiglu(a, wg, wu, name="swiglu"):
    m, k = a.shape
    n = wg.shape[1]
    tm = _pick(m, (1024, 512, 256, 128))
    tn = _pick(n, (512, 256, 128))
    vmem = 2 * (tm * k * 2 + 2 * k * tn * 2 + tm * tn * 2) + 3 * tm * tn * 4
    return pl.pallas_call(
        _mm_swiglu_body,
        grid=(m // tm, n // tn),
        in_specs=[pl.BlockSpec((tm, k), lambda i, j: (i, 0)),
                  pl.BlockSpec((k, tn), lambda i, j: (0, j)),
                  pl.BlockSpec((k, tn), lambda i, j: (0, j))],
        out_specs=pl.BlockSpec((tm, tn), lambda i, j: (i, j)),
        out_shape=jax.ShapeDtypeStruct((m, n), BF16),
        compiler_params=_cparams(("parallel", "arbitrary"), vmem),
        name=name,
    )(a, wg, wu)


def _mm_residual_body(a_ref, w_ref, x_ref, gate_ref, o_ref, *, nk):
    scale = gate_ref[...]
    part = _dot(a_ref[...], w_ref[...])
    if nk == 1:
        o_ref[...] = x_ref[...] + scale * part
    else:
        kk = pl.program_id(2)

        @pl.when(kk == 0)
        def _():
            o_ref[...] = part

        @pl.when(jnp.logical_and(kk > 0, kk < nk - 1))
        def _():
            o_ref[...] += part

        @pl.when(kk == nk - 1)
        def _():
            o_ref[...] = x_ref[...] + scale * (o_ref[...] + part)


def _mm_residual(a, w, x, gate, seq, name="mm_res"):
    m, k = a.shape
    n = w.shape[1]
    tm = _pick(seq, (1024, 512, 256, 128))
    tn = _pick(n, (1024, 512, 256, 128))
    if k <= 4096:
        tk = k
    else:
        tk = next(t for t in range(4096 // V7X_LANES * V7X_LANES, 0, -V7X_LANES) if k % t == 0)
        if tk < 512:
            tk = next(t for t in range(k // 2 // V7X_LANES * V7X_LANES, 0, -V7X_LANES) if k % t == 0)
    nk = k // tk
    if tk > 4096:
        tn = _pick(n, (512, 256, 128))
    per_batch = seq // tm
    in_specs = [pl.BlockSpec((tm, tk), lambda i, j, kk: (i, kk)),
                pl.BlockSpec((tk, tn), lambda i, j, kk: (kk, j)),
                pl.BlockSpec((tm, tn), lambda i, j, kk: (i, j)),
                pl.BlockSpec((None, 1, tn), lambda i, j, kk: (i // per_batch, 0, j))]
    vmem = 2 * (tm * tk * 2 + tk * tn * 2 + 2 * tm * tn * 4) + 2 * tm * tn * 4
    return pl.pallas_call(
        functools.partial(_mm_residual_body, nk=nk),
        grid=(m // tm, n // tn, nk),
        in_specs=in_specs,
        out_specs=pl.BlockSpec((tm, tn), lambda i, j, kk: (i, j)),
        out_shape=jax.ShapeDtypeStruct((m, n), F32),
        compiler_params=_cparams(("parallel", "parallel", "arbitrary"), vmem),
        name=name,
    )(a, w, x, gate)


def _merge_body(a_ref, b_ref, c_ref, wa_ref, wb_ref, wc_ref, ga_ref, gb_ref, gc_ref, o_ref):
    ya = _dot(a_ref[...], wa_ref[...])
    yb = _dot(b_ref[...], wb_ref[...])
    yc = _dot(c_ref[...], wc_ref[...])
    out = (_sigmoid(ga_ref[...].astype(F32)) * ya + _sigmoid(gb_ref[...].astype(F32)) * yb
           + _sigmoid(gc_ref[...].astype(F32)) * yc)
    o_ref[...] = out.astype(o_ref.dtype)


def _merge(o_a, o_b, o_c, w_a, w_b, w_c, gates):
    m = o_a.shape[0]
    d = w_a.shape[1]
    tm = _pick(m, (1024, 512, 256, 128))
    tn = _pick(d, (512, 256, 128))
    nj = d // tn
    ka, kb, kc = o_a.shape[1], o_b.shape[1], o_c.shape[1]
    vmem = 2 * 2 * (tm * (ka + kb + kc) + (ka + kb + kc) * tn + 4 * tm * tn) + 6 * tm * tn * 4
    return pl.pallas_call(
        _merge_body,
        grid=(m // tm, nj),
        in_specs=[pl.BlockSpec((tm, ka), lambda i, j: (i, 0)),
                  pl.BlockSpec((tm, kb), lambda i, j: (i, 0)),
                  pl.BlockSpec((tm, kc), lambda i, j: (i, 0)),
                  pl.BlockSpec((ka, tn), lambda i, j: (0, j)),
                  pl.BlockSpec((kb, tn), lambda i, j: (0, j)),
                  pl.BlockSpec((kc, tn), lambda i, j: (0, j)),
                  pl.BlockSpec((tm, tn), lambda i, j: (i, j)),
                  pl.BlockSpec((tm, tn), lambda i, j: (i, nj + j)),
                  pl.BlockSpec((tm, tn), lambda i, j: (i, 2 * nj + j))],
        out_specs=pl.BlockSpec((tm, tn), lambda i, j: (i, j)),
        out_shape=jax.ShapeDtypeStruct((m, d), BF16),
        compiler_params=_cparams(("parallel", "arbitrary"), vmem),
        name="merge",
    )(o_a, o_b, o_c, w_a, w_b, w_c, gates, gates, gates)


def _route_metadata(route, n_experts):
    t = route.shape[0]
    a_tot = TOP_K * t
    a_pad = a_tot + n_experts * MOE_TILE
    e = jnp.concatenate([route[:, ROUTE_E1], route[:, ROUTE_E2]]).astype(jnp.int32)
    order = jnp.argsort(e, stable=True).astype(jnp.int32)
    counts = jnp.sum((e[:, None] == jnp.arange(n_experts, dtype=jnp.int32)[None, :]).astype(jnp.int32), axis=0)
    padded = (counts + MOE_TILE - 1) // MOE_TILE * MOE_TILE
    ends_u = jnp.cumsum(counts)
    ends_p = jnp.cumsum(padded)
    start_u = ends_u - counts
    start_p = ends_p - padded
    p = jnp.arange(a_pad, dtype=jnp.int32)
    ep = jnp.minimum(jnp.searchsorted(ends_p, p, side="right"), n_experts - 1).astype(jnp.int32)
    rank = p - start_p[ep]
    valid = jnp.logical_and(rank < counts[ep], p < ends_p[-1])
    src = jnp.clip(start_u[ep] + rank, 0, a_tot - 1)
    row_token = jnp.where(valid, order[src] % t, 0).astype(jnp.int32)
    e_sorted = e[order]
    pos_sorted = jnp.arange(a_tot, dtype=jnp.int32) - start_u[e_sorted] + start_p[e_sorted]
    pos = jnp.zeros((a_tot,), jnp.int32).at[order].set(pos_sorted)
    n_tiles = a_pad // MOE_TILE
    n_valid = (ends_p[-1] // MOE_TILE).astype(jnp.int32)
    tile_start = jnp.arange(n_tiles, dtype=jnp.int32) * MOE_TILE
    tile_e = jnp.minimum(jnp.searchsorted(ends_p, tile_start, side="right"), n_experts - 1).astype(jnp.int32)
    tile_e = jnp.where(tile_start < ends_p[-1], tile_e, tile_e[jnp.maximum(n_valid - 1, 0)])
    return row_token, tile_e, n_valid.reshape(1), pos[:t], pos[t:]


def _row_gather_start(idx_ref, base, src_ref, dst_ref, sem, rows):
    def issue(r, carry):
        pltpu.make_async_copy(src_ref.at[idx_ref[base + r]], dst_ref.at[r], sem).start()
        return carry

    lax.fori_loop(0, rows, issue, 0)


def _row_gather_wait(src_ref, dst_ref, sem, rows):
    pltpu.make_async_copy(src_ref.at[pl.ds(0, rows)], dst_ref, sem).wait()


def _gather_rows_body(idx_ref, src_ref, o_ref, sem, *, rows):
    base = pl.program_id(0) * rows
    _row_gather_start(idx_ref, base, src_ref, o_ref, sem, rows)
    _row_gather_wait(src_ref, o_ref, sem, rows)


def _gather_rows(src, idx):
    m = idx.shape[0]
    w = src.shape[1]
    rows = GATHER_ROWS
    assert m % rows == 0 and src.shape[0] >= rows
    return pl.pallas_call(
        functools.partial(_gather_rows_body, rows=rows),
        grid_spec=pltpu.PrefetchScalarGridSpec(
            num_scalar_prefetch=1,
            grid=(m // rows,),
            in_specs=[pl.BlockSpec(memory_space=pl.ANY)],
            out_specs=pl.BlockSpec((rows, w), lambda i, idx_ref: (i, 0)),
            scratch_shapes=[pltpu.SemaphoreType.DMA(())]),
        out_shape=jax.ShapeDtypeStruct((m, w), src.dtype),
        compiler_params=_cparams(("arbitrary",), 2 * rows * w * 4),
        name="moe_gather",
    )(idx, src)


def _gmm_swiglu_body(te_ref, nv_ref, x_ref, wg_ref, wu_ref, o_ref):
    valid = pl.program_id(0) < nv_ref[0]

    @pl.when(valid)
    def _():
        a = x_ref[...].astype(BF16)
        g = _dot(a, wg_ref[...])
        u = _dot(a, wu_ref[...])
        o_ref[...] = (_silu(g) * u).astype(o_ref.dtype)

    @pl.when(jnp.logical_not(valid))
    def _():
        o_ref[...] = jnp.zeros_like(o_ref)


def _gmm_down_body(te_ref, nv_ref, a_ref, w_ref, o_ref):
    valid = pl.program_id(0) < nv_ref[0]

    @pl.when(valid)
    def _():
        o_ref[...] = _dot(a_ref[...], w_ref[...])

    @pl.when(jnp.logical_not(valid))
    def _():
        o_ref[...] = jnp.zeros_like(o_ref)


def _gmm_maps(nj):
    def rows(i, j, te, nv):
        return (jnp.minimum(i, nv[0] - 1), 0)

    def weights(i, j, te, nv):
        return (te[i], 0, jnp.where(i < nv[0], j, nj - 1))

    def out(i, j, te, nv):
        return (i, j)

    return rows, weights, out


def _gmm_swiglu(x_sorted, wg, wu, tile_e, n_valid):
    m, k = x_sorted.shape
    n = wg.shape[2]
    tm = MOE_TILE
    tn = _pick(n, (512, 256, 128))
    nj = n // tn
    rows, weights, out = _gmm_maps(nj)
    vmem = 2 * (tm * k * 4 + 2 * k * tn * 2 + tm * tn * 2) + tm * k * 2 + 3 * tm * tn * 4
    return pl.pallas_call(
        _gmm_swiglu_body,
        grid_spec=pltpu.PrefetchScalarGridSpec(
            num_scalar_prefetch=2,
            grid=(m // tm, nj),
            in_specs=[pl.BlockSpec((tm, k), rows),
                      pl.BlockSpec((None, k, tn), weights),
                      pl.BlockSpec((None, k, tn), weights)],
            out_specs=pl.BlockSpec((tm, tn), out)),
        out_shape=jax.ShapeDtypeStruct((m, n), BF16),
        compiler_params=_cparams(("arbitrary", "arbitrary"), vmem),
        name="moe_up",
    )(tile_e, n_valid, x_sorted, wg, wu)


def _gmm_down(a_sorted, wd, tile_e, n_valid):
    m, k = a_sorted.shape
    n = wd.shape[2]
    tm = MOE_TILE
    tn = _pick(n, (1024, 512, 256, 128))
    nj = n // tn
    rows, weights, out = _gmm_maps(nj)
    vmem = 2 * (tm * k * 2 + k * tn * 2 + tm * tn * 4) + tm * tn * 4
    return pl.pallas_call(
        _gmm_down_body,
        grid_spec=pltpu.PrefetchScalarGridSpec(
            num_scalar_prefetch=2,
            grid=(m // tm, nj),
            in_specs=[pl.BlockSpec((tm, k), rows),
                      pl.BlockSpec((None, k, tn), weights)],
            out_specs=pl.BlockSpec((tm, tn), out)),
        out_shape=jax.ShapeDtypeStruct((m, n), F32),
        compiler_params=_cparams(("arbitrary", "arbitrary"), vmem),
        name="moe_down",
    )(tile_e, n_valid, a_sorted, wd)


def _moe_combine_body(p1_ref, p2_ref, y_ref, x_ref, gate_ref, route_ref, o_ref, buf1, buf2, sem, *, rows):
    base = pl.program_id(0) * rows
    _row_gather_start(p1_ref, base, y_ref, buf1, sem.at[0], rows)
    _row_gather_start(p2_ref, base, y_ref, buf2, sem.at[1], rows)
    _row_gather_wait(y_ref, buf1, sem.at[0], rows)
    _row_gather_wait(y_ref, buf2, sem.at[1], rows)
    route = route_ref[...]
    w1 = route[:, ROUTE_W1:ROUTE_W1 + 1]
    w2 = route[:, ROUTE_W2:ROUTE_W2 + 1]
    o_ref[...] = x_ref[...] + gate_ref[...] * (w1 * buf1[...] + w2 * buf2[...])


def _moe_combine(y_sorted, pos1, pos2, x, gate, route, seq):
    t, d = x.shape
    rows = _pick(seq, (128,))
    per_batch = seq // rows
    return pl.pallas_call(
        functools.partial(_moe_combine_body, rows=rows),
        grid_spec=pltpu.PrefetchScalarGridSpec(
            num_scalar_prefetch=2,
            grid=(t // rows,),
            in_specs=[pl.BlockSpec(memory_space=pl.ANY),
                      pl.BlockSpec((rows, d), lambda i, p1, p2: (i, 0)),
                      pl.BlockSpec((None, 1, d), lambda i, p1, p2: (i // per_batch, 0, 0)),
                      pl.BlockSpec((rows, V7X_LANES), lambda i, p1, p2: (i, 0))],
            out_specs=pl.BlockSpec((rows, d), lambda i, p1, p2: (i, 0)),
            scratch_shapes=[pltpu.VMEM((rows, d), F32), pltpu.VMEM((rows, d), F32),
                            pltpu.SemaphoreType.DMA((2,))]),
        out_shape=jax.ShapeDtypeStruct((t, d), F32),
        compiler_params=_cparams(("arbitrary",), 8 * rows * d * 4),
        name="moe_combine",
    )(pos1, pos2, y_sorted, x, gate, route)


_HG_LEVELS = (HG_CHUNK // HG_SUB).bit_length() - 1


def _hgrn_level_masks():
    ti = np.arange(HG_CHUNK)[:, None]
    si = np.arange(HG_CHUNK)[None, :]
    out = []
    for lvl in range(_HG_LEVELS):
        half = HG_SUB << lvl
        blk = 2 * half
        out.append((ti // blk == si // blk) & (ti % blk >= half) & (si % blk < half))
    return jnp.asarray(np.stack(out), F32)


def _hgrn_chunk(q_in, f_in, v, g_in, lb, gain, state_t, tri, ones, lvl_mask_ref):
    c = HG_CHUNK
    fg = lb + (1.0 - lb) * _sigmoid(f_in)
    log_f = jnp.log(jnp.maximum(fg, MIN_FORGET))
    k = 1.0 - fg
    q = _silu(q_in)
    lf_hi, lf_mid, lf_lo = _split3(log_f)
    b = _dot(tri, lf_hi) + _dot(tri, lf_mid) + _dot(tri, lf_lo)
    b_last = b[c - 1:c, :]

    o = _dot_nt((q * jnp.exp(b)).astype(BF16), state_t.astype(BF16))

    row = lax.broadcasted_iota(jnp.int32, (c, 1), 0)
    scores = jnp.zeros((c, c), F32)
    for lvl in range(_HG_LEVELS):
        half = HG_SUB << lvl
        blk = 2 * half
        bref = jnp.concatenate(
            [jnp.broadcast_to(b[p * blk + half - 1:p * blk + half, :], (blk, HG_DK)) for p in range(c // blk)],
            axis=0)
        is_q = (row & half) != 0
        e = jnp.exp(-jnp.abs(b - bref))
        xk = jnp.where(is_q, q, k) * e
        qd = jnp.where(is_q, xk, 0.0).astype(BF16)
        kd = jnp.where(is_q, 0.0, xk).astype(BF16)
        scores = scores + _dot_nt(qd, kd) * lvl_mask_ref[lvl]
    o = o + _dot(scores.astype(BF16), v.astype(BF16))

    nb = c // HG_SUB
    b3 = b.reshape(nb, HG_SUB, HG_DK)
    q3 = q.reshape(nb, HG_SUB, HG_DK)
    k3 = k.reshape(nb, HG_SUB, HG_DK)
    v3 = v.reshape(nb, HG_SUB, HG_DK)
    t_in = lax.broadcasted_iota(jnp.int32, (nb, HG_SUB, HG_DK), 1)
    for s in range(HG_SUB):
        diff = b3 - b3[:, s:s + 1, :]
        dec = jnp.where(t_in >= s, jnp.exp(jnp.minimum(diff, 0.0)), 0.0)
        m = (q3 * (k3[:, s:s + 1, :] * dec)).reshape(c, HG_DK)
        m_hi, m_lo = _split2(m)
        r = _dot(m_hi, ones) + _dot(m_lo, ones)
        o = o + r * jnp.broadcast_to(v3[:, s:s + 1, :], (nb, HG_SUB, HG_DK)).reshape(c, HG_DK)

    kdec = (k * jnp.exp(b_last - b)).astype(BF16)
    new_state_t = state_t * jnp.exp(b_last) + _dot_tn(v.astype(BF16), kdec)

    ms = jnp.mean(o * o, axis=-1, keepdims=True)
    out = o * lax.rsqrt(ms + NORM_EPS) * gain * _silu(g_in)
    return out, new_state_t


def _hgrn_body(q_ref, f_ref, i_ref, g_ref, lbraw_ref, gain_ref, lvl_mask_ref, o_ref, state_ref, *, layer, n_chunks):
    @pl.when(pl.program_id(2) == 0)
    def _():
        state_ref[...] = jnp.zeros_like(state_ref)

    lbr = lbraw_ref[...]
    ex = jnp.exp(lbr - jnp.max(lbr, axis=0, keepdims=True))
    soft = ex / jnp.sum(ex, axis=0, keepdims=True)
    lb = jnp.zeros((1, HG_DK), F32)
    for j in range(1, layer + 1):
        lb = lb + soft[j:j + 1, :]
    gain = gain_ref[...]
    c = HG_CHUNK
    tri = (lax.broadcasted_iota(jnp.int32, (c, c), 0) >= lax.broadcasted_iota(jnp.int32, (c, c), 1)).astype(BF16)
    ones = jnp.ones((HG_DK, HG_DK), BF16)

    def chunk(ci, carry):
        sl = pl.ds(pl.multiple_of(ci * c, c), c)
        out, new_state = _hgrn_chunk(q_ref[sl, :].astype(F32), f_ref[sl, :].astype(F32),
                                     i_ref[sl, :].astype(F32), g_ref[sl, :].astype(F32),
                                     lb, gain, state_ref[...], tri, ones, lvl_mask_ref)
        o_ref[sl, :] = out.astype(o_ref.dtype)
        state_ref[...] = new_state
        return carry

    lax.fori_loop(0, n_chunks, chunk, 0)


def _hgrn(proj, lb_raw, out_gain, layer, n_batch, seq, width):
    heads = width // HG_DK
    lc = _pick(seq, (512, 256, 128))
    per_batch = seq // lc
    depth = lb_raw.shape[0]

    def col(off):
        return pl.BlockSpec((lc, HG_DK), lambda b, h, i: (b * per_batch + i, off * heads + h))

    return pl.pallas_call(
        functools.partial(_hgrn_body, layer=layer, n_chunks=lc // HG_CHUNK),
        grid=(n_batch, heads, per_batch),
        in_specs=[col(0), col(1), col(2), col(3),
                  pl.BlockSpec((depth, HG_DK), lambda b, h, i: (0, h)),
                  pl.BlockSpec((1, HG_DK), lambda b, h, i: (0, 0)),
                  pl.BlockSpec((_HG_LEVELS, HG_CHUNK, HG_CHUNK), lambda b, h, i: (0, 0, 0))],
        out_specs=pl.BlockSpec((lc, HG_DK), lambda b, h, i: (b * per_batch + i, h)),
        out_shape=jax.ShapeDtypeStruct((n_batch * seq, width), BF16),
        scratch_shapes=[pltpu.VMEM((HG_DK, HG_DK), F32)],
        compiler_params=_cparams(("parallel", "parallel", "arbitrary"), 16 * 1024 * 1024),
        name="hgrn2",
    )(proj, proj, proj, proj, lb_raw, out_gain.reshape(1, HG_DK), _hgrn_level_masks())


def _pool_body(u_ref, halo_ref, w_ref, scale_ref, o_ref, ext_ref, *, ts, gdim):
    i = pl.program_id(1)
    halo = halo_ref[...].astype(F32)
    ext_ref[0:POOL_HALO, :] = jnp.where(i > 0, halo, 0.0)
    ext_ref[POOL_HALO:, :] = u_ref[...].astype(F32)
    pos = (i * ts + lax.broadcasted_iota(jnp.int32, (ts, 1), 0) + 1).astype(F32)
    for g, win in enumerate(POOL_WINDOWS):
        cs = slice(g * gdim, (g + 1) * gdim)
        acc = ext_ref[POOL_HALO:, cs]
        for j in range(1, win):
            acc = acc + ext_ref[POOL_HALO - j:POOL_HALO - j + ts, cs]
        pooled = acc / jnp.minimum(pos, float(win)) - ext_ref[POOL_HALO:, cs]
        y = _dot(pooled.astype(BF16), w_ref[g])
        o_ref[:, cs] = (y * scale_ref[:, cs]).astype(o_ref.dtype)


def _pool(proj, col_off, pool_w, pool_scale, n_batch, seq, width):
    groups = len(POOL_WINDOWS)
    gdim = width // groups
    assert gdim % V7X_LANES == 0 and col_off % width == 0 and max(POOL_WINDOWS) <= POOL_HALO
    ts = _pick(seq, (512, 256, 128))
    per_batch = seq // ts
    cb = col_off // width
    hb = ts // POOL_HALO
    return pl.pallas_call(
        functools.partial(_pool_body, ts=ts, gdim=gdim),
        grid=(n_batch, per_batch),
        in_specs=[pl.BlockSpec((ts, width), lambda b, i: (b * per_batch + i, cb)),
                  pl.BlockSpec((POOL_HALO, width),
                               lambda b, i: (jnp.maximum((b * per_batch + i) * hb - 1, 0), cb)),
                  pl.BlockSpec((groups, gdim, gdim), lambda b, i: (0, 0, 0)),
                  pl.BlockSpec((1, width), lambda b, i: (0, 0))],
        out_specs=pl.BlockSpec((ts, width), lambda b, i: (b * per_batch + i, 0)),
        out_shape=jax.ShapeDtypeStruct((n_batch * seq, width), BF16),
        scratch_shapes=[pltpu.VMEM((ts + POOL_HALO, width), F32)],
        compiler_params=_cparams(("parallel", "parallel"), 16 * 1024 * 1024),
        name="pool",
    )(proj, proj, pool_w, pool_scale.reshape(1, width))


def _rope_table_body(pos_ref, cos_ref, sin_ref):
    pos = pos_ref[...].astype(F32)
    lane = lax.broadcasted_iota(jnp.int32, (1, V7X_LANES), 1)
    j = lane % MLA_ROPE
    fidx = (j % (MLA_ROPE // 2)).astype(F32)
    inv_freq = jnp.exp(fidx * (-2.0 / MLA_ROPE * math.log(ROPE_THETA)))
    ang = pos * inv_freq
    cos_ref[...] = jnp.cos(ang)
    sin_ref[...] = jnp.where(j < MLA_ROPE // 2, -1.0, 1.0) * jnp.sin(ang)


def _rope_table(positions):
    t = positions.size
    ts = _pick(t, (512, 256, 128))
    return pl.pallas_call(
        _rope_table_body,
        grid=(t // ts,),
        in_specs=[pl.BlockSpec((ts, 1), lambda i: (i, 0))],
        out_specs=[pl.BlockSpec((ts, V7X_LANES), lambda i: (i, 0))] * 2,
        out_shape=[jax.ShapeDtypeStruct((t, V7X_LANES), F32)] * 2,
        compiler_params=_cparams(("parallel",), 4 * 1024 * 1024),
        name="rope_table",
    )(positions.reshape(t, 1))


def _rms(x, gain):
    return x * lax.rsqrt(jnp.mean(x * x, axis=-1, keepdims=True) + NORM_EPS) * gain


def _prep_q_body(cq_ref, qn_ref, w_ref, gn_ref, gr_ref, grs_ref, seg_ref, cos_ref, sin_ref, o_ref,
                 *, heads, sm_scale):
    hn = heads * MLA_NOPE
    hr = heads * MLA_ROPE
    hq = _rms(cq_ref[...].astype(F32), qn_ref[...]).astype(BF16)
    y = _dot(hq, w_ref[...])
    yr = y[:, hn:hn + hr]
    ys = y[:, hn + hr:]
    sq_hi, sq_lo = _split2(yr * yr)
    seg = seg_ref[...]
    ss = _dot(sq_hi, seg) + _dot(sq_lo, seg)
    inv = lax.rsqrt(ss * (1.0 / MLA_ROPE) + NORM_EPS)
    reps = hr // V7X_LANES
    cosf = jnp.concatenate([cos_ref[...]] * reps, axis=1)
    sinf = jnp.concatenate([sin_ref[...]] * reps, axis=1)
    qr = (yr * inv * gr_ref[...]) * cosf + (ys * inv * grs_ref[...]) * sinf
    gn = gn_ref[...] * sm_scale
    for h in range(heads):
        qn = _rms(y[:, h * MLA_NOPE:(h + 1) * MLA_NOPE], gn)
        o_ref[h, :, 0:MLA_NOPE] = qn.astype(o_ref.dtype)
        o_ref[h, :, MLA_NOPE:] = (qr[:, h * MLA_ROPE:(h + 1) * MLA_ROPE] * sm_scale).astype(o_ref.dtype)


def _prep_kv_body(ckv_ref, kpe_ref, kvn_ref, w_ref, gn_ref, gr2_ref, cos_ref, sin_ref, k_ref, v_ref, *, heads):
    hk = _rms(ckv_ref[...].astype(F32), kvn_ref[...]).astype(BF16)
    y = _dot(hk, w_ref[...])
    kp = kpe_ref[...].astype(F32)
    kpe = kp[:, :MLA_ROPE]
    inv = lax.rsqrt(jnp.mean(kpe * kpe, axis=-1, keepdims=True) + NORM_EPS)
    kn = kp * inv * gr2_ref[...]
    kr = (kn[:, :MLA_ROPE] * cos_ref[:, :MLA_ROPE] + kn[:, MLA_ROPE:] * sin_ref[:, :MLA_ROPE]).astype(k_ref.dtype)
    per = MLA_NOPE + MLA_V
    for h in range(heads):
        k_ref[h, :, 0:MLA_NOPE] = _rms(y[:, h * per:h * per + MLA_NOPE], gn_ref[...]).astype(k_ref.dtype)
        k_ref[h, :, MLA_NOPE:] = kr
        v_ref[h, 0, :MLA_V, :] = y[:, h * per + MLA_NOPE:(h + 1) * per].T.astype(v_ref.dtype)
        v_ref[h, 0, MLA_V:, :] = jnp.ones((V7X_BF16_ROWS, y.shape[0]), v_ref.dtype)


def _flash_body(q_ref, k_ref, vt_ref, o_ref, acc_ref, *, tile, group):
    qi = pl.program_id(2)
    acc_ref[...] = jnp.zeros_like(acc_ref)

    def step(j, ms, masked):
        ks = pl.ds(pl.multiple_of(j * tile, tile), tile)
        sts = [_dot_nt(k_ref[g, ks, :], q_ref[g]) for g in range(group)]
        if masked:
            kidx = lax.broadcasted_iota(jnp.int32, (tile, tile), 0)
            qidx = lax.broadcasted_iota(jnp.int32, (tile, tile), 1)
            sts = [jnp.where(kidx <= qidx, st, NEG_BIG) for st in sts]
        m_new = [jnp.maximum(m, jnp.max(st, axis=0, keepdims=True)) for m, st in zip(ms, sts)]
        ps = [jnp.exp2(st - m).astype(BF16) for st, m in zip(sts, m_new)]
        for g in range(group):
            acc_ref[g] = jnp.exp2(ms[g] - m_new[g]) * acc_ref[g] + _dot(vt_ref[g, j], ps[g])
        return tuple(m_new)

    m0 = tuple(jnp.full((1, tile), NEG_BIG, F32) for _ in range(group))
    ms = lax.fori_loop(0, qi, lambda j, c: step(j, c, False), m0)
    step(qi, ms, True)
    for g in range(group):
        o = acc_ref[g, :MLA_V, :] / acc_ref[g, MLA_V:MLA_V + 1, :]
        o_ref[:, g * MLA_V:(g + 1) * MLA_V] = o.T.astype(o_ref.dtype)


def _mla(proj_c, cos_t, sin_t, q_norm, w_q, kv_norm, w_kv, g_qn, g_qr, g_kn, g_kr,
         n_batch, seq, heads, q_rank, kv_rank):
    t = n_batch * seq
    half = MLA_ROPE // 2
    hn, hr = heads * MLA_NOPE, heads * MLA_ROPE
    dq = MLA_NOPE + MLA_ROPE
    sm_scale = float(dq) ** -0.5 * math.log2(math.e)
    assert q_rank % kv_rank == 0 and (q_rank + kv_rank) % V7X_LANES == 0 and hr % V7X_LANES == 0
    tile = ATTN_TILE
    assert seq % tile == 0
    tm = tile
    vrows = MLA_V + V7X_BF16_ROWS

    def swap(g):
        return jnp.concatenate([g[half:], g[:half]])

    gr = jnp.tile(g_qr, heads).reshape(1, hr)
    grs = jnp.tile(swap(g_qr), heads).reshape(1, hr)
    lane = jnp.arange(hr) // MLA_ROPE
    seg = (lane[:, None] == lane[None, :]).astype(BF16)
    row = lambda i: (i, 0)
    const2 = lambda i: (0, 0)
    qcat = pl.pallas_call(
        functools.partial(_prep_q_body, heads=heads, sm_scale=sm_scale),
        grid=(t // tm,),
        in_specs=[pl.BlockSpec((tm, q_rank), row),
                  pl.BlockSpec((1, q_rank), const2),
                  pl.BlockSpec((q_rank, hn + 2 * hr), const2),
                  pl.BlockSpec((1, MLA_NOPE), const2),
                  pl.BlockSpec((1, hr), const2),
                  pl.BlockSpec((1, hr), const2),
                  pl.BlockSpec((hr, hr), const2),
                  pl.BlockSpec((tm, V7X_LANES), row),
                  pl.BlockSpec((tm, V7X_LANES), row)],
        out_specs=pl.BlockSpec((heads, tm, dq), lambda i: (0, i, 0)),
        out_shape=jax.ShapeDtypeStruct((heads, t, dq), BF16),
        compiler_params=_cparams(("parallel",), 40 * 1024 * 1024),
        name="mla_prep_q",
    )(proj_c, q_norm.reshape(1, q_rank), w_q, g_qn.reshape(1, MLA_NOPE), gr, grs, seg, cos_t, sin_t)

    gr2 = jnp.concatenate([g_kr, swap(g_kr)]).reshape(1, 2 * MLA_ROPE)
    kcat, vt = pl.pallas_call(
        functools.partial(_prep_kv_body, heads=heads),
        grid=(t // tm,),
        in_specs=[pl.BlockSpec((tm, kv_rank), lambda i: (i, q_rank // kv_rank)),
                  pl.BlockSpec((tm, V7X_LANES), lambda i: (i, (q_rank + kv_rank) // V7X_LANES)),
                  pl.BlockSpec((1, kv_rank), const2),
                  pl.BlockSpec((kv_rank, heads * (MLA_NOPE + MLA_V)), const2),
                  pl.BlockSpec((1, MLA_NOPE), const2),
                  pl.BlockSpec((1, 2 * MLA_ROPE), const2),
                  pl.BlockSpec((tm, V7X_LANES), row),
                  pl.BlockSpec((tm, V7X_LANES), row)],
        out_specs=[pl.BlockSpec((heads, tm, dq), lambda i: (0, i, 0)),
                   pl.BlockSpec((heads, 1, vrows, tile), lambda i: (0, i, 0, 0))],
        out_shape=[jax.ShapeDtypeStruct((heads, t, dq), BF16),
                   jax.ShapeDtypeStruct((heads, t // tile, vrows, tile), BF16)],
        compiler_params=_cparams(("parallel",), 40 * 1024 * 1024),
        name="mla_prep_kv",
    )(proj_c, proj_c, kv_norm.reshape(1, kv_rank), w_kv, g_kn.reshape(1, MLA_NOPE), gr2, cos_t, sin_t)

    nq = seq // tile
    group = 2 if heads % 2 == 0 else 1
    return pl.pallas_call(
        functools.partial(_flash_body, tile=tile, group=group),
        grid=(heads // group, n_batch, nq),
        in_specs=[pl.BlockSpec((group, tile, dq), lambda h, b, i: (h, b * nq + i, 0)),
                  pl.BlockSpec((group, seq, dq), lambda h, b, i: (h, b, 0)),
                  pl.BlockSpec((group, nq, vrows, tile), lambda h, b, i: (h, b, 0, 0))],
        out_specs=pl.BlockSpec((tile, group * MLA_V), lambda h, b, i: (b * nq + i, h)),
        out_shape=jax.ShapeDtypeStruct((t, heads * MLA_V), BF16),
        scratch_shapes=[pltpu.VMEM((group, vrows, tile), F32)],
        compiler_params=_cparams(("parallel", "parallel", "arbitrary"), 40 * 1024 * 1024),
        name="mla_flash",
    )(qcat, kcat, vt)


def kernel(x, c, positions, ada_w, ada_b, ada_layer, mix_norm, ffn_norm, w_in, hgrn_lower_bounds,
           hgrn_out_norm, pool_w, pool_scale, mla_q_norm, mla_w_uq, mla_kv_norm, mla_w_ukv,
           mla_qk_norm_q_nope, mla_qk_norm_q_rope, mla_qk_norm_k_nope, mla_qk_norm_k_rope,
           w_branch_a, w_branch_b, w_branch_c, w_o, ffn_w_gate, ffn_w_up, ffn_w_down,
           moe_router, moe_w_gate, moe_w_up, moe_w_down):
    n_batch, seq, d = x.shape
    depth = w_in.shape[0]
    t = n_batch * seq
    hg_width = hgrn_lower_bounds.shape[1]
    pool_width = pool_scale.shape[1]
    q_rank = mla_q_norm.shape[1]
    kv_rank = mla_kv_norm.shape[1]
    heads = mla_w_ukv.shape[2] // (MLA_NOPE + MLA_V)
    half = MLA_ROPE // 2
    n_a = 4 * hg_width + pool_width
    n_c = q_rank + kv_rank + MLA_ROPE
    assert w_in.shape[2] == n_a + n_c + 3 * d

    mod = _ada(c, ada_w, ada_b, ada_layer)
    cos_t, sin_t = _rope_table(positions)
    xf = x.reshape(t, d)

    for l in range(depth):
        mod_l = mod[l]
        gate1 = mod_l[:, 2:3, :]
        gate2 = mod_l[:, 5:6, :]

        w_l = w_in[l]
        w_a = w_l[:, :n_a].astype(BF16)
        kpe_w = w_l[:, n_a + q_rank + kv_rank:n_a + n_c]
        w_c = jnp.concatenate([w_l[:, n_a:n_a + n_c], kpe_w[:, half:], kpe_w[:, :half]], axis=1).astype(BF16)
        w_g = w_l[:, n_a + n_c:].astype(BF16)
        wq = mla_w_uq[l].reshape(q_rank, heads, MLA_NOPE + MLA_ROPE)
        wq_r = wq[:, :, MLA_NOPE:]
        w_q = jnp.concatenate([
            wq[:, :, :MLA_NOPE].reshape(q_rank, heads * MLA_NOPE),
            wq_r.reshape(q_rank, heads * MLA_ROPE),
            jnp.concatenate([wq_r[:, :, half:], wq_r[:, :, :half]], axis=2).reshape(q_rank, heads * MLA_ROPE),
        ], axis=1).astype(BF16)

        h = _modulate(xf.reshape(n_batch, seq, d), mix_norm[l], mod_l, 0, 1)
        proj_a = _mm_cast(h, w_a, name="proj_a")
        proj_c = _mm_cast(h, w_c, name="proj_c")
        gates = _mm_cast(h, w_g, name="proj_gates")

        o_a = _hgrn(proj_a, hgrn_lower_bounds, hgrn_out_norm[l], l, n_batch, seq, hg_width)
        o_b = _pool(proj_a, 4 * hg_width, pool_w[l].astype(BF16), pool_scale[l], n_batch, seq, pool_width)
        o_c = _mla(proj_c, cos_t, sin_t, mla_q_norm[l], w_q, mla_kv_norm[l], mla_w_ukv[l].astype(BF16),
                   mla_qk_norm_q_nope[l], mla_qk_norm_q_rope[l], mla_qk_norm_k_nope[l], mla_qk_norm_k_rope[l],
                   n_batch, seq, heads, q_rank, kv_rank)
        merged = _merge(o_a, o_b, o_c, w_branch_a[l].astype(BF16), w_branch_b[l].astype(BF16),
                        w_branch_c[l].astype(BF16), gates)
        xf = _mm_residual(merged, w_o[l].astype(BF16), xf, gate1, seq, name="out_proj")

        j = l // 2
        if l % 2 == 0:
            h = _modulate(xf.reshape(n_batch, seq, d), ffn_norm[l], mod_l, 3, 4)
            act = _mm_swiglu(h, ffn_w_gate[j].astype(BF16), ffn_w_up[j].astype(BF16), name="ffn_up")
            xf = _mm_residual(act, ffn_w_down[j].astype(BF16), xf, gate2, seq, name="ffn_down")
        else:
            n_experts = moe_w_gate.shape[1]
            h, route = _modulate(xf.reshape(n_batch, seq, d), ffn_norm[l], mod_l, 3, 4, router=moe_router[j])
            row_token, tile_e, n_valid, pos1, pos2 = _route_metadata(route, n_experts)
            h_sorted = _gather_rows(h, row_token)
            act = _gmm_swiglu(h_sorted, moe_w_gate[j].astype(BF16), moe_w_up[j].astype(BF16), tile_e, n_valid)
            y_sorted = _gmm_down(act, moe_w_down[j].astype(BF16), tile_e, n_valid)
            xf = _moe_combine(y_sorted, pos1, pos2, xf, gate2, route, seq)
    return xf.reshape(n_batch, seq, d)
```

```python
import functools
import math

import numpy as np
import jax
import jax.numpy as jnp
from jax import lax
from jax.experimental import pallas as pl
from jax.experimental.pallas import tpu as pltpu

F32 = jnp.float32
BF16 = jnp.bfloat16

HG_DK = 128
POOL_WINDOWS = (2, 4, 8, 16)
MLA_NOPE = 128
MLA_ROPE = 64
MLA_V = 128
ROPE_THETA = 10000.0
MIN_FORGET = 1e-30
NORM_EPS = 1e-6
N_MOD = 6
TOP_K = 2
NEG_BIG = -1e30

V7X_LANES = 128
V7X_SUBLANES = 8
V7X_BF16_ROWS = 16
V7X_VMEM_BYTES = 64 * 1024 * 1024
VMEM_CAP = V7X_VMEM_BYTES - 8 * 1024 * 1024

HG_CHUNK = 128
HG_SUB = 8
POOL_HALO = 16
ATTN_TILE = 1024
ATTN_GROUP = 2
ROUTE_E1, ROUTE_E2, ROUTE_W1, ROUTE_W2 = 0, 1, 2, 3
MOE_TILE = 512
GATHER_ROWS = 512


def _pick(n, prefs):
    for p in prefs:
        if n % p == 0:
            return p
    raise ValueError(f"no tile in {prefs} divides {n}")


def _cparams(sem, vmem_bytes):
    limit = int(min(VMEM_CAP, max(32 * 1024 * 1024, vmem_bytes * 5 // 4)))
    return pltpu.CompilerParams(dimension_semantics=sem, vmem_limit_bytes=limit)


def _sigmoid(x):
    return 1.0 / (1.0 + jnp.exp(-x))


def _silu(x):
    return x * _sigmoid(x)


def _dot(a, b):
    return jnp.dot(a, b, preferred_element_type=F32)


def _dot_nt(a, b):
    return lax.dot_general(a, b, (((1,), (1,)), ((), ())), preferred_element_type=F32)


def _dot_tn(a, b):
    return lax.dot_general(a, b, (((0,), (0,)), ((), ())), preferred_element_type=F32)


def _split3(x):
    hi = x.astype(BF16)
    r1 = x - hi.astype(F32)
    mid = r1.astype(BF16)
    lo = (r1 - mid.astype(F32)).astype(BF16)
    return hi, mid, lo


def _split2(x):
    hi = x.astype(BF16)
    lo = (x - hi.astype(F32)).astype(BF16)
    return hi, lo


def _ada_body(ct_ref, w_ref, b_ref, lay_ref, o_ref, *, n_batch, depth):
    kk = pl.program_id(1)
    w = w_ref[...]
    s = _silu(ct_ref[...])
    for b in range(n_batch):
        r = jnp.sum(w * s[:, b:b + 1], axis=0, keepdims=True)

        @pl.when(kk == 0)
        def _():
            for l in range(depth):
                o_ref[l, b:b + 1, :] = r + b_ref[...] + lay_ref[l:l + 1, :]

        @pl.when(kk > 0)
        def _():
            for l in range(depth):
                o_ref[l, b:b + 1, :] += r


def _ada(c, ada_w, ada_b, ada_layer):
    n_batch, d = c.shape
    depth = ada_layer.shape[0]
    n = ada_w.shape[1]
    tn = _pick(n, (2048, 1024, 512, 256, 128))
    tk = _pick(d, (1024, 512, 256, 128))
    ct = c.T
    lay = ada_layer.reshape(depth, n)
    out = pl.pallas_call(
        functools.partial(_ada_body, n_batch=n_batch, depth=depth),
        grid=(n // tn, d // tk),
        in_specs=[
            pl.BlockSpec((tk, n_batch), lambda j, k: (k, 0)),
            pl.BlockSpec((tk, tn), lambda j, k: (k, j)),
            pl.BlockSpec((1, tn), lambda j, k: (0, j)),
            pl.BlockSpec((depth, tn), lambda j, k: (0, j)),
        ],
        out_specs=pl.BlockSpec((depth, n_batch, tn), lambda j, k: (0, 0, j)),
        out_shape=jax.ShapeDtypeStruct((depth, n_batch, n), F32),
        compiler_params=_cparams(("parallel", "arbitrary"), 4 * tk * tn * 4),
        name="ada",
    )(ct, ada_w, ada_b.reshape(1, n), lay)
    return out.reshape(depth, n_batch, N_MOD, d)


def _modulated(x_ref, g_ref, mod_ref, shift_idx, scale_idx):
    x = x_ref[...]
    ms = jnp.mean(x * x, axis=-1, keepdims=True)
    y = x * lax.rsqrt(ms + NORM_EPS) * g_ref[...]
    return y * (1.0 + mod_ref[scale_idx:scale_idx + 1, :]) + mod_ref[shift_idx:shift_idx + 1, :]


def _modulate_body(x_ref, g_ref, mod_ref, o_ref, *, shift_idx, scale_idx):
    o_ref[...] = _modulated(x_ref, g_ref, mod_ref, shift_idx, scale_idx).astype(o_ref.dtype)


def _modulate_route_body(x_ref, g_ref, mod_ref, r_ref, o_ref, route_ref, *, shift_idx, scale_idx, n_experts):
    h = _modulated(x_ref, g_ref, mod_ref, shift_idx, scale_idx)
    o_ref[...] = h
    h_hi, h_mid, h_lo = _split3(h)
    r = r_ref[...]
    r_hi, r_mid, r_lo = _split3(r)
    logits = (_dot(h_hi, r_hi) + _dot(h_hi, r_mid) + _dot(h_mid, r_hi)
              + _dot(h_hi, r_lo) + _dot(h_mid, r_mid) + _dot(h_lo, r_hi))
    lane = lax.broadcasted_iota(jnp.int32, logits.shape, 1).astype(F32)
    lg = jnp.where(lane < n_experts, logits, -jnp.inf)
    m1 = jnp.max(lg, axis=-1, keepdims=True)
    i1 = jnp.min(jnp.where(lg == m1, lane, float(V7X_LANES)), axis=-1, keepdims=True)
    lg2 = jnp.where(lane == i1, -jnp.inf, lg)
    m2 = jnp.max(lg2, axis=-1, keepdims=True)
    i2 = jnp.min(jnp.where(lg2 == m2, lane, float(V7X_LANES)), axis=-1, keepdims=True)
    e2 = jnp.exp(m2 - m1)
    w1 = 1.0 / (1.0 + e2)
    w2 = e2 / (1.0 + e2)
    route_ref[...] = (jnp.where(lane == ROUTE_E1, i1, 0.0) + jnp.where(lane == ROUTE_E2, i2, 0.0)
                      + jnp.where(lane == ROUTE_W1, w1, 0.0) + jnp.where(lane == ROUTE_W2, w2, 0.0))


def _modulate(x3, gain, mod_l, shift_idx, scale_idx, router=None):
    n_batch, seq, d = x3.shape
    ts = _pick(seq, (512, 256, 128))
    grid = (n_batch, seq // ts)
    x_spec = pl.BlockSpec((None, ts, d), lambda b, i: (b, i, 0))
    g_spec = pl.BlockSpec((1, d), lambda b, i: (0, 0))
    mod_spec = pl.BlockSpec((None, N_MOD, d), lambda b, i: (b, 0, 0))
    h_spec = pl.BlockSpec((None, ts, d), lambda b, i: (b, i, 0))
    vmem = 2 * ts * d * (4 + 2) + 4 * ts * d * 4
    if router is None:
        h = pl.pallas_call(
            functools.partial(_modulate_body, shift_idx=shift_idx, scale_idx=scale_idx),
            grid=grid,
            in_specs=[x_spec, g_spec, mod_spec],
            out_specs=h_spec,
            out_shape=jax.ShapeDtypeStruct((n_batch, seq, d), BF16),
            compiler_params=_cparams(("parallel", "parallel"), vmem),
            name="modulate",
        )(x3, gain.reshape(1, d), mod_l)
        return h.reshape(n_batch * seq, d)
    n_experts = router.shape[1]
    assert n_experts <= V7X_LANES
    r_pad = jnp.zeros((d, V7X_LANES), F32).at[:, :n_experts].set(router)
    h, route = pl.pallas_call(
        functools.partial(_modulate_route_body, shift_idx=shift_idx, scale_idx=scale_idx, n_experts=n_experts),
        grid=grid,
        in_specs=[x_spec, g_spec, mod_spec, pl.BlockSpec((d, V7X_LANES), lambda b, i: (0, 0))],
        out_specs=[h_spec, pl.BlockSpec((None, ts, V7X_LANES), lambda b, i: (b, i, 0))],
        out_shape=[jax.ShapeDtypeStruct((n_batch, seq, d), F32),
                   jax.ShapeDtypeStruct((n_batch, seq, V7X_LANES), F32)],
        compiler_params=_cparams(("parallel", "parallel"), vmem + 2 * ts * d * 2 + 6 * ts * d * 2),
        name="modulate_route",
    )(x3, gain.reshape(1, d), mod_l, r_pad)
    return h.reshape(n_batch * seq, d), route.reshape(n_batch * seq, V7X_LANES)


def _mm_cast_body(a_ref, w_ref, o_ref):
    o_ref[...] = _dot(a_ref[...], w_ref[...]).astype(o_ref.dtype)


def _mm_cast(a, w, out_dtype=BF16, name="mm"):
    m, k = a.shape
    n = w.shape[1]
    tn = n if n <= 2048 else _pick(n, (1024, 512, 256, 128))
    need = lambda tm_: 2 * (tm_ * k * 2 + k * tn * 2 + tm_ * tn * 2) + tm_ * tn * 4
    tm = next(t_ for t_ in (1024, 512, 256, 128) if m % t_ == 0 and need(t_) * 5 // 4 <= VMEM_CAP)
    vmem = need(tm)
    return pl.pallas_call(
        _mm_cast_body,
        grid=(m // tm, n // tn),
        in_specs=[pl.BlockSpec((tm, k), lambda i, j: (i, 0)),
                  pl.BlockSpec((k, tn), lambda i, j: (0, j))],
        out_specs=pl.BlockSpec((tm, tn), lambda i, j: (i, j)),
        out_shape=jax.ShapeDtypeStruct((m, n), out_dtype),
        compiler_params=_cparams(("parallel", "arbitrary"), vmem),
        name=name,
    )(a, w)


def _mm_swiglu_body(a_ref, wg_ref, wu_ref, o_ref):
    a = a_ref[...]
    g = _dot(a, wg_ref[...])
    u = _dot(a, wu_ref[...])
    o_ref[...] = (_silu(g) * u).astype(o_ref.dtype)


def _mm_swiglu(a, wg, wu, name="swiglu"):
    m, k = a.shape
    n = wg.shape[1]
    tn = _pick(n, (512, 256, 128))
    need = lambda tm_: 2 * (tm_ * k * 2 + 2 * k * tn * 2 + tm_ * tn * 2) + 3 * tm_ * tn * 4
    tm = next(t_ for t_ in (2048, 1024, 512, 256, 128) if m % t_ == 0 and need(t_) * 10 // 9 <= VMEM_CAP)
    vmem = need(tm)
    return pl.pallas_call(
        _mm_swiglu_body,
        grid=(m // tm, n // tn),
        in_specs=[pl.BlockSpec((tm, k), lambda i, j: (i, 0)),
                  pl.BlockSpec((k, tn), lambda i, j: (0, j)),
                  pl.BlockSpec((k, tn), lambda i, j: (0, j))],
        out_specs=pl.BlockSpec((tm, tn), lambda i, j: (i, j)),
        out_shape=jax.ShapeDtypeStruct((m, n), BF16),
        compiler_params=_cparams(("parallel", "arbitrary"), vmem),
        name=name,
    )(a, wg, wu)


def _mm_residual_body(a_ref, w_ref, x_ref, gate_ref, o_ref, *, nk):
    scale = gate_ref[...]
    part = _dot(a_ref[...], w_ref[...])
    if nk == 1:
        o_ref[...] = x_ref[...] + scale * part
    else:
        kk = pl.program_id(2)

        @pl.when(kk == 0)
        def _():
            o_ref[...] = part

        @pl.when(jnp.logical_and(kk > 0, kk < nk - 1))
        def _():
            o_ref[...] += part

        @pl.when(kk == nk - 1)
        def _():
            o_ref[...] = x_ref[...] + scale * (o_ref[...] + part)


def _mm_residual(a, w, x, gate, seq, name="mm_res"):
    m, k = a.shape
    n = w.shape[1]
    tm = _pick(seq, (1024, 512, 256, 128))
    tn = _pick(n, (1024, 512, 256, 128))
    if k <= 4096:
        tk = k
    else:
        tk = next(t for t in range(4096 // V7X_LANES * V7X_LANES, 0, -V7X_LANES) if k % t == 0)
        if tk < 512:
            tk = next(t for t in range(k // 2 // V7X_LANES * V7X_LANES, 0, -V7X_LANES) if k % t == 0)
    nk = k // tk
    if tk > 4096:
        tn = _pick(n, (512, 256, 128))
    per_batch = seq // tm
    in_specs = [pl.BlockSpec((tm, tk), lambda i, j, kk: (i, kk)),
                pl.BlockSpec((tk, tn), lambda i, j, kk: (kk, j)),
                pl.BlockSpec((tm, tn), lambda i, j, kk: (i, j)),
                pl.BlockSpec((None, 1, tn), lambda i, j, kk: (i // per_batch, 0, j))]
    vmem = 2 * (tm * tk * 2 + tk * tn * 2 + 2 * tm * tn * 4) + 2 * tm * tn * 4
    return pl.pallas_call(
        functools.partial(_mm_residual_body, nk=nk),
        grid=(m // tm, n // tn, nk),
        in_specs=in_specs,
        out_specs=pl.BlockSpec((tm, tn), lambda i, j, kk: (i, j)),
        out_shape=jax.ShapeDtypeStruct((m, n), F32),
        compiler_params=_cparams(("parallel", "parallel", "arbitrary"), vmem),
        name=name,
    )(a, w, x, gate)


def _merge_body(a_ref, b_ref, c_ref, wa_ref, wb_ref, wc_ref, ga_ref, gb_ref, gc_ref, o_ref):
    ya = _dot(a_ref[...], wa_ref[...])
    yb = _dot(b_ref[...], wb_ref[...])
    yc = _dot(c_ref[...], wc_ref[...])
    out = (_sigmoid(ga_ref[...].astype(F32)) * ya + _sigmoid(gb_ref[...].astype(F32)) * yb
           + _sigmoid(gc_ref[...].astype(F32)) * yc)
    o_ref[...] = out.astype(o_ref.dtype)


def _merge(o_a, o_b, o_c, w_a, w_b, w_c, gates):
    m = o_a.shape[0]
    d = w_a.shape[1]
    tm = _pick(m, (1024, 512, 256, 128))
    tn = _pick(d, (512, 256, 128))
    nj = d // tn
    ka, kb, kc = o_a.shape[1], o_b.shape[1], o_c.shape[1]
    vmem = 2 * 2 * (tm * (ka + kb + kc) + (ka + kb + kc) * tn + 4 * tm * tn) + 6 * tm * tn * 4
    return pl.pallas_call(
        _merge_body,
        grid=(m // tm, nj),
        in_specs=[pl.BlockSpec((tm, ka), lambda i, j: (i, 0)),
                  pl.BlockSpec((tm, kb), lambda i, j: (i, 0)),
                  pl.BlockSpec((tm, kc), lambda i, j: (i, 0)),
                  pl.BlockSpec((ka, tn), lambda i, j: (0, j)),
                  pl.BlockSpec((kb, tn), lambda i, j: (0, j)),
                  pl.BlockSpec((kc, tn), lambda i, j: (0, j)),
                  pl.BlockSpec((tm, tn), lambda i, j: (i, j)),
                  pl.BlockSpec((tm, tn), lambda i, j: (i, nj + j)),
                  pl.BlockSpec((tm, tn), lambda i, j: (i, 2 * nj + j))],
        out_specs=pl.BlockSpec((tm, tn), lambda i, j: (i, j)),
        out_shape=jax.ShapeDtypeStruct((m, d), BF16),
        compiler_params=_cparams(("parallel", "arbitrary"), vmem),
        name="merge",
    )(o_a, o_b, o_c, w_a, w_b, w_c, gates, gates, gates)


def _route_metadata(route, n_experts):
    t = route.shape[0]
    a_tot = TOP_K * t
    a_pad = a_tot + n_experts * MOE_TILE
    e = jnp.concatenate([route[:, ROUTE_E1], route[:, ROUTE_E2]]).astype(jnp.int32)
    order = jnp.argsort(e, stable=True).astype(jnp.int32)
    counts = jnp.sum((e[:, None] == jnp.arange(n_experts, dtype=jnp.int32)[None, :]).astype(jnp.int32), axis=0)
    padded = (counts + MOE_TILE - 1) // MOE_TILE * MOE_TILE
    ends_u = jnp.cumsum(counts)
    ends_p = jnp.cumsum(padded)
    start_u = ends_u - counts
    start_p = ends_p - padded
    p = jnp.arange(a_pad, dtype=jnp.int32)
    ep = jnp.minimum(jnp.searchsorted(ends_p, p, side="right"), n_experts - 1).astype(jnp.int32)
    rank = p - start_p[ep]
    valid = jnp.logical_and(rank < counts[ep], p < ends_p[-1])
    src = jnp.clip(start_u[ep] + rank, 0, a_tot - 1)
    row_token = jnp.where(valid, order[src] % t, 0).astype(jnp.int32)
    e_sorted = e[order]
    pos_sorted = jnp.arange(a_tot, dtype=jnp.int32) - start_u[e_sorted] + start_p[e_sorted]
    pos = jnp.zeros((a_tot,), jnp.int32).at[order].set(pos_sorted)
    n_tiles = a_pad // MOE_TILE
    n_valid = (ends_p[-1] // MOE_TILE).astype(jnp.int32)
    tile_start = jnp.arange(n_tiles, dtype=jnp.int32) * MOE_TILE
    tile_e = jnp.minimum(jnp.searchsorted(ends_p, tile_start, side="right"), n_experts - 1).astype(jnp.int32)
    tile_e = jnp.where(tile_start < ends_p[-1], tile_e, tile_e[jnp.maximum(n_valid - 1, 0)])
    return row_token, tile_e, n_valid.reshape(1), pos[:t], pos[t:]


def _row_gather_start(idx_ref, base, src_ref, dst_ref, sem, rows):
    def issue(r, carry):
        pltpu.make_async_copy(src_ref.at[idx_ref[base + r]], dst_ref.at[r], sem).start()
        return carry

    lax.fori_loop(0, rows, issue, 0)


def _row_gather_wait(src_ref, dst_ref, sem, rows):
    pltpu.make_async_copy(src_ref.at[pl.ds(0, rows)], dst_ref, sem).wait()


def _gather_rows_body(idx_ref, src_ref, o_ref, buf, sem, *, rows):
    base = pl.program_id(0) * rows
    _row_gather_start(idx_ref, base, src_ref, buf, sem, rows)
    _row_gather_wait(src_ref, buf, sem, rows)
    o_ref[...] = buf[...].astype(o_ref.dtype)


def _gather_rows(src, idx):
    m = idx.shape[0]
    w = src.shape[1]
    rows = GATHER_ROWS
    assert m % rows == 0 and src.shape[0] >= rows
    return pl.pallas_call(
        functools.partial(_gather_rows_body, rows=rows),
        grid_spec=pltpu.PrefetchScalarGridSpec(
            num_scalar_prefetch=1,
            grid=(m // rows,),
            in_specs=[pl.BlockSpec(memory_space=pl.ANY)],
            out_specs=pl.BlockSpec((rows, w), lambda i, idx_ref: (i, 0)),
            scratch_shapes=[pltpu.VMEM((rows, w), src.dtype), pltpu.SemaphoreType.DMA(())]),
        out_shape=jax.ShapeDtypeStruct((m, w), BF16),
        compiler_params=_cparams(("arbitrary",), rows * w * (4 + 2 * 2 + 4)),
        name="moe_gather",
    )(idx, src)


def _gmm_new_weights(te_ref):
    i = pl.program_id(1)
    return jnp.logical_or(i == 0, te_ref[i] != te_ref[jnp.maximum(i - 1, 0)])


def _gmm_swiglu_body(te_ref, nv_ref, x_ref, wg_ref, wu_ref, o_ref, wg_bf, wu_bf):
    valid = pl.program_id(1) < nv_ref[0]

    @pl.when(_gmm_new_weights(te_ref))
    def _():
        wg_bf[...] = wg_ref[...].astype(BF16)
        wu_bf[...] = wu_ref[...].astype(BF16)

    @pl.when(valid)
    def _():
        a = x_ref[...]
        g = _dot(a, wg_bf[...])
        u = _dot(a, wu_bf[...])
        o_ref[...] = (_silu(g) * u).astype(o_ref.dtype)

    @pl.when(jnp.logical_not(valid))
    def _():
        o_ref[...] = jnp.zeros_like(o_ref)


def _gmm_down_body(te_ref, nv_ref, a_ref, w_ref, o_ref, w_bf):
    valid = pl.program_id(1) < nv_ref[0]

    @pl.when(_gmm_new_weights(te_ref))
    def _():
        w_bf[...] = w_ref[...].astype(BF16)

    @pl.when(valid)
    def _():
        o_ref[...] = _dot(a_ref[...], w_bf[...])

    @pl.when(jnp.logical_not(valid))
    def _():
        o_ref[...] = jnp.zeros_like(o_ref)


def _gmm_maps():
    def rows(j, i, te, nv):
        return (jnp.minimum(i, nv[0] - 1), 0)

    def weights(j, i, te, nv):
        return (te[i], 0, j)

    def out(j, i, te, nv):
        return (i, j)

    return rows, weights, out


def _gmm_swiglu(x_sorted, wg, wu, tile_e, n_valid):
    m, k = x_sorted.shape
    n = wg.shape[2]
    tm = MOE_TILE
    tn = _pick(n, (512, 256, 128))
    rows, weights, out = _gmm_maps()
    vmem = 2 * (tm * k * 2 + 2 * k * tn * 4 + tm * tn * 2) + 2 * k * tn * 2 + 3 * tm * tn * 4
    return pl.pallas_call(
        _gmm_swiglu_body,
        grid_spec=pltpu.PrefetchScalarGridSpec(
            num_scalar_prefetch=2,
            grid=(n // tn, m // tm),
            in_specs=[pl.BlockSpec((tm, k), rows),
                      pl.BlockSpec((None, k, tn), weights),
                      pl.BlockSpec((None, k, tn), weights)],
            out_specs=pl.BlockSpec((tm, tn), out),
            scratch_shapes=[pltpu.VMEM((k, tn), BF16), pltpu.VMEM((k, tn), BF16)]),
        out_shape=jax.ShapeDtypeStruct((m, n), BF16),
        compiler_params=_cparams(("arbitrary", "arbitrary"), vmem),
        name="moe_up",
    )(tile_e, n_valid, x_sorted, wg, wu)


def _gmm_down(a_sorted, wd, tile_e, n_valid):
    m, k = a_sorted.shape
    n = wd.shape[2]
    tm = MOE_TILE
    tn = _pick(n, (1024, 512, 256, 128))
    rows, weights, out = _gmm_maps()
    vmem = 2 * (tm * k * 2 + k * tn * 4 + tm * tn * 4) + k * tn * 2 + tm * tn * 4
    return pl.pallas_call(
        _gmm_down_body,
        grid_spec=pltpu.PrefetchScalarGridSpec(
            num_scalar_prefetch=2,
            grid=(n // tn, m // tm),
            in_specs=[pl.BlockSpec((tm, k), rows),
                      pl.BlockSpec((None, k, tn), weights)],
            out_specs=pl.BlockSpec((tm, tn), out),
            scratch_shapes=[pltpu.VMEM((k, tn), BF16)]),
        out_shape=jax.ShapeDtypeStruct((m, n), F32),
        compiler_params=_cparams(("arbitrary", "arbitrary"), vmem),
        name="moe_down",
    )(tile_e, n_valid, a_sorted, wd)


def _moe_combine_body(p1_ref, p2_ref, y_ref, x_ref, gate_ref, route_ref, o_ref, buf1, buf2, sem, *, rows):
    base = pl.program_id(0) * rows
    _row_gather_start(p1_ref, base, y_ref, buf1, sem.at[0], rows)
    _row_gather_start(p2_ref, base, y_ref, buf2, sem.at[1], rows)
    _row_gather_wait(y_ref, buf1, sem.at[0], rows)
    _row_gather_wait(y_ref, buf2, sem.at[1], rows)
    route = route_ref[...]
    w1 = route[:, ROUTE_W1:ROUTE_W1 + 1]
    w2 = route[:, ROUTE_W2:ROUTE_W2 + 1]
    o_ref[...] = x_ref[...] + gate_ref[...] * (w1 * buf1[...] + w2 * buf2[...])


def _moe_combine(y_sorted, pos1, pos2, x, gate, route, seq):
    t, d = x.shape
    rows = _pick(seq, (128,))
    per_batch = seq // rows
    return pl.pallas_call(
        functools.partial(_moe_combine_body, rows=rows),
        grid_spec=pltpu.PrefetchScalarGridSpec(
            num_scalar_prefetch=2,
            grid=(t // rows,),
            in_specs=[pl.BlockSpec(memory_space=pl.ANY),
                      pl.BlockSpec((rows, d), lambda i, p1, p2: (i, 0)),
                      pl.BlockSpec((None, 1, d), lambda i, p1, p2: (i // per_batch, 0, 0)),
                      pl.BlockSpec((rows, V7X_LANES), lambda i, p1, p2: (i, 0))],
            out_specs=pl.BlockSpec((rows, d), lambda i, p1, p2: (i, 0)),
            scratch_shapes=[pltpu.VMEM((rows, d), F32), pltpu.VMEM((rows, d), F32),
                            pltpu.SemaphoreType.DMA((2,))]),
        out_shape=jax.ShapeDtypeStruct((t, d), F32),
        compiler_params=_cparams(("arbitrary",), 8 * rows * d * 4),
        name="moe_combine",
    )(pos1, pos2, y_sorted, x, gate, route)


_HG_LEVELS = (HG_CHUNK // HG_SUB).bit_length() - 1


def _hgrn_level_masks():
    ti = np.arange(HG_CHUNK)[:, None]
    si = np.arange(HG_CHUNK)[None, :]
    out = []
    for lvl in range(_HG_LEVELS):
        half = HG_SUB << lvl
        blk = 2 * half
        out.append((ti // blk == si // blk) & (ti % blk >= half) & (si % blk < half))
    return jnp.asarray(np.stack(out), F32)


def _hgrn_chunk(q_in, f_in, v, g_in, lb, gain, state_t, tri, ones, lvl_mask_ref):
    c = HG_CHUNK
    fg = lb + (1.0 - lb) * _sigmoid(f_in)
    log_f = jnp.log(jnp.maximum(fg, MIN_FORGET))
    k = 1.0 - fg
    q = _silu(q_in)
    lf_hi, lf_mid, lf_lo = _split3(log_f)
    b = _dot(tri, lf_hi) + _dot(tri, lf_mid) + _dot(tri, lf_lo)
    b_last = b[c - 1:c, :]

    o = _dot_nt((q * jnp.exp(b)).astype(BF16), state_t.astype(BF16))

    row = lax.broadcasted_iota(jnp.int32, (c, 1), 0)
    scores = jnp.zeros((c, c), F32)
    for lvl in range(_HG_LEVELS):
        half = HG_SUB << lvl
        blk = 2 * half
        bref = jnp.concatenate(
            [jnp.broadcast_to(b[p * blk + half - 1:p * blk + half, :], (blk, HG_DK)) for p in range(c // blk)],
            axis=0)
        is_q = (row & half) != 0
        e = jnp.exp(-jnp.abs(b - bref))
        xk = jnp.where(is_q, q, k) * e
        qd = jnp.where(is_q, xk, 0.0).astype(BF16)
        kd = jnp.where(is_q, 0.0, xk).astype(BF16)
        scores = scores + _dot_nt(qd, kd) * lvl_mask_ref[lvl]
    o = o + _dot(scores.astype(BF16), v.astype(BF16))

    nb = c // HG_SUB
    b3 = b.reshape(nb, HG_SUB, HG_DK)
    q3 = q.reshape(nb, HG_SUB, HG_DK)
    k3 = k.reshape(nb, HG_SUB, HG_DK)
    v3 = v.reshape(nb, HG_SUB, HG_DK)
    t_in = lax.broadcasted_iota(jnp.int32, (nb, HG_SUB, HG_DK), 1)
    for s in range(HG_SUB):
        diff = b3 - b3[:, s:s + 1, :]
        dec = jnp.exp(diff if s == 0 else jnp.where(t_in >= s, diff, NEG_BIG))
        m = (q3 * (k3[:, s:s + 1, :] * dec)).reshape(c, HG_DK)
        r = _dot(m.astype(BF16), ones)
        o = o + r * jnp.broadcast_to(v3[:, s:s + 1, :], (nb, HG_SUB, HG_DK)).reshape(c, HG_DK)

    kdec = (k * jnp.exp(b_last - b)).astype(BF16)
    new_state_t = state_t * jnp.exp(b_last) + _dot_tn(v.astype(BF16), kdec)

    ms = jnp.mean(o * o, axis=-1, keepdims=True)
    out = o * lax.rsqrt(ms + NORM_EPS) * gain * _silu(g_in)
    return out, new_state_t


def _hgrn_body(q_ref, f_ref, i_ref, g_ref, lbraw_ref, gain_ref, lvl_mask_ref, o_ref, state_ref, *, layer, n_chunks):
    @pl.when(pl.program_id(2) == 0)
    def _():
        state_ref[...] = jnp.zeros_like(state_ref)

    lbr = lbraw_ref[...]
    ex = jnp.exp(lbr - jnp.max(lbr, axis=0, keepdims=True))
    soft = ex / jnp.sum(ex, axis=0, keepdims=True)
    lb = jnp.zeros((1, HG_DK), F32)
    for j in range(1, layer + 1):
        lb = lb + soft[j:j + 1, :]
    gain = gain_ref[...]
    c = HG_CHUNK
    tri = (lax.broadcasted_iota(jnp.int32, (c, c), 0) >= lax.broadcasted_iota(jnp.int32, (c, c), 1)).astype(BF16)
    ones = jnp.ones((HG_DK, HG_DK), BF16)

    def chunk(ci, carry):
        sl = pl.ds(pl.multiple_of(ci * c, c), c)
        out, new_state = _hgrn_chunk(q_ref[sl, :].astype(F32), f_ref[sl, :].astype(F32),
                                     i_ref[sl, :].astype(F32), g_ref[sl, :].astype(F32),
                                     lb, gain, state_ref[...], tri, ones, lvl_mask_ref)
        o_ref[sl, :] = out.astype(o_ref.dtype)
        state_ref[...] = new_state
        return carry

    lax.fori_loop(0, n_chunks, chunk, 0)


def _hgrn(proj, lb_raw, out_gain, layer, n_batch, seq, width):
    heads = width // HG_DK
    lc = _pick(seq, (512, 256, 128))
    per_batch = seq // lc
    depth = lb_raw.shape[0]

    def col(off):
        return pl.BlockSpec((lc, HG_DK), lambda b, h, i: (b * per_batch + i, off * heads + h))

    return pl.pallas_call(
        functools.partial(_hgrn_body, layer=layer, n_chunks=lc // HG_CHUNK),
        grid=(n_batch, heads, per_batch),
        in_specs=[col(0), col(1), col(2), col(3),
                  pl.BlockSpec((depth, HG_DK), lambda b, h, i: (0, h)),
                  pl.BlockSpec((1, HG_DK), lambda b, h, i: (0, 0)),
                  pl.BlockSpec((_HG_LEVELS, HG_CHUNK, HG_CHUNK), lambda b, h, i: (0, 0, 0))],
        out_specs=pl.BlockSpec((lc, HG_DK), lambda b, h, i: (b * per_batch + i, h)),
        out_shape=jax.ShapeDtypeStruct((n_batch * seq, width), BF16),
        scratch_shapes=[pltpu.VMEM((HG_DK, HG_DK), F32)],
        compiler_params=_cparams(("parallel", "parallel", "arbitrary"), 16 * 1024 * 1024),
        name="hgrn2",
    )(proj, proj, proj, proj, lb_raw, out_gain.reshape(1, HG_DK), _hgrn_level_masks())


def _pool_body(u_ref, halo_ref, w_ref, scale_ref, o_ref, ext_ref, *, ts, gdim):
    i = pl.program_id(1)
    halo = halo_ref[...].astype(F32)
    ext_ref[0:POOL_HALO, :] = jnp.where(i > 0, halo, 0.0)
    ext_ref[POOL_HALO:, :] = u_ref[...].astype(F32)
    pos = (i * ts + lax.broadcasted_iota(jnp.int32, (ts, 1), 0) + 1).astype(F32)
    for g, win in enumerate(POOL_WINDOWS):
        cs = slice(g * gdim, (g + 1) * gdim)
        acc = ext_ref[POOL_HALO:, cs]
        for j in range(1, win):
            acc = acc + ext_ref[POOL_HALO - j:POOL_HALO - j + ts, cs]
        pooled = acc / jnp.minimum(pos, float(win)) - ext_ref[POOL_HALO:, cs]
        y = _dot(pooled.astype(BF16), w_ref[g])
        o_ref[:, cs] = (y * scale_ref[:, cs]).astype(o_ref.dtype)


def _pool(proj, col_off, pool_w, pool_scale, n_batch, seq, width):
    groups = len(POOL_WINDOWS)
    gdim = width // groups
    assert gdim % V7X_LANES == 0 and col_off % width == 0 and max(POOL_WINDOWS) <= POOL_HALO
    ts = _pick(seq, (512, 256, 128))
    per_batch = seq // ts
    cb = col_off // width
    hb = ts // POOL_HALO
    return pl.pallas_call(
        functools.partial(_pool_body, ts=ts, gdim=gdim),
        grid=(n_batch, per_batch),
        in_specs=[pl.BlockSpec((ts, width), lambda b, i: (b * per_batch + i, cb)),
                  pl.BlockSpec((POOL_HALO, width),
                               lambda b, i: (jnp.maximum((b * per_batch + i) * hb - 1, 0), cb)),
                  pl.BlockSpec((groups, gdim, gdim), lambda b, i: (0, 0, 0)),
                  pl.BlockSpec((1, width), lambda b, i: (0, 0))],
        out_specs=pl.BlockSpec((ts, width), lambda b, i: (b * per_batch + i, 0)),
        out_shape=jax.ShapeDtypeStruct((n_batch * seq, width), BF16),
        scratch_shapes=[pltpu.VMEM((ts + POOL_HALO, width), F32)],
        compiler_params=_cparams(("parallel", "parallel"), 16 * 1024 * 1024),
        name="pool",
    )(proj, proj, pool_w, pool_scale.reshape(1, width))


def _rope_table_body(pos_ref, cos_ref, sin_ref):
    pos = pos_ref[...].astype(F32)
    lane = lax.broadcasted_iota(jnp.int32, (1, V7X_LANES), 1)
    j = lane % MLA_ROPE
    fidx = (j % (MLA_ROPE // 2)).astype(F32)
    inv_freq = jnp.exp(fidx * (-2.0 / MLA_ROPE * math.log(ROPE_THETA)))
    ang = pos * inv_freq
    cos_ref[...] = jnp.cos(ang)
    sin_ref[...] = jnp.where(j < MLA_ROPE // 2, -1.0, 1.0) * jnp.sin(ang)


def _rope_table(positions):
    t = positions.size
    ts = _pick(t, (512, 256, 128))
    return pl.pallas_call(
        _rope_table_body,
        grid=(t // ts,),
        in_specs=[pl.BlockSpec((ts, 1), lambda i: (i, 0))],
        out_specs=[pl.BlockSpec((ts, V7X_LANES), lambda i: (i, 0))] * 2,
        out_shape=[jax.ShapeDtypeStruct((t, V7X_LANES), F32)] * 2,
        compiler_params=_cparams(("parallel",), 4 * 1024 * 1024),
        name="rope_table",
    )(positions.reshape(t, 1))


def _rms(x, gain):
    return x * lax.rsqrt(jnp.mean(x * x, axis=-1, keepdims=True) + NORM_EPS) * gain


def _prep_q_body(cq_ref, qn_ref, w_ref, gn_ref, gr_ref, grs_ref, seg_ref, cos_ref, sin_ref, o_ref,
                 *, heads, sm_scale):
    hn = heads * MLA_NOPE
    hr = heads * MLA_ROPE
    hq = _rms(cq_ref[...].astype(F32), qn_ref[...]).astype(BF16)
    y = _dot(hq, w_ref[...])
    yr = y[:, hn:hn + hr]
    ys = y[:, hn + hr:]
    sq_hi, sq_lo = _split2(yr * yr)
    seg = seg_ref[...]
    ss = _dot(sq_hi, seg) + _dot(sq_lo, seg)
    inv = lax.rsqrt(ss * (1.0 / MLA_ROPE) + NORM_EPS)
    reps = hr // V7X_LANES
    cosf = jnp.concatenate([cos_ref[...]] * reps, axis=1)
    sinf = jnp.concatenate([sin_ref[...]] * reps, axis=1)
    qr = (yr * inv * gr_ref[...]) * cosf + (ys * inv * grs_ref[...]) * sinf
    gn = gn_ref[...] * sm_scale
    for h in range(heads):
        qn = _rms(y[:, h * MLA_NOPE:(h + 1) * MLA_NOPE], gn)
        o_ref[h, :, 0:MLA_NOPE] = qn.astype(o_ref.dtype)
        o_ref[h, :, MLA_NOPE:] = (qr[:, h * MLA_ROPE:(h + 1) * MLA_ROPE] * sm_scale).astype(o_ref.dtype)


def _prep_kv_body(ckv_ref, kpe_ref, kvn_ref, w_ref, gn_ref, gr2_ref, cos_ref, sin_ref, k_ref, v_ref, *, heads):
    hk = _rms(ckv_ref[...].astype(F32), kvn_ref[...]).astype(BF16)
    y = _dot(hk, w_ref[...])
    kp = kpe_ref[...].astype(F32)
    kpe = kp[:, :MLA_ROPE]
    inv = lax.rsqrt(jnp.mean(kpe * kpe, axis=-1, keepdims=True) + NORM_EPS)
    kn = kp * inv * gr2_ref[...]
    kr = (kn[:, :MLA_ROPE] * cos_ref[:, :MLA_ROPE] + kn[:, MLA_ROPE:] * sin_ref[:, :MLA_ROPE]).astype(k_ref.dtype)
    per = MLA_NOPE + MLA_V
    for h in range(heads):
        k_ref[h, :, 0:MLA_NOPE] = _rms(y[:, h * per:h * per + MLA_NOPE], gn_ref[...]).astype(k_ref.dtype)
        k_ref[h, :, MLA_NOPE:] = kr
        v_ref[h, 0, :MLA_V, :] = y[:, h * per + MLA_NOPE:(h + 1) * per].T.astype(v_ref.dtype)
        v_ref[h, 0, MLA_V:, :] = jnp.ones((V7X_BF16_ROWS, y.shape[0]), v_ref.dtype)


def _flash_body(q_ref, k_ref, vt_ref, o_ref, acc_ref, *, tile, group):
    qi = pl.program_id(2)
    acc_ref[...] = jnp.zeros_like(acc_ref)

    def step(j, ms, masked):
        ks = pl.ds(pl.multiple_of(j * tile, tile), tile)
        sts = [_dot_nt(k_ref[g, ks, :], q_ref[g]) for g in range(group)]
        if masked:
            kidx = lax.broadcasted_iota(jnp.int32, (tile, tile), 0)
            qidx = lax.broadcasted_iota(jnp.int32, (tile, tile), 1)
            sts = [jnp.where(kidx <= qidx, st, NEG_BIG) for st in sts]
        m_new = [jnp.maximum(m, jnp.max(st, axis=0, keepdims=True)) for m, st in zip(ms, sts)]
        ps = [jnp.exp2(st - m).astype(BF16) for st, m in zip(sts, m_new)]
        for g in range(group):
            acc_ref[g] = jnp.exp2(ms[g] - m_new[g]) * acc_ref[g] + _dot(vt_ref[g, j], ps[g])
        return tuple(m_new)

    m0 = tuple(jnp.full((1, tile), NEG_BIG, F32) for _ in range(group))
    ms = lax.fori_loop(0, qi, lambda j, c: step(j, c, False), m0)
    step(qi, ms, True)
    for g in range(group):
        o = acc_ref[g, :MLA_V, :] / acc_ref[g, MLA_V:MLA_V + 1, :]
        o_ref[:, g * MLA_V:(g + 1) * MLA_V] = o.T.astype(o_ref.dtype)


def _mla(proj_c, cos_t, sin_t, q_norm, w_q, kv_norm, w_kv, g_qn, g_qr, g_kn, g_kr,
         n_batch, seq, heads, q_rank, kv_rank):
    t = n_batch * seq
    half = MLA_ROPE // 2
    hn, hr = heads * MLA_NOPE, heads * MLA_ROPE
    dq = MLA_NOPE + MLA_ROPE
    sm_scale = float(dq) ** -0.5 * math.log2(math.e)
    assert q_rank % kv_rank == 0 and (q_rank + kv_rank) % V7X_LANES == 0 and hr % V7X_LANES == 0
    tile = ATTN_TILE
    assert seq % tile == 0
    tm = tile
    vrows = MLA_V + V7X_BF16_ROWS

    def swap(g):
        return jnp.concatenate([g[half:], g[:half]])

    gr = jnp.tile(g_qr, heads).reshape(1, hr)
    grs = jnp.tile(swap(g_qr), heads).reshape(1, hr)
    lane = jnp.arange(hr) // MLA_ROPE
    seg = (lane[:, None] == lane[None, :]).astype(BF16)
    row = lambda i: (i, 0)
    const2 = lambda i: (0, 0)
    qcat = pl.pallas_call(
        functools.partial(_prep_q_body, heads=heads, sm_scale=sm_scale),
        grid=(t // tm,),
        in_specs=[pl.BlockSpec((tm, q_rank), row),
                  pl.BlockSpec((1, q_rank), const2),
                  pl.BlockSpec((q_rank, hn + 2 * hr), const2),
                  pl.BlockSpec((1, MLA_NOPE), const2),
                  pl.BlockSpec((1, hr), const2),
                  pl.BlockSpec((1, hr), const2),
                  pl.BlockSpec((hr, hr), const2),
                  pl.BlockSpec((tm, V7X_LANES), row),
                  pl.BlockSpec((tm, V7X_LANES), row)],
        out_specs=pl.BlockSpec((heads, tm, dq), lambda i: (0, i, 0)),
        out_shape=jax.ShapeDtypeStruct((heads, t, dq), BF16),
        compiler_params=_cparams(("parallel",), 40 * 1024 * 1024),
        name="mla_prep_q",
    )(proj_c, q_norm.reshape(1, q_rank), w_q, g_qn.reshape(1, MLA_NOPE), gr, grs, seg, cos_t, sin_t)

    gr2 = jnp.concatenate([g_kr, swap(g_kr)]).reshape(1, 2 * MLA_ROPE)
    kcat, vt = pl.pallas_call(
        functools.partial(_prep_kv_body, heads=heads),
        grid=(t // tm,),
        in_specs=[pl.BlockSpec((tm, kv_rank), lambda i: (i, q_rank // kv_rank)),
                  pl.BlockSpec((tm, V7X_LANES), lambda i: (i, (q_rank + kv_rank) // V7X_LANES)),
                  pl.BlockSpec((1, kv_rank), const2),
                  pl.BlockSpec((kv_rank, heads * (MLA_NOPE + MLA_V)), const2),
                  pl.BlockSpec((1, MLA_NOPE), const2),
                  pl.BlockSpec((1, 2 * MLA_ROPE), const2),
                  pl.BlockSpec((tm, V7X_LANES), row),
                  pl.BlockSpec((tm, V7X_LANES), row)],
        out_specs=[pl.BlockSpec((heads, tm, dq), lambda i: (0, i, 0)),
                   pl.BlockSpec((heads, 1, vrows, tile), lambda i: (0, i, 0, 0))],
        out_shape=[jax.ShapeDtypeStruct((heads, t, dq), BF16),
                   jax.ShapeDtypeStruct((heads, t // tile, vrows, tile), BF16)],
        compiler_params=_cparams(("parallel",), 40 * 1024 * 1024),
        name="mla_prep_kv",
    )(proj_c, proj_c, kv_norm.reshape(1, kv_rank), w_kv, g_kn.reshape(1, MLA_NOPE), gr2, cos_t, sin_t)

    nq = seq // tile
    group = ATTN_GROUP if heads % ATTN_GROUP == 0 else 1
    return pl.pallas_call(
        functools.partial(_flash_body, tile=tile, group=group),
        grid=(heads // group, n_batch, nq),
        in_specs=[pl.BlockSpec((group, tile, dq), lambda h, b, i: (h, b * nq + i, 0)),
                  pl.BlockSpec((group, seq, dq), lambda h, b, i: (h, b, 0)),
                  pl.BlockSpec((group, nq, vrows, tile), lambda h, b, i: (h, b, 0, 0))],
        out_specs=pl.BlockSpec((tile, group * MLA_V), lambda h, b, i: (b * nq + i, h)),
        out_shape=jax.ShapeDtypeStruct((t, heads * MLA_V), BF16),
        scratch_shapes=[pltpu.VMEM((group, vrows, tile), F32)],
        compiler_params=_cparams(("parallel", "parallel", "arbitrary"), 40 * 1024 * 1024),
        name="mla_flash",
    )(qcat, kcat, vt)


def kernel(x, c, positions, ada_w, ada_b, ada_layer, mix_norm, ffn_norm, w_in, hgrn_lower_bounds,
           hgrn_out_norm, pool_w, pool_scale, mla_q_norm, mla_w_uq, mla_kv_norm, mla_w_ukv,
           mla_qk_norm_q_nope, mla_qk_norm_q_rope, mla_qk_norm_k_nope, mla_qk_norm_k_rope,
           w_branch_a, w_branch_b, w_branch_c, w_o, ffn_w_gate, ffn_w_up, ffn_w_down,
           moe_router, moe_w_gate, moe_w_up, moe_w_down):
    n_batch, seq, d = x.shape
    depth = w_in.shape[0]
    t = n_batch * seq
    hg_width = hgrn_lower_bounds.shape[1]
    pool_width = pool_scale.shape[1]
    q_rank = mla_q_norm.shape[1]
    kv_rank = mla_kv_norm.shape[1]
    heads = mla_w_ukv.shape[2] // (MLA_NOPE + MLA_V)
    half = MLA_ROPE // 2
    n_a = 4 * hg_width + pool_width
    n_c = q_rank + kv_rank + MLA_ROPE
    assert w_in.shape[2] == n_a + n_c + 3 * d

    mod = _ada(c, ada_w, ada_b, ada_layer)
    cos_t, sin_t = _rope_table(positions)
    xf = x.reshape(t, d)

    for l in range(depth):
        mod_l = mod[l]
        gate1 = mod_l[:, 2:3, :]
        gate2 = mod_l[:, 5:6, :]

        w_l = w_in[l]
        w_a = w_l[:, :n_a].astype(BF16)
        kpe_w = w_l[:, n_a + q_rank + kv_rank:n_a + n_c]
        w_c = jnp.concatenate([w_l[:, n_a:n_a + n_c], kpe_w[:, half:], kpe_w[:, :half]], axis=1).astype(BF16)
        w_g = w_l[:, n_a + n_c:].astype(BF16)
        wq = mla_w_uq[l].reshape(q_rank, heads, MLA_NOPE + MLA_ROPE)
        wq_r = wq[:, :, MLA_NOPE:]
        w_q = jnp.concatenate([
            wq[:, :, :MLA_NOPE].reshape(q_rank, heads * MLA_NOPE),
            wq_r.reshape(q_rank, heads * MLA_ROPE),
            jnp.concatenate([wq_r[:, :, half:], wq_r[:, :, :half]], axis=2).reshape(q_rank, heads * MLA_ROPE),
        ], axis=1).astype(BF16)

        h = _modulate(xf.reshape(n_batch, seq, d), mix_norm[l], mod_l, 0, 1)
        proj_a = _mm_cast(h, w_a, name="proj_a")
        proj_c = _mm_cast(h, w_c, name="proj_c")
        gates = _mm_cast(h, w_g, name="proj_gates")

        o_a = _hgrn(proj_a, hgrn_lower_bounds, hgrn_out_norm[l], l, n_batch, seq, hg_width)
        o_b = _pool(proj_a, 4 * hg_width, pool_w[l].astype(BF16), pool_scale[l], n_batch, seq, pool_width)
        o_c = _mla(proj_c, cos_t, sin_t, mla_q_norm[l], w_q, mla_kv_norm[l], mla_w_ukv[l].astype(BF16),
                   mla_qk_norm_q_nope[l], mla_qk_norm_q_rope[l], mla_qk_norm_k_nope[l], mla_qk_norm_k_rope[l],
                   n_batch, seq, heads, q_rank, kv_rank)
        merged = _merge(o_a, o_b, o_c, w_branch_a[l].astype(BF16), w_branch_b[l].astype(BF16),
                        w_branch_c[l].astype(BF16), gates)
        xf = _mm_residual(merged, w_o[l].astype(BF16), xf, gate1, seq, name="out_proj")

        j = l // 2
        if l % 2 == 0:
            h = _modulate(xf.reshape(n_batch, seq, d), ffn_norm[l], mod_l, 3, 4)
            act = _mm_swiglu(h, ffn_w_gate[j].astype(BF16), ffn_w_up[j].astype(BF16), name="ffn_up")
            xf = _mm_residual(act, ffn_w_down[j].astype(BF16), xf, gate2, seq, name="ffn_down")
        else:
            n_experts = moe_w_gate.shape[1]
            h, route = _modulate(xf.reshape(n_batch, seq, d), ffn_norm[l], mod_l, 3, 4, router=moe_router[j])
            row_token, tile_e, n_valid, pos1, pos2 = _route_metadata(route, n_experts)
            h_sorted = _gather_rows(h, row_token)
            act = _gmm_swiglu(h_sorted, moe_w_gate[j], moe_w_up[j], tile_e, n_valid)
            y_sorted = _gmm_down(act, moe_w_down[j], tile_e, n_valid)
            xf = _moe_combine(y_sorted, pos1, pos2, xf, gate2, route, seq)
    return xf.reshape(n_batch, seq, d)
```

```python
import functools
import math

import numpy as np
import jax
import jax.numpy as jnp
from jax import lax
from jax.experimental import pallas as pl
from jax.experimental.pallas import tpu as pltpu

F32 = jnp.float32
BF16 = jnp.bfloat16

HG_DK = 128
POOL_WINDOWS = (2, 4, 8, 16)
MLA_NOPE = 128
MLA_ROPE = 64
MLA_V = 128
ROPE_THETA = 10000.0
MIN_FORGET = 1e-30
NORM_EPS = 1e-6
N_MOD = 6
TOP_K = 2
NEG_BIG = -1e30

V7X_LANES = 128
V7X_SUBLANES = 8
V7X_BF16_ROWS = 16
V7X_VMEM_BYTES = 64 * 1024 * 1024
VMEM_CAP = V7X_VMEM_BYTES - 8 * 1024 * 1024

HG_CHUNK = 128
HG_SUB = 8
POOL_HALO = 16
ATTN_TILE = 1024
ATTN_GROUP = 2
ROUTE_E1, ROUTE_E2, ROUTE_W1, ROUTE_W2 = 0, 1, 2, 3
MOE_TILE = 512
GATHER_ROWS = 512


def _pick(n, prefs):
    for p in prefs:
        if n % p == 0:
            return p
    raise ValueError(f"no tile in {prefs} divides {n}")


def _cparams(sem, vmem_bytes):
    limit = int(min(VMEM_CAP, max(32 * 1024 * 1024, vmem_bytes * 5 // 4)))
    return pltpu.CompilerParams(dimension_semantics=sem, vmem_limit_bytes=limit)


def _sigmoid(x):
    return 1.0 / (1.0 + jnp.exp(-x))


def _silu(x):
    return x * _sigmoid(x)


def _dot(a, b):
    return jnp.dot(a, b, preferred_element_type=F32)


def _dot_nt(a, b):
    return lax.dot_general(a, b, (((1,), (1,)), ((), ())), preferred_element_type=F32)


def _dot_tn(a, b):
    return lax.dot_general(a, b, (((0,), (0,)), ((), ())), preferred_element_type=F32)


def _split3(x):
    hi = x.astype(BF16)
    r1 = x - hi.astype(F32)
    mid = r1.astype(BF16)
    lo = (r1 - mid.astype(F32)).astype(BF16)
    return hi, mid, lo


def _split2(x):
    hi = x.astype(BF16)
    lo = (x - hi.astype(F32)).astype(BF16)
    return hi, lo


def _ada_body(ct_ref, w_ref, b_ref, lay_ref, o_ref, *, n_batch, depth):
    kk = pl.program_id(1)
    w = w_ref[...]
    tk = w.shape[0]
    s = _silu(ct_ref[pl.ds(pl.multiple_of(kk * tk, tk), tk), :])
    for b in range(n_batch):
        r = jnp.sum(w * s[:, b:b + 1], axis=0, keepdims=True)

        @pl.when(kk == 0)
        def _():
            for l in range(depth):
                o_ref[l, b:b + 1, :] = r + b_ref[...] + lay_ref[l:l + 1, :]

        @pl.when(kk > 0)
        def _():
            for l in range(depth):
                o_ref[l, b:b + 1, :] += r


def _ada(c, ada_w, ada_b, ada_layer):
    n_batch, d = c.shape
    depth = ada_layer.shape[0]
    n = ada_w.shape[1]
    tn = _pick(n, (2048, 1024, 512, 256, 128))
    tk = _pick(d, (1024, 512, 256, 128))
    ct = c.T
    lay = ada_layer.reshape(depth, n)
    out = pl.pallas_call(
        functools.partial(_ada_body, n_batch=n_batch, depth=depth),
        grid=(n // tn, d // tk),
        in_specs=[
            pl.BlockSpec((d, n_batch), lambda j, k: (0, 0)),
            pl.BlockSpec((tk, tn), lambda j, k: (k, j)),
            pl.BlockSpec((1, tn), lambda j, k: (0, j)),
            pl.BlockSpec((depth, tn), lambda j, k: (0, j)),
        ],
        out_specs=pl.BlockSpec((depth, n_batch, tn), lambda j, k: (0, 0, j)),
        out_shape=jax.ShapeDtypeStruct((depth, n_batch, n), F32),
        compiler_params=_cparams(("parallel", "arbitrary"), 4 * tk * tn * 4),
        name="ada",
    )(ct, ada_w, ada_b.reshape(1, n), lay)
    return out.reshape(depth, n_batch, N_MOD, d)


def _modulated(x_ref, g_ref, mod_ref, shift_idx, scale_idx):
    x = x_ref[...]
    ms = jnp.mean(x * x, axis=-1, keepdims=True)
    y = x * lax.rsqrt(ms + NORM_EPS) * g_ref[...]
    return y * (1.0 + mod_ref[scale_idx:scale_idx + 1, :]) + mod_ref[shift_idx:shift_idx + 1, :]


def _modulate_body(x_ref, g_ref, mod_ref, o_ref, *, shift_idx, scale_idx):
    o_ref[...] = _modulated(x_ref, g_ref, mod_ref, shift_idx, scale_idx).astype(o_ref.dtype)


def _modulate_route_body(x_ref, g_ref, mod_ref, r_ref, o_ref, route_ref, *, shift_idx, scale_idx, n_experts):
    h = _modulated(x_ref, g_ref, mod_ref, shift_idx, scale_idx)
    o_ref[...] = h
    h_hi, h_mid, h_lo = _split3(h)
    r = r_ref[...]
    r_hi, r_mid, r_lo = _split3(r)
    logits = (_dot(h_hi, r_hi) + _dot(h_hi, r_mid) + _dot(h_mid, r_hi)
              + _dot(h_hi, r_lo) + _dot(h_mid, r_mid) + _dot(h_lo, r_hi))
    lane = lax.broadcasted_iota(jnp.int32, logits.shape, 1).astype(F32)
    lg = jnp.where(lane < n_experts, logits, -jnp.inf)
    m1 = jnp.max(lg, axis=-1, keepdims=True)
    i1 = jnp.min(jnp.where(lg == m1, lane, float(V7X_LANES)), axis=-1, keepdims=True)
    lg2 = jnp.where(lane == i1, -jnp.inf, lg)
    m2 = jnp.max(lg2, axis=-1, keepdims=True)
    i2 = jnp.min(jnp.where(lg2 == m2, lane, float(V7X_LANES)), axis=-1, keepdims=True)
    e2 = jnp.exp(m2 - m1)
    w1 = 1.0 / (1.0 + e2)
    w2 = e2 / (1.0 + e2)
    route_ref[...] = (jnp.where(lane == ROUTE_E1, i1, 0.0) + jnp.where(lane == ROUTE_E2, i2, 0.0)
                      + jnp.where(lane == ROUTE_W1, w1, 0.0) + jnp.where(lane == ROUTE_W2, w2, 0.0))


def _modulate(x3, gain, mod_l, shift_idx, scale_idx, router=None):
    n_batch, seq, d = x3.shape
    ts = _pick(seq, (512, 256, 128))
    grid = (n_batch, seq // ts)
    x_spec = pl.BlockSpec((None, ts, d), lambda b, i: (b, i, 0))
    g_spec = pl.BlockSpec((1, d), lambda b, i: (0, 0))
    mod_spec = pl.BlockSpec((None, N_MOD, d), lambda b, i: (b, 0, 0))
    h_spec = pl.BlockSpec((None, ts, d), lambda b, i: (b, i, 0))
    vmem = 2 * ts * d * (4 + 2) + 4 * ts * d * 4
    if router is None:
        h = pl.pallas_call(
            functools.partial(_modulate_body, shift_idx=shift_idx, scale_idx=scale_idx),
            grid=grid,
            in_specs=[x_spec, g_spec, mod_spec],
            out_specs=h_spec,
            out_shape=jax.ShapeDtypeStruct((n_batch, seq, d), BF16),
            compiler_params=_cparams(("parallel", "parallel"), vmem),
            name="modulate",
        )(x3, gain.reshape(1, d), mod_l)
        return h.reshape(n_batch * seq, d)
    n_experts = router.shape[1]
    assert n_experts <= V7X_LANES
    r_pad = jnp.zeros((d, V7X_LANES), F32).at[:, :n_experts].set(router)
    h, route = pl.pallas_call(
        functools.partial(_modulate_route_body, shift_idx=shift_idx, scale_idx=scale_idx, n_experts=n_experts),
        grid=grid,
        in_specs=[x_spec, g_spec, mod_spec, pl.BlockSpec((d, V7X_LANES), lambda b, i: (0, 0))],
        out_specs=[h_spec, pl.BlockSpec((None, ts, V7X_LANES), lambda b, i: (b, i, 0))],
        out_shape=[jax.ShapeDtypeStruct((n_batch, seq, d), F32),
                   jax.ShapeDtypeStruct((n_batch, seq, V7X_LANES), F32)],
        compiler_params=_cparams(("parallel", "parallel"), vmem + 2 * ts * d * 2 + 6 * ts * d * 2),
        name="modulate_route",
    )(x3, gain.reshape(1, d), mod_l, r_pad)
    return h.reshape(n_batch * seq, d), route.reshape(n_batch * seq, V7X_LANES)


def _mm_cast_body(a_ref, w_ref, o_ref):
    o_ref[...] = _dot(a_ref[...], w_ref[...]).astype(o_ref.dtype)


def _mm_cast(a, w, out_dtype=BF16, name="mm"):
    m, k = a.shape
    n = w.shape[1]
    tn = n if n <= 2048 else _pick(n, (1024, 512, 256, 128))
    need = lambda tm_: 2 * (tm_ * k * 2 + k * tn * 2 + tm_ * tn * 2) + tm_ * tn * 4
    tm = next(t_ for t_ in (1024, 512, 256, 128) if m % t_ == 0 and need(t_) * 5 // 4 <= VMEM_CAP)
    vmem = need(tm)
    return pl.pallas_call(
        _mm_cast_body,
        grid=(m // tm, n // tn),
        in_specs=[pl.BlockSpec((tm, k), lambda i, j: (i, 0)),
                  pl.BlockSpec((k, tn), lambda i, j: (0, j))],
        out_specs=pl.BlockSpec((tm, tn), lambda i, j: (i, j)),
        out_shape=jax.ShapeDtypeStruct((m, n), out_dtype),
        compiler_params=_cparams(("parallel", "arbitrary"), vmem),
        name=name,
    )(a, w)


def _mm_swiglu_body(a_ref, wg_ref, wu_ref, o_ref):
    a = a_ref[...]
    g = _dot(a, wg_ref[...])
    u = _dot(a, wu_ref[...])
    o_ref[...] = (_silu(g) * u).astype(o_ref.dtype)


def _mm_swiglu(a, wg, wu, name="swiglu"):
    m, k = a.shape
    n = wg.shape[1]
    tn = _pick(n, (512, 256, 128))
    need = lambda tm_: 2 * (tm_ * k * 2 + 2 * k * tn * 2 + tm_ * tn * 2) + 3 * tm_ * tn * 4
    tm = next(t_ for t_ in (2048, 1024, 512, 256, 128) if m % t_ == 0 and need(t_) * 10 // 9 <= VMEM_CAP)
    vmem = need(tm)
    return pl.pallas_call(
        _mm_swiglu_body,
        grid=(m // tm, n // tn),
        in_specs=[pl.BlockSpec((tm, k), lambda i, j: (i, 0)),
                  pl.BlockSpec((k, tn), lambda i, j: (0, j)),
                  pl.BlockSpec((k, tn), lambda i, j: (0, j))],
        out_specs=pl.BlockSpec((tm, tn), lambda i, j: (i, j)),
        out_shape=jax.ShapeDtypeStruct((m, n), BF16),
        compiler_params=_cparams(("parallel", "arbitrary"), vmem),
        name=name,
    )(a, wg, wu)


def _mm_residual_body(a_ref, w_ref, x_ref, gate_ref, o_ref, *, nk):
    scale = gate_ref[...]
    part = _dot(a_ref[...], w_ref[...])
    if nk == 1:
        o_ref[...] = x_ref[...] + scale * part
    else:
        kk = pl.program_id(2)

        @pl.when(kk == 0)
        def _():
            o_ref[...] = part

        @pl.when(jnp.logical_and(kk > 0, kk < nk - 1))
        def _():
            o_ref[...] += part

        @pl.when(kk == nk - 1)
        def _():
            o_ref[...] = x_ref[...] + scale * (o_ref[...] + part)


def _mm_residual(a, w, x, gate, seq, name="mm_res"):
    m, k = a.shape
    n = w.shape[1]
    tm = _pick(seq, (1024, 512, 256, 128))
    tn = _pick(n, (1024, 512, 256, 128))
    if k <= 4096:
        tk = k
    else:
        tk = next(t for t in range(4096 // V7X_LANES * V7X_LANES, 0, -V7X_LANES) if k % t == 0)
        if tk < 512:
            tk = next(t for t in range(k // 2 // V7X_LANES * V7X_LANES, 0, -V7X_LANES) if k % t == 0)
    nk = k // tk
    if tk > 4096:
        tn = _pick(n, (512, 256, 128))
    per_batch = seq // tm
    in_specs = [pl.BlockSpec((tm, tk), lambda i, j, kk: (i, kk)),
                pl.BlockSpec((tk, tn), lambda i, j, kk: (kk, j)),
                pl.BlockSpec((tm, tn), lambda i, j, kk: (i, j)),
                pl.BlockSpec((None, 1, tn), lambda i, j, kk: (i // per_batch, 0, j))]
    vmem = 2 * (tm * tk * 2 + tk * tn * 2 + 2 * tm * tn * 4) + 2 * tm * tn * 4
    return pl.pallas_call(
        functools.partial(_mm_residual_body, nk=nk),
        grid=(m // tm, n // tn, nk),
        in_specs=in_specs,
        out_specs=pl.BlockSpec((tm, tn), lambda i, j, kk: (i, j)),
        out_shape=jax.ShapeDtypeStruct((m, n), F32),
        compiler_params=_cparams(("parallel", "parallel", "arbitrary"), vmem),
        name=name,
    )(a, w, x, gate)


def _merge_body(a_ref, b_ref, c_ref, wa_ref, wb_ref, wc_ref, ga_ref, gb_ref, gc_ref, o_ref):
    ya = _dot(a_ref[...], wa_ref[...])
    yb = _dot(b_ref[...], wb_ref[...])
    yc = _dot(c_ref[...], wc_ref[...])
    out = (_sigmoid(ga_ref[...].astype(F32)) * ya + _sigmoid(gb_ref[...].astype(F32)) * yb
           + _sigmoid(gc_ref[...].astype(F32)) * yc)
    o_ref[...] = out.astype(o_ref.dtype)


def _merge(o_a, o_b, o_c, w_a, w_b, w_c, gates):
    m = o_a.shape[0]
    d = w_a.shape[1]
    tm = _pick(m, (1024, 512, 256, 128))
    tn = _pick(d, (512, 256, 128))
    nj = d // tn
    ka, kb, kc = o_a.shape[1], o_b.shape[1], o_c.shape[1]
    vmem = 2 * 2 * (tm * (ka + kb + kc) + (ka + kb + kc) * tn + 4 * tm * tn) + 6 * tm * tn * 4
    return pl.pallas_call(
        _merge_body,
        grid=(m // tm, nj),
        in_specs=[pl.BlockSpec((tm, ka), lambda i, j: (i, 0)),
                  pl.BlockSpec((tm, kb), lambda i, j: (i, 0)),
                  pl.BlockSpec((tm, kc), lambda i, j: (i, 0)),
                  pl.BlockSpec((ka, tn), lambda i, j: (0, j)),
                  pl.BlockSpec((kb, tn), lambda i, j: (0, j)),
                  pl.BlockSpec((kc, tn), lambda i, j: (0, j)),
                  pl.BlockSpec((tm, tn), lambda i, j: (i, j)),
                  pl.BlockSpec((tm, tn), lambda i, j: (i, nj + j)),
                  pl.BlockSpec((tm, tn), lambda i, j: (i, 2 * nj + j))],
        out_specs=pl.BlockSpec((tm, tn), lambda i, j: (i, j)),
        out_shape=jax.ShapeDtypeStruct((m, d), BF16),
        compiler_params=_cparams(("parallel", "arbitrary"), vmem),
        name="merge",
    )(o_a, o_b, o_c, w_a, w_b, w_c, gates, gates, gates)


def _route_metadata(route, n_experts):
    t = route.shape[0]
    a_tot = TOP_K * t
    a_pad = a_tot + n_experts * MOE_TILE
    e = jnp.concatenate([route[:, ROUTE_E1], route[:, ROUTE_E2]]).astype(jnp.int32)
    order = jnp.argsort(e, stable=True).astype(jnp.int32)
    counts = jnp.sum((e[:, None] == jnp.arange(n_experts, dtype=jnp.int32)[None, :]).astype(jnp.int32), axis=0)
    padded = (counts + MOE_TILE - 1) // MOE_TILE * MOE_TILE
    ends_u = jnp.cumsum(counts)
    ends_p = jnp.cumsum(padded)
    start_u = ends_u - counts
    start_p = ends_p - padded
    p = jnp.arange(a_pad, dtype=jnp.int32)
    ep = jnp.minimum(jnp.searchsorted(ends_p, p, side="right"), n_experts - 1).astype(jnp.int32)
    rank = p - start_p[ep]
    valid = jnp.logical_and(rank < counts[ep], p < ends_p[-1])
    src = jnp.clip(start_u[ep] + rank, 0, a_tot - 1)
    row_token = jnp.where(valid, order[src] % t, 0).astype(jnp.int32)
    rank_sorted = jnp.argsort(order).astype(jnp.int32)
    pos = rank_sorted + (start_p - start_u)[e]
    n_tiles = a_pad // MOE_TILE
    n_valid = (ends_p[-1] // MOE_TILE).astype(jnp.int32)
    tile_start = jnp.arange(n_tiles, dtype=jnp.int32) * MOE_TILE
    tile_e = jnp.minimum(jnp.searchsorted(ends_p, tile_start, side="right"), n_experts - 1).astype(jnp.int32)
    tile_e = jnp.where(tile_start < ends_p[-1], tile_e, tile_e[jnp.maximum(n_valid - 1, 0)])
    return row_token, tile_e, n_valid.reshape(1), pos[:t], pos[t:]


def _row_gather_start(idx_ref, base, src_ref, dst_ref, sem, rows):
    def issue(r, carry):
        pltpu.make_async_copy(src_ref.at[idx_ref[base + r]], dst_ref.at[r], sem).start()
        return carry

    lax.fori_loop(0, rows, issue, 0)


def _row_gather_wait(src_ref, dst_ref, sem, rows):
    pltpu.make_async_copy(src_ref.at[pl.ds(0, rows)], dst_ref, sem).wait()


def _prefetched_gather(starts, waits):
    i = pl.program_id(0)
    slot = i % 2

    @pl.when(i == 0)
    def _():
        starts(0, 0)

    @pl.when(i + 1 < pl.num_programs(0))
    def _():
        starts(i + 1, 1 - slot)

    waits(slot)
    return slot


def _gather_rows_body(idx_ref, src_ref, o_ref, buf, sem, *, rows):
    def starts(step, slot):
        _row_gather_start(idx_ref, step * rows, src_ref, buf.at[slot], sem.at[slot], rows)

    def waits(slot):
        _row_gather_wait(src_ref, buf.at[slot], sem.at[slot], rows)

    slot = _prefetched_gather(starts, waits)
    o_ref[...] = buf[slot].astype(o_ref.dtype)


def _gather_rows(src, idx):
    m = idx.shape[0]
    w = src.shape[1]
    rows = GATHER_ROWS
    assert m % rows == 0 and src.shape[0] >= rows
    return pl.pallas_call(
        functools.partial(_gather_rows_body, rows=rows),
        grid_spec=pltpu.PrefetchScalarGridSpec(
            num_scalar_prefetch=1,
            grid=(m // rows,),
            in_specs=[pl.BlockSpec(memory_space=pl.ANY)],
            out_specs=pl.BlockSpec((rows, w), lambda i, idx_ref: (i, 0)),
            scratch_shapes=[pltpu.VMEM((2, rows, w), src.dtype), pltpu.SemaphoreType.DMA((2,))]),
        out_shape=jax.ShapeDtypeStruct((m, w), BF16),
        compiler_params=_cparams(("arbitrary",), rows * w * (2 * 4 + 2 * 2 + 4)),
        name="moe_gather",
    )(idx, src)


def _gmm_new_weights(te_ref):
    i = pl.program_id(1)
    return jnp.logical_or(i == 0, te_ref[i] != te_ref[jnp.maximum(i - 1, 0)])


def _gmm_swiglu_body(te_ref, nv_ref, x_ref, wg_ref, wu_ref, o_ref, wg_bf, wu_bf):
    valid = pl.program_id(1) < nv_ref[0]

    @pl.when(_gmm_new_weights(te_ref))
    def _():
        wg_bf[...] = wg_ref[...].astype(BF16)
        wu_bf[...] = wu_ref[...].astype(BF16)

    @pl.when(valid)
    def _():
        a = x_ref[...]
        g = _dot(a, wg_bf[...])
        u = _dot(a, wu_bf[...])
        o_ref[...] = (_silu(g) * u).astype(o_ref.dtype)

    @pl.when(jnp.logical_not(valid))
    def _():
        o_ref[...] = jnp.zeros_like(o_ref)


def _gmm_down_body(te_ref, nv_ref, a_ref, w_ref, o_ref, w_bf):
    valid = pl.program_id(1) < nv_ref[0]

    @pl.when(_gmm_new_weights(te_ref))
    def _():
        w_bf[...] = w_ref[...].astype(BF16)

    @pl.when(valid)
    def _():
        o_ref[...] = _dot(a_ref[...], w_bf[...])

    @pl.when(jnp.logical_not(valid))
    def _():
        o_ref[...] = jnp.zeros_like(o_ref)


def _gmm_maps():
    def rows(j, i, te, nv):
        return (jnp.minimum(i, nv[0] - 1), 0)

    def weights(j, i, te, nv):
        return (te[i], 0, j)

    def out(j, i, te, nv):
        return (i, j)

    return rows, weights, out


def _gmm_swiglu(x_sorted, wg, wu, tile_e, n_valid):
    m, k = x_sorted.shape
    n = wg.shape[2]
    tm = MOE_TILE
    tn = _pick(n, (512, 256, 128))
    rows, weights, out = _gmm_maps()
    vmem = 2 * (tm * k * 2 + 2 * k * tn * 4 + tm * tn * 2) + 2 * k * tn * 2 + 3 * tm * tn * 4
    return pl.pallas_call(
        _gmm_swiglu_body,
        grid_spec=pltpu.PrefetchScalarGridSpec(
            num_scalar_prefetch=2,
            grid=(n // tn, m // tm),
            in_specs=[pl.BlockSpec((tm, k), rows),
                      pl.BlockSpec((None, k, tn), weights),
                      pl.BlockSpec((None, k, tn), weights)],
            out_specs=pl.BlockSpec((tm, tn), out),
            scratch_shapes=[pltpu.VMEM((k, tn), BF16), pltpu.VMEM((k, tn), BF16)]),
        out_shape=jax.ShapeDtypeStruct((m, n), BF16),
        compiler_params=_cparams(("arbitrary", "arbitrary"), vmem),
        name="moe_up",
    )(tile_e, n_valid, x_sorted, wg, wu)


def _gmm_down(a_sorted, wd, tile_e, n_valid):
    m, k = a_sorted.shape
    n = wd.shape[2]
    tm = MOE_TILE
    tn = _pick(n, (1024, 512, 256, 128))
    rows, weights, out = _gmm_maps()
    vmem = 2 * (tm * k * 2 + k * tn * 4 + tm * tn * 4) + k * tn * 2 + tm * tn * 4
    return pl.pallas_call(
        _gmm_down_body,
        grid_spec=pltpu.PrefetchScalarGridSpec(
            num_scalar_prefetch=2,
            grid=(n // tn, m // tm),
            in_specs=[pl.BlockSpec((tm, k), rows),
                      pl.BlockSpec((None, k, tn), weights)],
            out_specs=pl.BlockSpec((tm, tn), out),
            scratch_shapes=[pltpu.VMEM((k, tn), BF16)]),
        out_shape=jax.ShapeDtypeStruct((m, n), F32),
        compiler_params=_cparams(("arbitrary", "arbitrary"), vmem),
        name="moe_down",
    )(tile_e, n_valid, a_sorted, wd)


def _moe_combine_body(p1_ref, p2_ref, y_ref, x_ref, gate_ref, route_ref, o_ref, buf1, buf2, sem, *, rows):
    def starts(step, slot):
        _row_gather_start(p1_ref, step * rows, y_ref, buf1.at[slot], sem.at[0, slot], rows)
        _row_gather_start(p2_ref, step * rows, y_ref, buf2.at[slot], sem.at[1, slot], rows)

    def waits(slot):
        _row_gather_wait(y_ref, buf1.at[slot], sem.at[0, slot], rows)
        _row_gather_wait(y_ref, buf2.at[slot], sem.at[1, slot], rows)

    slot = _prefetched_gather(starts, waits)
    route = route_ref[...]
    w1 = route[:, ROUTE_W1:ROUTE_W1 + 1]
    w2 = route[:, ROUTE_W2:ROUTE_W2 + 1]
    o_ref[...] = x_ref[...] + gate_ref[...] * (w1 * buf1[slot] + w2 * buf2[slot])


def _moe_combine(y_sorted, pos1, pos2, x, gate, route, seq):
    t, d = x.shape
    rows = _pick(seq, (128,))
    per_batch = seq // rows
    return pl.pallas_call(
        functools.partial(_moe_combine_body, rows=rows),
        grid_spec=pltpu.PrefetchScalarGridSpec(
            num_scalar_prefetch=2,
            grid=(t // rows,),
            in_specs=[pl.BlockSpec(memory_space=pl.ANY),
                      pl.BlockSpec((rows, d), lambda i, p1, p2: (i, 0)),
                      pl.BlockSpec((None, 1, d), lambda i, p1, p2: (i // per_batch, 0, 0)),
                      pl.BlockSpec((rows, V7X_LANES), lambda i, p1, p2: (i, 0))],
            out_specs=pl.BlockSpec((rows, d), lambda i, p1, p2: (i, 0)),
            scratch_shapes=[pltpu.VMEM((2, rows, d), F32), pltpu.VMEM((2, rows, d), F32),
                            pltpu.SemaphoreType.DMA((2, 2))]),
        out_shape=jax.ShapeDtypeStruct((t, d), F32),
        compiler_params=_cparams(("arbitrary",), 10 * rows * d * 4),
        name="moe_combine",
    )(pos1, pos2, y_sorted, x, gate, route)


_HG_LEVELS = (HG_CHUNK // HG_SUB).bit_length() - 1


def _hgrn_level_masks():
    ti = np.arange(HG_CHUNK)[:, None]
    si = np.arange(HG_CHUNK)[None, :]
    out = []
    for lvl in range(_HG_LEVELS):
        half = HG_SUB << lvl
        blk = 2 * half
        out.append((ti // blk == si // blk) & (ti % blk >= half) & (si % blk < half))
    return jnp.asarray(np.stack(out), F32)


def _hgrn_chunk(q_in, f_in, v, g_in, lb, gain, state_t, tri, ones, lvl_mask_ref):
    c = HG_CHUNK
    fg = lb + (1.0 - lb) * _sigmoid(f_in)
    log_f = jnp.log2(jnp.maximum(fg, MIN_FORGET))
    k = 1.0 - fg
    q = _silu(q_in)
    lf_hi, lf_mid, lf_lo = _split3(log_f)
    b = _dot(tri, lf_hi) + _dot(tri, lf_mid) + _dot(tri, lf_lo)
    b_last = b[c - 1:c, :]

    o = _dot_nt((q * jnp.exp2(b)).astype(BF16), state_t.astype(BF16))

    row = lax.broadcasted_iota(jnp.int32, (c, 1), 0)
    scores = jnp.zeros((c, c), F32)
    for lvl in range(_HG_LEVELS):
        half = HG_SUB << lvl
        blk = 2 * half
        bref = jnp.concatenate(
            [jnp.broadcast_to(b[p * blk + half - 1:p * blk + half, :], (blk, HG_DK)) for p in range(c // blk)],
            axis=0)
        is_q = (row & half) != 0
        e = jnp.exp2(-jnp.abs(b - bref))
        xk = jnp.where(is_q, q, k) * e
        qd = jnp.where(is_q, xk, 0.0).astype(BF16)
        kd = jnp.where(is_q, 0.0, xk).astype(BF16)
        scores = scores + _dot_nt(qd, kd) * lvl_mask_ref[lvl]
    o = o + _dot(scores.astype(BF16), v.astype(BF16))

    nb = c // HG_SUB
    b3 = b.reshape(nb, HG_SUB, HG_DK)
    q3 = q.reshape(nb, HG_SUB, HG_DK)
    k3 = k.reshape(nb, HG_SUB, HG_DK)
    v3 = v.reshape(nb, HG_SUB, HG_DK)
    t_in = lax.broadcasted_iota(jnp.int32, (nb, HG_SUB, HG_DK), 1)
    for s in range(HG_SUB):
        diff = b3 - b3[:, s:s + 1, :]
        dec = jnp.exp2(diff if s == 0 else jnp.where(t_in >= s, diff, NEG_BIG))
        m = (q3 * (k3[:, s:s + 1, :] * dec)).reshape(c, HG_DK)
        r = _dot(m.astype(BF16), ones)
        o = o + r * jnp.broadcast_to(v3[:, s:s + 1, :], (nb, HG_SUB, HG_DK)).reshape(c, HG_DK)

    kdec = (k * jnp.exp2(b_last - b)).astype(BF16)
    new_state_t = state_t * jnp.exp2(b_last) + _dot_tn(v.astype(BF16), kdec)

    ms = jnp.mean(o * o, axis=-1, keepdims=True)
    out = o * lax.rsqrt(ms + NORM_EPS) * gain * _silu(g_in)
    return out, new_state_t


def _hgrn_body(q_ref, f_ref, i_ref, g_ref, lbraw_ref, gain_ref, lvl_mask_ref, o_ref, state_ref, *, layer, n_chunks):
    @pl.when(pl.program_id(2) == 0)
    def _():
        state_ref[...] = jnp.zeros_like(state_ref)

    lbr = lbraw_ref[...]
    ex = jnp.exp(lbr - jnp.max(lbr, axis=0, keepdims=True))
    soft = ex / jnp.sum(ex, axis=0, keepdims=True)
    lb = jnp.zeros((1, HG_DK), F32)
    for j in range(1, layer + 1):
        lb = lb + soft[j:j + 1, :]
    gain = gain_ref[...]
    c = HG_CHUNK
    tri = (lax.broadcasted_iota(jnp.int32, (c, c), 0) >= lax.broadcasted_iota(jnp.int32, (c, c), 1)).astype(BF16)
    ones = jnp.ones((HG_DK, HG_DK), BF16)

    def chunk(ci, carry):
        sl = pl.ds(pl.multiple_of(ci * c, c), c)
        out, new_state = _hgrn_chunk(q_ref[sl, :].astype(F32), f_ref[sl, :].astype(F32),
                                     i_ref[sl, :].astype(F32), g_ref[sl, :].astype(F32),
                                     lb, gain, state_ref[...], tri, ones, lvl_mask_ref)
        o_ref[sl, :] = out.astype(o_ref.dtype)
        state_ref[...] = new_state
        return carry

    lax.fori_loop(0, n_chunks, chunk, 0)


def _hgrn(proj, lb_raw, out_gain, layer, n_batch, seq, width):
    heads = width // HG_DK
    lc = _pick(seq, (512, 256, 128))
    per_batch = seq // lc
    depth = lb_raw.shape[0]

    def col(off):
        return pl.BlockSpec((lc, HG_DK), lambda b, h, i: (b * per_batch + i, off * heads + h))

    return pl.pallas_call(
        functools.partial(_hgrn_body, layer=layer, n_chunks=lc // HG_CHUNK),
        grid=(n_batch, heads, per_batch),
        in_specs=[col(0), col(1), col(2), col(3),
                  pl.BlockSpec((depth, HG_DK), lambda b, h, i: (0, h)),
                  pl.BlockSpec((1, HG_DK), lambda b, h, i: (0, 0)),
                  pl.BlockSpec((_HG_LEVELS, HG_CHUNK, HG_CHUNK), lambda b, h, i: (0, 0, 0))],
        out_specs=pl.BlockSpec((lc, HG_DK), lambda b, h, i: (b * per_batch + i, h)),
        out_shape=jax.ShapeDtypeStruct((n_batch * seq, width), BF16),
        scratch_shapes=[pltpu.VMEM((HG_DK, HG_DK), F32)],
        compiler_params=_cparams(("parallel", "parallel", "arbitrary"), 16 * 1024 * 1024),
        name="hgrn2",
    )(proj, proj, proj, proj, lb_raw, out_gain.reshape(1, HG_DK), _hgrn_level_masks())


def _pool_body(u_ref, halo_ref, w_ref, scale_ref, o_ref, ext_ref, *, ts, gdim):
    i = pl.program_id(1)
    halo = halo_ref[...].astype(F32)
    ext_ref[0:POOL_HALO, :] = jnp.where(i > 0, halo, 0.0)
    ext_ref[POOL_HALO:, :] = u_ref[...].astype(F32)
    pos = (i * ts + lax.broadcasted_iota(jnp.int32, (ts, 1), 0) + 1).astype(F32)
    for g, win in enumerate(POOL_WINDOWS):
        cs = slice(g * gdim, (g + 1) * gdim)
        acc = ext_ref[POOL_HALO:, cs]
        for j in range(1, win):
            acc = acc + ext_ref[POOL_HALO - j:POOL_HALO - j + ts, cs]
        pooled = acc / jnp.minimum(pos, float(win)) - ext_ref[POOL_HALO:, cs]
        y = _dot(pooled.astype(BF16), w_ref[g])
        o_ref[:, cs] = (y * scale_ref[:, cs]).astype(o_ref.dtype)


def _pool(proj, col_off, pool_w, pool_scale, n_batch, seq, width):
    groups = len(POOL_WINDOWS)
    gdim = width // groups
    assert gdim % V7X_LANES == 0 and col_off % width == 0 and max(POOL_WINDOWS) <= POOL_HALO
    ts = _pick(seq, (512, 256, 128))
    per_batch = seq // ts
    cb = col_off // width
    hb = ts // POOL_HALO
    return pl.pallas_call(
        functools.partial(_pool_body, ts=ts, gdim=gdim),
        grid=(n_batch, per_batch),
        in_specs=[pl.BlockSpec((ts, width), lambda b, i: (b * per_batch + i, cb)),
                  pl.BlockSpec((POOL_HALO, width),
                               lambda b, i: (jnp.maximum((b * per_batch + i) * hb - 1, 0), cb)),
                  pl.BlockSpec((groups, gdim, gdim), lambda b, i: (0, 0, 0)),
                  pl.BlockSpec((1, width), lambda b, i: (0, 0))],
        out_specs=pl.BlockSpec((ts, width), lambda b, i: (b * per_batch + i, 0)),
        out_shape=jax.ShapeDtypeStruct((n_batch * seq, width), BF16),
        scratch_shapes=[pltpu.VMEM((ts + POOL_HALO, width), F32)],
        compiler_params=_cparams(("parallel", "parallel"), 16 * 1024 * 1024),
        name="pool",
    )(proj, proj, pool_w, pool_scale.reshape(1, width))


def _rope_table_body(pos_ref, cos_ref, sin_ref):
    pos = pos_ref[...].astype(F32)
    lane = lax.broadcasted_iota(jnp.int32, (1, V7X_LANES), 1)
    j = lane % MLA_ROPE
    fidx = (j % (MLA_ROPE // 2)).astype(F32)
    inv_freq = jnp.exp(fidx * (-2.0 / MLA_ROPE * math.log(ROPE_THETA)))
    ang = pos * inv_freq
    cos_ref[...] = jnp.cos(ang)
    sin_ref[...] = jnp.where(j < MLA_ROPE // 2, -1.0, 1.0) * jnp.sin(ang)


def _rope_table(positions):
    t = positions.size
    ts = _pick(t, (512, 256, 128))
    return pl.pallas_call(
        _rope_table_body,
        grid=(t // ts,),
        in_specs=[pl.BlockSpec((ts, 1), lambda i: (i, 0))],
        out_specs=[pl.BlockSpec((ts, V7X_LANES), lambda i: (i, 0))] * 2,
        out_shape=[jax.ShapeDtypeStruct((t, V7X_LANES), F32)] * 2,
        compiler_params=_cparams(("parallel",), 4 * 1024 * 1024),
        name="rope_table",
    )(positions.reshape(t, 1))


def _rms(x, gain):
    return x * lax.rsqrt(jnp.mean(x * x, axis=-1, keepdims=True) + NORM_EPS) * gain


def _prep_q_body(cq_ref, qn_ref, w_ref, gn_ref, gr_ref, grs_ref, seg_ref, cos_ref, sin_ref, o_ref,
                 *, heads, sm_scale):
    hn = heads * MLA_NOPE
    hr = heads * MLA_ROPE
    hq = _rms(cq_ref[...].astype(F32), qn_ref[...]).astype(BF16)
    y = _dot(hq, w_ref[...])
    yr = y[:, hn:hn + hr]
    ys = y[:, hn + hr:]
    sq_hi, sq_lo = _split2(yr * yr)
    seg = seg_ref[...]
    ss = _dot(sq_hi, seg) + _dot(sq_lo, seg)
    inv = lax.rsqrt(ss * (1.0 / MLA_ROPE) + NORM_EPS)
    reps = hr // V7X_LANES
    cosf = jnp.concatenate([cos_ref[...]] * reps, axis=1)
    sinf = jnp.concatenate([sin_ref[...]] * reps, axis=1)
    qr = (yr * inv * gr_ref[...]) * cosf + (ys * inv * grs_ref[...]) * sinf
    gn = gn_ref[...] * sm_scale
    for h in range(heads):
        qn = _rms(y[:, h * MLA_NOPE:(h + 1) * MLA_NOPE], gn)
        o_ref[h, :, 0:MLA_NOPE] = qn.astype(o_ref.dtype)
        o_ref[h, :, MLA_NOPE:] = (qr[:, h * MLA_ROPE:(h + 1) * MLA_ROPE] * sm_scale).astype(o_ref.dtype)


def _prep_kv_body(ckv_ref, kpe_ref, kvn_ref, w_ref, gn_ref, gr2_ref, cos_ref, sin_ref, k_ref, v_ref, *, heads):
    hk = _rms(ckv_ref[...].astype(F32), kvn_ref[...]).astype(BF16)
    y = _dot(hk, w_ref[...])
    kp = kpe_ref[...].astype(F32)
    kpe = kp[:, :MLA_ROPE]
    inv = lax.rsqrt(jnp.mean(kpe * kpe, axis=-1, keepdims=True) + NORM_EPS)
    kn = kp * inv * gr2_ref[...]
    kr = (kn[:, :MLA_ROPE] * cos_ref[:, :MLA_ROPE] + kn[:, MLA_ROPE:] * sin_ref[:, :MLA_ROPE]).astype(k_ref.dtype)
    per = MLA_NOPE + MLA_V
    for h in range(heads):
        k_ref[h, :, 0:MLA_NOPE] = _rms(y[:, h * per:h * per + MLA_NOPE], gn_ref[...]).astype(k_ref.dtype)
        k_ref[h, :, MLA_NOPE:] = kr
        v_ref[h, 0, :MLA_V, :] = y[:, h * per + MLA_NOPE:(h + 1) * per].T.astype(v_ref.dtype)
        v_ref[h, 0, MLA_V:, :] = jnp.ones((V7X_BF16_ROWS, y.shape[0]), v_ref.dtype)


def _flash_body(q_ref, k_ref, vt_ref, o_ref, acc_ref, st_a, st_b, *, tile, group):
    qi = pl.program_id(2)
    acc_ref[...] = jnp.zeros_like(acc_ref)

    def scores(j, dst):
        ks = pl.ds(pl.multiple_of(j * tile, tile), tile)
        for g in range(group):
            dst[g] = _dot_nt(k_ref[g, ks, :], q_ref[g])

    def consume(j, src, ms, masked):
        m_new = []
        for g in range(group):
            st = src[g]
            if masked:
                kidx = lax.broadcasted_iota(jnp.int32, (tile, tile), 0)
                qidx = lax.broadcasted_iota(jnp.int32, (tile, tile), 1)
                st = jnp.where(kidx <= qidx, st, NEG_BIG)
            m = jnp.maximum(ms[g], jnp.max(st, axis=0, keepdims=True))
            p = jnp.exp2(st - m).astype(BF16)
            acc_ref[g] = jnp.exp2(ms[g] - m) * acc_ref[g] + _dot(vt_ref[g, j], p)
            m_new.append(m)
        return tuple(m_new)

    def finish():
        for g in range(group):
            o = acc_ref[g, :MLA_V, :] / acc_ref[g, MLA_V:MLA_V + 1, :]
            o_ref[:, g * MLA_V:(g + 1) * MLA_V] = o.T.astype(o_ref.dtype)

    scores(0, st_a)

    def pair(pi, ms):
        j = 2 * pi
        scores(j + 1, st_b)
        ms = consume(j, st_a, ms, False)
        scores(j + 2, st_a)
        return consume(j + 1, st_b, ms, False)

    m0 = tuple(jnp.full((1, tile), NEG_BIG, F32) for _ in range(group))
    ms = lax.fori_loop(0, qi // 2, pair, m0)
    odd = qi % 2 == 1

    @pl.when(odd)
    def _():
        scores(qi, st_b)
        consume(qi, st_b, consume(qi - 1, st_a, ms, False), True)
        finish()

    @pl.when(jnp.logical_not(odd))
    def _():
        consume(qi, st_a, ms, True)
        finish()


def _mla(proj_c, cos_t, sin_t, q_norm, w_q, kv_norm, w_kv, g_qn, g_qr, g_kn, g_kr,
         n_batch, seq, heads, q_rank, kv_rank):
    t = n_batch * seq
    half = MLA_ROPE // 2
    hn, hr = heads * MLA_NOPE, heads * MLA_ROPE
    dq = MLA_NOPE + MLA_ROPE
    sm_scale = float(dq) ** -0.5 * math.log2(math.e)
    assert q_rank % kv_rank == 0 and (q_rank + kv_rank) % V7X_LANES == 0 and hr % V7X_LANES == 0
    tile = ATTN_TILE
    assert seq % tile == 0
    tm = tile
    vrows = MLA_V + V7X_BF16_ROWS

    def swap(g):
        return jnp.concatenate([g[half:], g[:half]])

    gr = jnp.tile(g_qr, heads).reshape(1, hr)
    grs = jnp.tile(swap(g_qr), heads).reshape(1, hr)
    lane = jnp.arange(hr) // MLA_ROPE
    seg = (lane[:, None] == lane[None, :]).astype(BF16)
    row = lambda i: (i, 0)
    const2 = lambda i: (0, 0)
    qcat = pl.pallas_call(
        functools.partial(_prep_q_body, heads=heads, sm_scale=sm_scale),
        grid=(t // tm,),
        in_specs=[pl.BlockSpec((tm, q_rank), row),
                  pl.BlockSpec((1, q_rank), const2),
                  pl.BlockSpec((q_rank, hn + 2 * hr), const2),
                  pl.BlockSpec((1, MLA_NOPE), const2),
                  pl.BlockSpec((1, hr), const2),
                  pl.BlockSpec((1, hr), const2),
                  pl.BlockSpec((hr, hr), const2),
                  pl.BlockSpec((tm, V7X_LANES), row),
                  pl.BlockSpec((tm, V7X_LANES), row)],
        out_specs=pl.BlockSpec((heads, tm, dq), lambda i: (0, i, 0)),
        out_shape=jax.ShapeDtypeStruct((heads, t, dq), BF16),
        compiler_params=_cparams(("parallel",), 40 * 1024 * 1024),
        name="mla_prep_q",
    )(proj_c, q_norm.reshape(1, q_rank), w_q, g_qn.reshape(1, MLA_NOPE), gr, grs, seg, cos_t, sin_t)

    gr2 = jnp.concatenate([g_kr, swap(g_kr)]).reshape(1, 2 * MLA_ROPE)
    kcat, vt = pl.pallas_call(
        functools.partial(_prep_kv_body, heads=heads),
        grid=(t // tm,),
        in_specs=[pl.BlockSpec((tm, kv_rank), lambda i: (i, q_rank // kv_rank)),
                  pl.BlockSpec((tm, V7X_LANES), lambda i: (i, (q_rank + kv_rank) // V7X_LANES)),
                  pl.BlockSpec((1, kv_rank), const2),
                  pl.BlockSpec((kv_rank, heads * (MLA_NOPE + MLA_V)), const2),
                  pl.BlockSpec((1, MLA_NOPE), const2),
                  pl.BlockSpec((1, 2 * MLA_ROPE), const2),
                  pl.BlockSpec((tm, V7X_LANES), row),
                  pl.BlockSpec((tm, V7X_LANES), row)],
        out_specs=[pl.BlockSpec((heads, tm, dq), lambda i: (0, i, 0)),
                   pl.BlockSpec((heads, 1, vrows, tile), lambda i: (0, i, 0, 0))],
        out_shape=[jax.ShapeDtypeStruct((heads, t, dq), BF16),
                   jax.ShapeDtypeStruct((heads, t // tile, vrows, tile), BF16)],
        compiler_params=_cparams(("parallel",), 40 * 1024 * 1024),
        name="mla_prep_kv",
    )(proj_c, proj_c, kv_norm.reshape(1, kv_rank), w_kv, g_kn.reshape(1, MLA_NOPE), gr2, cos_t, sin_t)

    nq = seq // tile
    group = ATTN_GROUP if heads % ATTN_GROUP == 0 else 1
    return pl.pallas_call(
        functools.partial(_flash_body, tile=tile, group=group),
        grid=(heads // group, n_batch, nq),
        in_specs=[pl.BlockSpec((group, tile, dq), lambda h, b, i: (h, b * nq + i, 0)),
                  pl.BlockSpec((group, seq, dq), lambda h, b, i: (h, b, 0)),
                  pl.BlockSpec((group, nq, vrows, tile), lambda h, b, i: (h, b, 0, 0))],
        out_specs=pl.BlockSpec((tile, group * MLA_V), lambda h, b, i: (b * nq + i, h)),
        out_shape=jax.ShapeDtypeStruct((t, heads * MLA_V), BF16),
        scratch_shapes=[pltpu.VMEM((group, vrows, tile), F32),
                        pltpu.VMEM((group, tile, tile), F32), pltpu.VMEM((group, tile, tile), F32)],
        compiler_params=_cparams(("parallel", "parallel", "arbitrary"), 48 * 1024 * 1024),
        name="mla_flash",
    )(qcat, kcat, vt)


def kernel(x, c, positions, ada_w, ada_b, ada_layer, mix_norm, ffn_norm, w_in, hgrn_lower_bounds,
           hgrn_out_norm, pool_w, pool_scale, mla_q_norm, mla_w_uq, mla_kv_norm, mla_w_ukv,
           mla_qk_norm_q_nope, mla_qk_norm_q_rope, mla_qk_norm_k_nope, mla_qk_norm_k_rope,
           w_branch_a, w_branch_b, w_branch_c, w_o, ffn_w_gate, ffn_w_up, ffn_w_down,
           moe_router, moe_w_gate, moe_w_up, moe_w_down):
    n_batch, seq, d = x.shape
    depth = w_in.shape[0]
    t = n_batch * seq
    hg_width = hgrn_lower_bounds.shape[1]
    pool_width = pool_scale.shape[1]
    q_rank = mla_q_norm.shape[1]
    kv_rank = mla_kv_norm.shape[1]
    heads = mla_w_ukv.shape[2] // (MLA_NOPE + MLA_V)
    half = MLA_ROPE // 2
    n_a = 4 * hg_width + pool_width
    n_c = q_rank + kv_rank + MLA_ROPE
    assert w_in.shape[2] == n_a + n_c + 3 * d

    mod = _ada(c, ada_w, ada_b, ada_layer)
    cos_t, sin_t = _rope_table(positions)
    xf = x.reshape(t, d)

    for l in range(depth):
        mod_l = mod[l]
        gate1 = mod_l[:, 2:3, :]
        gate2 = mod_l[:, 5:6, :]

        w_l = w_in[l]
        w_a = w_l[:, :n_a].astype(BF16)
        kpe_w = w_l[:, n_a + q_rank + kv_rank:n_a + n_c]
        w_c = jnp.concatenate([w_l[:, n_a:n_a + n_c], kpe_w[:, half:], kpe_w[:, :half]], axis=1).astype(BF16)
        w_g = w_l[:, n_a + n_c:].astype(BF16)
        wq = mla_w_uq[l].reshape(q_rank, heads, MLA_NOPE + MLA_ROPE)
        wq_r = wq[:, :, MLA_NOPE:]
        w_q = jnp.concatenate([
            wq[:, :, :MLA_NOPE].reshape(q_rank, heads * MLA_NOPE),
            wq_r.reshape(q_rank, heads * MLA_ROPE),
            jnp.concatenate([wq_r[:, :, half:], wq_r[:, :, :half]], axis=2).reshape(q_rank, heads * MLA_ROPE),
        ], axis=1).astype(BF16)

        h = _modulate(xf.reshape(n_batch, seq, d), mix_norm[l], mod_l, 0, 1)
        proj_a = _mm_cast(h, w_a, name="proj_a")
        proj_c = _mm_cast(h, w_c, name="proj_c")
        gates = _mm_cast(h, w_g, name="proj_gates")

        o_a = _hgrn(proj_a, hgrn_lower_bounds, hgrn_out_norm[l], l, n_batch, seq, hg_width)
        o_b = _pool(proj_a, 4 * hg_width, pool_w[l].astype(BF16), pool_scale[l], n_batch, seq, pool_width)
        o_c = _mla(proj_c, cos_t, sin_t, mla_q_norm[l], w_q, mla_kv_norm[l], mla_w_ukv[l].astype(BF16),
                   mla_qk_norm_q_nope[l], mla_qk_norm_q_rope[l], mla_qk_norm_k_nope[l], mla_qk_norm_k_rope[l],
                   n_batch, seq, heads, q_rank, kv_rank)
        merged = _merge(o_a, o_b, o_c, w_branch_a[l].astype(BF16), w_branch_b[l].astype(BF16),
                        w_branch_c[l].astype(BF16), gates)
        xf = _mm_residual(merged, w_o[l].astype(BF16), xf, gate1, seq, name="out_proj")

        j = l // 2
        if l % 2 == 0:
            h = _modulate(xf.reshape(n_batch, seq, d), ffn_norm[l], mod_l, 3, 4)
            act = _mm_swiglu(h, ffn_w_gate[j].astype(BF16), ffn_w_up[j].astype(BF16), name="ffn_up")
            xf = _mm_residual(act, ffn_w_down[j].astype(BF16), xf, gate2, seq, name="ffn_down")
        else:
            n_experts = moe_w_gate.shape[1]
            h, route = _modulate(xf.reshape(n_batch, seq, d), ffn_norm[l], mod_l, 3, 4, router=moe_router[j])
            row_token, tile_e, n_valid, pos1, pos2 = _route_metadata(route, n_experts)
            h_sorted = _gather_rows(h, row_token)
            act = _gmm_swiglu(h_sorted, moe_w_gate[j], moe_w_up[j], tile_e, n_valid)
            y_sorted = _gmm_down(act, moe_w_down[j], tile_e, n_valid)
            xf = _moe_combine(y_sorted, pos1, pos2, xf, gate2, route, seq)
    return xf.reshape(n_batch, seq, d)
```

```python
import functools
import math

import numpy as np
import jax
import jax.numpy as jnp
from jax import lax
from jax.experimental import pallas as pl
from jax.experimental.pallas import tpu as pltpu

F32 = jnp.float32
BF16 = jnp.bfloat16

HG_DK = 128
POOL_WINDOWS = (2, 4, 8, 16)
MLA_NOPE = 128
MLA_ROPE = 64
MLA_V = 128
ROPE_THETA = 10000.0
MIN_FORGET = 1e-30
NORM_EPS = 1e-6
N_MOD = 6
TOP_K = 2
NEG_BIG = -1e30

V7X_LANES = 128
V7X_SUBLANES = 8
V7X_BF16_ROWS = 16
V7X_VMEM_BYTES = 64 * 1024 * 1024
VMEM_CAP = V7X_VMEM_BYTES - 8 * 1024 * 1024

HG_CHUNK = 128
HG_SUB = 8
HG_GROUP = 2
POOL_HALO = 16
ATTN_TILE = 1024
ATTN_GROUP = 2
ROUTE_E1, ROUTE_E2, ROUTE_W1, ROUTE_W2 = 0, 1, 2, 3
MOE_TILE = 512
GATHER_ROWS = 512


def _pick(n, prefs):
    for p in prefs:
        if n % p == 0:
            return p
    raise ValueError(f"no tile in {prefs} divides {n}")


def _cparams(sem, vmem_bytes):
    limit = int(min(VMEM_CAP, max(32 * 1024 * 1024, vmem_bytes * 5 // 4)))
    return pltpu.CompilerParams(dimension_semantics=sem, vmem_limit_bytes=limit)


def _sigmoid(x):
    return 1.0 / (1.0 + jnp.exp(-x))


def _silu(x):
    return x * _sigmoid(x)


def _dot(a, b):
    return jnp.dot(a, b, preferred_element_type=F32)


def _dot_nt(a, b):
    return lax.dot_general(a, b, (((1,), (1,)), ((), ())), preferred_element_type=F32)


def _dot_tn(a, b):
    return lax.dot_general(a, b, (((0,), (0,)), ((), ())), preferred_element_type=F32)


def _split3(x):
    hi = x.astype(BF16)
    r1 = x - hi.astype(F32)
    mid = r1.astype(BF16)
    lo = (r1 - mid.astype(F32)).astype(BF16)
    return hi, mid, lo


def _split2(x):
    hi = x.astype(BF16)
    lo = (x - hi.astype(F32)).astype(BF16)
    return hi, lo


def _ada_body(ct_ref, w_ref, b_ref, lay_ref, o_ref, *, n_batch, depth):
    kk = pl.program_id(1)
    w = w_ref[...]
    tk = w.shape[0]
    s = _silu(ct_ref[pl.ds(pl.multiple_of(kk * tk, tk), tk), :])
    for b in range(n_batch):
        r = jnp.sum(w * s[:, b:b + 1], axis=0, keepdims=True)

        @pl.when(kk == 0)
        def _():
            for l in range(depth):
                o_ref[l, b:b + 1, :] = r + b_ref[...] + lay_ref[l:l + 1, :]

        @pl.when(kk > 0)
        def _():
            for l in range(depth):
                o_ref[l, b:b + 1, :] += r


def _ada(c, ada_w, ada_b, ada_layer):
    n_batch, d = c.shape
    depth = ada_layer.shape[0]
    n = ada_w.shape[1]
    tn = n
    tk = _pick(d, (64, 32, 16, 8))
    ct = c.T
    lay = ada_layer.reshape(depth, n)
    out = pl.pallas_call(
        functools.partial(_ada_body, n_batch=n_batch, depth=depth),
        grid=(n // tn, d // tk),
        in_specs=[
            pl.BlockSpec((d, n_batch), lambda j, k: (0, 0)),
            pl.BlockSpec((tk, tn), lambda j, k: (k, j)),
            pl.BlockSpec((1, tn), lambda j, k: (0, j)),
            pl.BlockSpec((depth, tn), lambda j, k: (0, j)),
        ],
        out_specs=pl.BlockSpec((depth, n_batch, tn), lambda j, k: (0, 0, j)),
        out_shape=jax.ShapeDtypeStruct((depth, n_batch, n), F32),
        compiler_params=_cparams(("parallel", "arbitrary"), 4 * tk * tn * 4),
        name="ada",
    )(ct, ada_w, ada_b.reshape(1, n), lay)
    return out.reshape(depth, n_batch, N_MOD, d)


def _modulated(x_ref, g_ref, mod_ref, shift_idx, scale_idx):
    x = x_ref[...]
    ms = jnp.mean(x * x, axis=-1, keepdims=True)
    y = x * lax.rsqrt(ms + NORM_EPS) * g_ref[...]
    return y * (1.0 + mod_ref[scale_idx:scale_idx + 1, :]) + mod_ref[shift_idx:shift_idx + 1, :]


def _modulate_body(x_ref, g_ref, mod_ref, o_ref, *, shift_idx, scale_idx):
    o_ref[...] = _modulated(x_ref, g_ref, mod_ref, shift_idx, scale_idx).astype(o_ref.dtype)


def _modulate_route_body(x_ref, g_ref, mod_ref, r_ref, o_ref, route_ref, *, shift_idx, scale_idx, n_experts):
    h = _modulated(x_ref, g_ref, mod_ref, shift_idx, scale_idx)
    o_ref[...] = h
    h_hi, h_mid, h_lo = _split3(h)
    r = r_ref[...]
    r_hi, r_mid, r_lo = _split3(r)
    logits = (_dot(h_hi, r_hi) + _dot(h_hi, r_mid) + _dot(h_mid, r_hi)
              + _dot(h_hi, r_lo) + _dot(h_mid, r_mid) + _dot(h_lo, r_hi))
    lane = lax.broadcasted_iota(jnp.int32, logits.shape, 1).astype(F32)
    lg = jnp.where(lane < n_experts, logits, -jnp.inf)
    m1 = jnp.max(lg, axis=-1, keepdims=True)
    i1 = jnp.min(jnp.where(lg == m1, lane, float(V7X_LANES)), axis=-1, keepdims=True)
    lg2 = jnp.where(lane == i1, -jnp.inf, lg)
    m2 = jnp.max(lg2, axis=-1, keepdims=True)
    i2 = jnp.min(jnp.where(lg2 == m2, lane, float(V7X_LANES)), axis=-1, keepdims=True)
    e2 = jnp.exp(m2 - m1)
    w1 = 1.0 / (1.0 + e2)
    w2 = e2 / (1.0 + e2)
    route_ref[...] = (jnp.where(lane == ROUTE_E1, i1, 0.0) + jnp.where(lane == ROUTE_E2, i2, 0.0)
                      + jnp.where(lane == ROUTE_W1, w1, 0.0) + jnp.where(lane == ROUTE_W2, w2, 0.0))


def _modulate(x3, gain, mod_l, shift_idx, scale_idx, router=None):
    n_batch, seq, d = x3.shape
    ts = _pick(seq, (512, 256, 128))
    grid = (n_batch, seq // ts)
    x_spec = pl.BlockSpec((None, ts, d), lambda b, i: (b, i, 0))
    g_spec = pl.BlockSpec((1, d), lambda b, i: (0, 0))
    mod_spec = pl.BlockSpec((None, N_MOD, d), lambda b, i: (b, 0, 0))
    h_spec = pl.BlockSpec((None, ts, d), lambda b, i: (b, i, 0))
    vmem = 2 * ts * d * (4 + 2) + 4 * ts * d * 4
    if router is None:
        h = pl.pallas_call(
            functools.partial(_modulate_body, shift_idx=shift_idx, scale_idx=scale_idx),
            grid=grid,
            in_specs=[x_spec, g_spec, mod_spec],
            out_specs=h_spec,
            out_shape=jax.ShapeDtypeStruct((n_batch, seq, d), BF16),
            compiler_params=_cparams(("parallel", "parallel"), vmem),
            name="modulate",
        )(x3, gain.reshape(1, d), mod_l)
        return h.reshape(n_batch * seq, d)
    n_experts = router.shape[1]
    assert n_experts <= V7X_LANES
    r_pad = jnp.zeros((d, V7X_LANES), F32).at[:, :n_experts].set(router)
    h, route = pl.pallas_call(
        functools.partial(_modulate_route_body, shift_idx=shift_idx, scale_idx=scale_idx, n_experts=n_experts),
        grid=grid,
        in_specs=[x_spec, g_spec, mod_spec, pl.BlockSpec((d, V7X_LANES), lambda b, i: (0, 0))],
        out_specs=[h_spec, pl.BlockSpec((None, ts, V7X_LANES), lambda b, i: (b, i, 0))],
        out_shape=[jax.ShapeDtypeStruct((n_batch, seq, d), F32),
                   jax.ShapeDtypeStruct((n_batch, seq, V7X_LANES), F32)],
        compiler_params=_cparams(("parallel", "parallel"), vmem + 2 * ts * d * 2 + 6 * ts * d * 2),
        name="modulate_route",
    )(x3, gain.reshape(1, d), mod_l, r_pad)
    return h.reshape(n_batch * seq, d), route.reshape(n_batch * seq, V7X_LANES)


def _mm_cast_body(a_ref, w_ref, o_ref):
    o_ref[...] = _dot(a_ref[...], w_ref[...]).astype(o_ref.dtype)


def _mm_cast(a, w, out_dtype=BF16, name="mm"):
    m, k = a.shape
    n = w.shape[1]
    tn = n if n <= 2048 else _pick(n, (1024, 512, 256, 128))
    need = lambda tm_: 2 * (tm_ * k * 2 + k * tn * 2 + tm_ * tn * 2) + tm_ * tn * 4
    tm = next(t_ for t_ in (1024, 512, 256, 128) if m % t_ == 0 and need(t_) * 5 // 4 <= VMEM_CAP)
    vmem = need(tm)
    return pl.pallas_call(
        _mm_cast_body,
        grid=(m // tm, n // tn),
        in_specs=[pl.BlockSpec((tm, k), lambda i, j: (i, 0)),
                  pl.BlockSpec((k, tn), lambda i, j: (0, j))],
        out_specs=pl.BlockSpec((tm, tn), lambda i, j: (i, j)),
        out_shape=jax.ShapeDtypeStruct((m, n), out_dtype),
        compiler_params=_cparams(("parallel", "arbitrary"), vmem),
        name=name,
    )(a, w)


def _mm_swiglu_body(a_ref, wg_ref, wu_ref, o_ref):
    a = a_ref[...]
    g = _dot(a, wg_ref[...])
    u = _dot(a, wu_ref[...])
    o_ref[...] = (_silu(g) * u).astype(o_ref.dtype)


def _mm_swiglu(a, wg, wu, name="swiglu"):
    m, k = a.shape
    n = wg.shape[1]
    tn = _pick(n, (512, 256, 128))
    need = lambda tm_: 2 * (tm_ * k * 2 + 2 * k * tn * 2 + tm_ * tn * 2) + 3 * tm_ * tn * 4
    tm = next(t_ for t_ in (2048, 1024, 512, 256, 128) if m % t_ == 0 and need(t_) * 10 // 9 <= VMEM_CAP)
    vmem = need(tm)
    return pl.pallas_call(
        _mm_swiglu_body,
        grid=(m // tm, n // tn),
        in_specs=[pl.BlockSpec((tm, k), lambda i, j: (i, 0)),
                  pl.BlockSpec((k, tn), lambda i, j: (0, j)),
                  pl.BlockSpec((k, tn), lambda i, j: (0, j))],
        out_specs=pl.BlockSpec((tm, tn), lambda i, j: (i, j)),
        out_shape=jax.ShapeDtypeStruct((m, n), BF16),
        compiler_params=_cparams(("parallel", "arbitrary"), vmem),
        name=name,
    )(a, wg, wu)


def _mm_residual_body(a_ref, w_ref, x_ref, gate_ref, o_ref, *, nk):
    scale = gate_ref[...]
    part = _dot(a_ref[...], w_ref[...])
    if nk == 1:
        o_ref[...] = x_ref[...] + scale * part
    else:
        kk = pl.program_id(2)

        @pl.when(kk == 0)
        def _():
            o_ref[...] = part

        @pl.when(jnp.logical_and(kk > 0, kk < nk - 1))
        def _():
            o_ref[...] += part

        @pl.when(kk == nk - 1)
        def _():
            o_ref[...] = x_ref[...] + scale * (o_ref[...] + part)


def _mm_residual(a, w, x, gate, seq, name="mm_res"):
    m, k = a.shape
    n = w.shape[1]
    tm = _pick(seq, (1024, 512, 256, 128))
    tn = _pick(n, (1024, 512, 256, 128))
    if k <= 4096:
        tk = k
    else:
        tk = next(t for t in range(4096 // V7X_LANES * V7X_LANES, 0, -V7X_LANES) if k % t == 0)
        if tk < 512:
            tk = next(t for t in range(k // 2 // V7X_LANES * V7X_LANES, 0, -V7X_LANES) if k % t == 0)
    nk = k // tk
    if tk > 4096:
        tn = _pick(n, (512, 256, 128))
    per_batch = seq // tm
    in_specs = [pl.BlockSpec((tm, tk), lambda i, j, kk: (i, kk)),
                pl.BlockSpec((tk, tn), lambda i, j, kk: (kk, j)),
                pl.BlockSpec((tm, tn), lambda i, j, kk: (i, j)),
                pl.BlockSpec((None, 1, tn), lambda i, j, kk: (i // per_batch, 0, j))]
    vmem = 2 * (tm * tk * 2 + tk * tn * 2 + 2 * tm * tn * 4) + 2 * tm * tn * 4
    return pl.pallas_call(
        functools.partial(_mm_residual_body, nk=nk),
        grid=(m // tm, n // tn, nk),
        in_specs=in_specs,
        out_specs=pl.BlockSpec((tm, tn), lambda i, j, kk: (i, j)),
        out_shape=jax.ShapeDtypeStruct((m, n), F32),
        compiler_params=_cparams(("parallel", "parallel", "arbitrary"), vmem),
        name=name,
    )(a, w, x, gate)


def _merge_body(a_ref, b_ref, c_ref, wa_ref, wb_ref, wc_ref, ga_ref, gb_ref, gc_ref, o_ref):
    ya = _dot(a_ref[...], wa_ref[...])
    yb = _dot(b_ref[...], wb_ref[...])
    yc = _dot(c_ref[...], wc_ref[...])
    out = (_sigmoid(ga_ref[...].astype(F32)) * ya + _sigmoid(gb_ref[...].astype(F32)) * yb
           + _sigmoid(gc_ref[...].astype(F32)) * yc)
    o_ref[...] = out.astype(o_ref.dtype)


def _merge(o_a, o_b, o_c, w_a, w_b, w_c, gates):
    m = o_a.shape[0]
    d = w_a.shape[1]
    tm = _pick(m, (1024, 512, 256, 128))
    tn = _pick(d, (512, 256, 128))
    nj = d // tn
    ka, kb, kc = o_a.shape[1], o_b.shape[1], o_c.shape[1]
    vmem = 2 * 2 * (tm * (ka + kb + kc) + (ka + kb + kc) * tn + 4 * tm * tn) + 6 * tm * tn * 4
    return pl.pallas_call(
        _merge_body,
        grid=(m // tm, nj),
        in_specs=[pl.BlockSpec((tm, ka), lambda i, j: (i, 0)),
                  pl.BlockSpec((tm, kb), lambda i, j: (i, 0)),
                  pl.BlockSpec((tm, kc), lambda i, j: (i, 0)),
                  pl.BlockSpec((ka, tn), lambda i, j: (0, j)),
                  pl.BlockSpec((kb, tn), lambda i, j: (0, j)),
                  pl.BlockSpec((kc, tn), lambda i, j: (0, j)),
                  pl.BlockSpec((tm, tn), lambda i, j: (i, j)),
                  pl.BlockSpec((tm, tn), lambda i, j: (i, nj + j)),
                  pl.BlockSpec((tm, tn), lambda i, j: (i, 2 * nj + j))],
        out_specs=pl.BlockSpec((tm, tn), lambda i, j: (i, j)),
        out_shape=jax.ShapeDtypeStruct((m, d), BF16),
        compiler_params=_cparams(("parallel", "arbitrary"), vmem),
        name="merge",
    )(o_a, o_b, o_c, w_a, w_b, w_c, gates, gates, gates)


def _route_metadata(route, n_experts):
    t = route.shape[0]
    a_tot = TOP_K * t
    a_pad = a_tot + n_experts * MOE_TILE
    e = jnp.concatenate([route[:, ROUTE_E1], route[:, ROUTE_E2]]).astype(jnp.int32)
    order = jnp.argsort(e, stable=True).astype(jnp.int32)
    counts = jnp.sum((e[:, None] == jnp.arange(n_experts, dtype=jnp.int32)[None, :]).astype(jnp.int32), axis=0)
    padded = (counts + MOE_TILE - 1) // MOE_TILE * MOE_TILE
    ends_u = jnp.cumsum(counts)
    ends_p = jnp.cumsum(padded)
    start_u = ends_u - counts
    start_p = ends_p - padded
    p = jnp.arange(a_pad, dtype=jnp.int32)
    ep = jnp.minimum(jnp.searchsorted(ends_p, p, side="right"), n_experts - 1).astype(jnp.int32)
    rank = p - start_p[ep]
    valid = jnp.logical_and(rank < counts[ep], p < ends_p[-1])
    src = jnp.clip(start_u[ep] + rank, 0, a_tot - 1)
    row_token = jnp.where(valid, order[src] % t, 0).astype(jnp.int32)
    rank_sorted = jnp.argsort(order).astype(jnp.int32)
    pos = rank_sorted + (start_p - start_u)[e]
    n_tiles = a_pad // MOE_TILE
    n_valid = (ends_p[-1] // MOE_TILE).astype(jnp.int32)
    tile_start = jnp.arange(n_tiles, dtype=jnp.int32) * MOE_TILE
    tile_e = jnp.minimum(jnp.searchsorted(ends_p, tile_start, side="right"), n_experts - 1).astype(jnp.int32)
    tile_e = jnp.where(tile_start < ends_p[-1], tile_e, tile_e[jnp.maximum(n_valid - 1, 0)])
    return row_token, tile_e, n_valid.reshape(1), pos[:t], pos[t:]


def _row_gather_start(idx_ref, base, src_ref, dst_ref, sem, rows):
    def issue(r, carry):
        pltpu.make_async_copy(src_ref.at[idx_ref[base + r]], dst_ref.at[r], sem).start()
        return carry

    lax.fori_loop(0, rows, issue, 0, unroll=8)


def _row_gather_wait(src_ref, dst_ref, sem, rows):
    pltpu.make_async_copy(src_ref.at[pl.ds(0, rows)], dst_ref, sem).wait()


def _prefetched_gather(starts, waits):
    i = pl.program_id(0)
    slot = i % 2

    @pl.when(i == 0)
    def _():
        starts(0, 0)

    @pl.when(i + 1 < pl.num_programs(0))
    def _():
        starts(i + 1, 1 - slot)

    waits(slot)
    return slot


def _gather_rows_body(idx_ref, src_ref, o_ref, buf, sem, *, rows):
    def starts(step, slot):
        _row_gather_start(idx_ref, step * rows, src_ref, buf.at[slot], sem.at[slot], rows)

    def waits(slot):
        _row_gather_wait(src_ref, buf.at[slot], sem.at[slot], rows)

    slot = _prefetched_gather(starts, waits)
    o_ref[...] = buf[slot].astype(o_ref.dtype)


def _gather_rows(src, idx):
    m = idx.shape[0]
    w = src.shape[1]
    rows = GATHER_ROWS
    assert m % rows == 0 and src.shape[0] >= rows
    return pl.pallas_call(
        functools.partial(_gather_rows_body, rows=rows),
        grid_spec=pltpu.PrefetchScalarGridSpec(
            num_scalar_prefetch=1,
            grid=(m // rows,),
            in_specs=[pl.BlockSpec(memory_space=pl.ANY)],
            out_specs=pl.BlockSpec((rows, w), lambda i, idx_ref: (i, 0)),
            scratch_shapes=[pltpu.VMEM((2, rows, w), src.dtype), pltpu.SemaphoreType.DMA((2,))]),
        out_shape=jax.ShapeDtypeStruct((m, w), BF16),
        compiler_params=_cparams(("arbitrary",), rows * w * (2 * 4 + 2 * 2 + 4)),
        name="moe_gather",
    )(idx, src)


def _gmm_new_weights(te_ref):
    i = pl.program_id(1)
    return jnp.logical_or(i == 0, te_ref[i] != te_ref[jnp.maximum(i - 1, 0)])


def _gmm_swiglu_body(te_ref, nv_ref, x_ref, wg_ref, wu_ref, o_ref, wg_bf, wu_bf):
    valid = pl.program_id(1) < nv_ref[0]

    @pl.when(_gmm_new_weights(te_ref))
    def _():
        wg_bf[...] = wg_ref[...].astype(BF16)
        wu_bf[...] = wu_ref[...].astype(BF16)

    @pl.when(valid)
    def _():
        a = x_ref[...]
        g = _dot(a, wg_bf[...])
        u = _dot(a, wu_bf[...])
        o_ref[...] = (_silu(g) * u).astype(o_ref.dtype)

    @pl.when(jnp.logical_not(valid))
    def _():
        o_ref[...] = jnp.zeros_like(o_ref)


def _gmm_down_body(te_ref, nv_ref, a_ref, w_ref, o_ref, w_bf):
    valid = pl.program_id(1) < nv_ref[0]

    @pl.when(_gmm_new_weights(te_ref))
    def _():
        w_bf[...] = w_ref[...].astype(BF16)

    @pl.when(valid)
    def _():
        o_ref[...] = _dot(a_ref[...], w_bf[...])

    @pl.when(jnp.logical_not(valid))
    def _():
        o_ref[...] = jnp.zeros_like(o_ref)


def _gmm_maps():
    def rows(j, i, te, nv):
        return (jnp.minimum(i, nv[0] - 1), 0)

    def weights(j, i, te, nv):
        return (te[i], 0, j)

    def out(j, i, te, nv):
        return (i, j)

    return rows, weights, out


def _gmm_swiglu(x_sorted, wg, wu, tile_e, n_valid):
    m, k = x_sorted.shape
    n = wg.shape[2]
    tm = MOE_TILE
    tn = _pick(n, (512, 256, 128))
    rows, weights, out = _gmm_maps()
    vmem = 2 * (tm * k * 2 + 2 * k * tn * 4 + tm * tn * 2) + 2 * k * tn * 2 + 3 * tm * tn * 4
    return pl.pallas_call(
        _gmm_swiglu_body,
        grid_spec=pltpu.PrefetchScalarGridSpec(
            num_scalar_prefetch=2,
            grid=(n // tn, m // tm),
            in_specs=[pl.BlockSpec((tm, k), rows),
                      pl.BlockSpec((None, k, tn), weights),
                      pl.BlockSpec((None, k, tn), weights)],
            out_specs=pl.BlockSpec((tm, tn), out),
            scratch_shapes=[pltpu.VMEM((k, tn), BF16), pltpu.VMEM((k, tn), BF16)]),
        out_shape=jax.ShapeDtypeStruct((m, n), BF16),
        compiler_params=_cparams(("arbitrary", "arbitrary"), vmem),
        name="moe_up",
    )(tile_e, n_valid, x_sorted, wg, wu)


def _gmm_down(a_sorted, wd, tile_e, n_valid):
    m, k = a_sorted.shape
    n = wd.shape[2]
    tm = MOE_TILE
    tn = _pick(n, (1024, 512, 256, 128))
    rows, weights, out = _gmm_maps()
    vmem = 2 * (tm * k * 2 + k * tn * 4 + tm * tn * 4) + k * tn * 2 + tm * tn * 4
    return pl.pallas_call(
        _gmm_down_body,
        grid_spec=pltpu.PrefetchScalarGridSpec(
            num_scalar_prefetch=2,
            grid=(n // tn, m // tm),
            in_specs=[pl.BlockSpec((tm, k), rows),
                      pl.BlockSpec((None, k, tn), weights)],
            out_specs=pl.BlockSpec((tm, tn), out),
            scratch_shapes=[pltpu.VMEM((k, tn), BF16)]),
        out_shape=jax.ShapeDtypeStruct((m, n), F32),
        compiler_params=_cparams(("arbitrary", "arbitrary"), vmem),
        name="moe_down",
    )(tile_e, n_valid, a_sorted, wd)


def _moe_combine_body(p1_ref, p2_ref, y_ref, x_ref, gate_ref, route_ref, o_ref, buf1, buf2, sem, *, rows):
    def starts(step, slot):
        _row_gather_start(p1_ref, step * rows, y_ref, buf1.at[slot], sem.at[0, slot], rows)
        _row_gather_start(p2_ref, step * rows, y_ref, buf2.at[slot], sem.at[1, slot], rows)

    def waits(slot):
        _row_gather_wait(y_ref, buf1.at[slot], sem.at[0, slot], rows)
        _row_gather_wait(y_ref, buf2.at[slot], sem.at[1, slot], rows)

    slot = _prefetched_gather(starts, waits)
    route = route_ref[...]
    w1 = route[:, ROUTE_W1:ROUTE_W1 + 1]
    w2 = route[:, ROUTE_W2:ROUTE_W2 + 1]
    o_ref[...] = x_ref[...] + gate_ref[...] * (w1 * buf1[slot] + w2 * buf2[slot])


def _moe_combine(y_sorted, pos1, pos2, x, gate, route, seq):
    t, d = x.shape
    rows = _pick(seq, (128,))
    per_batch = seq // rows
    return pl.pallas_call(
        functools.partial(_moe_combine_body, rows=rows),
        grid_spec=pltpu.PrefetchScalarGridSpec(
            num_scalar_prefetch=2,
            grid=(t // rows,),
            in_specs=[pl.BlockSpec(memory_space=pl.ANY),
                      pl.BlockSpec((rows, d), lambda i, p1, p2: (i, 0)),
                      pl.BlockSpec((None, 1, d), lambda i, p1, p2: (i // per_batch, 0, 0)),
                      pl.BlockSpec((rows, V7X_LANES), lambda i, p1, p2: (i, 0))],
            out_specs=pl.BlockSpec((rows, d), lambda i, p1, p2: (i, 0)),
            scratch_shapes=[pltpu.VMEM((2, rows, d), F32), pltpu.VMEM((2, rows, d), F32),
                            pltpu.SemaphoreType.DMA((2, 2))]),
        out_shape=jax.ShapeDtypeStruct((t, d), F32),
        compiler_params=_cparams(("arbitrary",), 10 * rows * d * 4),
        name="moe_combine",
    )(pos1, pos2, y_sorted, x, gate, route)


_HG_LEVELS = (HG_CHUNK // HG_SUB).bit_length() - 1


def _hgrn_level_masks():
    ti = np.arange(HG_CHUNK)[:, None]
    si = np.arange(HG_CHUNK)[None, :]
    out = []
    for lvl in range(_HG_LEVELS):
        half = HG_SUB << lvl
        blk = 2 * half
        out.append((ti // blk == si // blk) & (ti % blk >= half) & (si % blk < half))
    return jnp.asarray(np.stack(out), F32)


def _hgrn_chunk(q_in, f_in, v, g_in, lb, gain, state_t, tri, ones, lvl_mask_ref):
    c = HG_CHUNK
    fg = lb + (1.0 - lb) * _sigmoid(f_in)
    log_f = jnp.log2(jnp.maximum(fg, MIN_FORGET))
    k = 1.0 - fg
    q = _silu(q_in)
    lf_hi, lf_mid, lf_lo = _split3(log_f)
    b = _dot(tri, lf_hi) + _dot(tri, lf_mid) + _dot(tri, lf_lo)
    b_last = b[c - 1:c, :]

    o = _dot_nt((q * jnp.exp2(b)).astype(BF16), state_t.astype(BF16))

    row = lax.broadcasted_iota(jnp.int32, (c, 1), 0)
    scores = jnp.zeros((c, c), F32)
    for lvl in range(_HG_LEVELS):
        half = HG_SUB << lvl
        blk = 2 * half
        bref = jnp.concatenate(
            [jnp.broadcast_to(b[p * blk + half - 1:p * blk + half, :], (blk, HG_DK)) for p in range(c // blk)],
            axis=0)
        is_q = (row & half) != 0
        e = jnp.exp2(-jnp.abs(b - bref))
        xk = jnp.where(is_q, q, k) * e
        qd = jnp.where(is_q, xk, 0.0).astype(BF16)
        kd = jnp.where(is_q, 0.0, xk).astype(BF16)
        scores = scores + _dot_nt(qd, kd) * lvl_mask_ref[lvl]
    o = o + _dot(scores.astype(BF16), v.astype(BF16))

    nb = c // HG_SUB
    b3 = b.reshape(nb, HG_SUB, HG_DK)
    q3 = q.reshape(nb, HG_SUB, HG_DK)
    k3 = k.reshape(nb, HG_SUB, HG_DK)
    v3 = v.reshape(nb, HG_SUB, HG_DK)
    t_in = lax.broadcasted_iota(jnp.int32, (nb, HG_SUB, HG_DK), 1)
    for s in range(HG_SUB):
        diff = b3 - b3[:, s:s + 1, :]
        dec = jnp.exp2(diff if s == 0 else jnp.where(t_in >= s, diff, NEG_BIG))
        m = (q3 * (k3[:, s:s + 1, :] * dec)).reshape(c, HG_DK)
        r = _dot(m.astype(BF16), ones)
        o = o + r * jnp.broadcast_to(v3[:, s:s + 1, :], (nb, HG_SUB, HG_DK)).reshape(c, HG_DK)

    kdec = (k * jnp.exp2(b_last - b)).astype(BF16)
    new_state_t = state_t * jnp.exp2(b_last) + _dot_tn(v.astype(BF16), kdec)

    ms = jnp.mean(o * o, axis=-1, keepdims=True)
    out = o * lax.rsqrt(ms + NORM_EPS) * gain * _silu(g_in)
    return out, new_state_t


def _hgrn_body(q_ref, f_ref, i_ref, g_ref, lbraw_ref, gain_ref, lvl_mask_ref, o_ref, state_ref,
               *, layer, n_chunks, group):
    @pl.when(pl.program_id(2) == 0)
    def _():
        state_ref[...] = jnp.zeros_like(state_ref)

    lbr = lbraw_ref[...]
    ex = jnp.exp(lbr - jnp.max(lbr, axis=0, keepdims=True))
    soft = ex / jnp.sum(ex, axis=0, keepdims=True)
    lb = jnp.zeros((1, group * HG_DK), F32)
    for j in range(1, layer + 1):
        lb = lb + soft[j:j + 1, :]
    gain = gain_ref[...]
    c = HG_CHUNK
    tri = (lax.broadcasted_iota(jnp.int32, (c, c), 0) >= lax.broadcasted_iota(jnp.int32, (c, c), 1)).astype(BF16)
    ones = jnp.ones((HG_DK, HG_DK), BF16)

    def chunk(ci, carry):
        sl = pl.ds(pl.multiple_of(ci * c, c), c)
        for hh in range(group):
            cs = slice(hh * HG_DK, (hh + 1) * HG_DK)
            out, new_state = _hgrn_chunk(q_ref[sl, cs].astype(F32), f_ref[sl, cs].astype(F32),
                                         i_ref[sl, cs].astype(F32), g_ref[sl, cs].astype(F32),
                                         lb[:, cs], gain, state_ref[hh], tri, ones, lvl_mask_ref)
            o_ref[sl, cs] = out.astype(o_ref.dtype)
            state_ref[hh] = new_state
        return carry

    lax.fori_loop(0, n_chunks, chunk, 0)


def _hgrn(proj, lb_raw, out_gain, layer, n_batch, seq, width):
    heads = width // HG_DK
    group = HG_GROUP if heads % HG_GROUP == 0 else 1
    hgroups = heads // group
    gw = group * HG_DK
    lc = _pick(seq, (512, 256, 128))
    per_batch = seq // lc
    depth = lb_raw.shape[0]

    def col(off):
        return pl.BlockSpec((lc, gw), lambda b, h, i: (b * per_batch + i, off * hgroups + h))

    return pl.pallas_call(
        functools.partial(_hgrn_body, layer=layer, n_chunks=lc // HG_CHUNK, group=group),
        grid=(n_batch, hgroups, per_batch),
        in_specs=[col(0), col(1), col(2), col(3),
                  pl.BlockSpec((depth, gw), lambda b, h, i: (0, h)),
                  pl.BlockSpec((1, HG_DK), lambda b, h, i: (0, 0)),
                  pl.BlockSpec((_HG_LEVELS, HG_CHUNK, HG_CHUNK), lambda b, h, i: (0, 0, 0))],
        out_specs=pl.BlockSpec((lc, gw), lambda b, h, i: (b * per_batch + i, h)),
        out_shape=jax.ShapeDtypeStruct((n_batch * seq, width), BF16),
        scratch_shapes=[pltpu.VMEM((group, HG_DK, HG_DK), F32)],
        compiler_params=_cparams(("parallel", "parallel", "arbitrary"), 16 * 1024 * 1024),
        name="hgrn2",
    )(proj, proj, proj, proj, lb_raw, out_gain.reshape(1, HG_DK), _hgrn_level_masks())


def _pool_body(u_ref, halo_ref, w_ref, scale_ref, o_ref, ext_ref, *, ts, gdim):
    i = pl.program_id(1)
    halo = halo_ref[...].astype(F32)
    ext_ref[0:POOL_HALO, :] = jnp.where(i > 0, halo, 0.0)
    ext_ref[POOL_HALO:, :] = u_ref[...].astype(F32)
    pos = (i * ts + lax.broadcasted_iota(jnp.int32, (ts, 1), 0) + 1).astype(F32)
    for g, win in enumerate(POOL_WINDOWS):
        cs = slice(g * gdim, (g + 1) * gdim)
        acc = ext_ref[POOL_HALO:, cs]
        for j in range(1, win):
            acc = acc + ext_ref[POOL_HALO - j:POOL_HALO - j + ts, cs]
        pooled = acc / jnp.minimum(pos, float(win)) - ext_ref[POOL_HALO:, cs]
        y = _dot(pooled.astype(BF16), w_ref[g])
        o_ref[:, cs] = (y * scale_ref[:, cs]).astype(o_ref.dtype)


def _pool(proj, col_off, pool_w, pool_scale, n_batch, seq, width):
    groups = len(POOL_WINDOWS)
    gdim = width // groups
    assert gdim % V7X_LANES == 0 and col_off % width == 0 and max(POOL_WINDOWS) <= POOL_HALO
    ts = _pick(seq, (512, 256, 128))
    per_batch = seq // ts
    cb = col_off // width
    hb = ts // POOL_HALO
    return pl.pallas_call(
        functools.partial(_pool_body, ts=ts, gdim=gdim),
        grid=(n_batch, per_batch),
        in_specs=[pl.BlockSpec((ts, width), lambda b, i: (b * per_batch + i, cb)),
                  pl.BlockSpec((POOL_HALO, width),
                               lambda b, i: (jnp.maximum((b * per_batch + i) * hb - 1, 0), cb)),
                  pl.BlockSpec((groups, gdim, gdim), lambda b, i: (0, 0, 0)),
                  pl.BlockSpec((1, width), lambda b, i: (0, 0))],
        out_specs=pl.BlockSpec((ts, width), lambda b, i: (b * per_batch + i, 0)),
        out_shape=jax.ShapeDtypeStruct((n_batch * seq, width), BF16),
        scratch_shapes=[pltpu.VMEM((ts + POOL_HALO, width), F32)],
        compiler_params=_cparams(("parallel", "parallel"), 16 * 1024 * 1024),
        name="pool",
    )(proj, proj, pool_w, pool_scale.reshape(1, width))


def _rope_table_body(pos_ref, cos_ref, sin_ref):
    pos = pos_ref[...].astype(F32)
    lane = lax.broadcasted_iota(jnp.int32, (1, V7X_LANES), 1)
    j = lane % MLA_ROPE
    fidx = (j % (MLA_ROPE // 2)).astype(F32)
    inv_freq = jnp.exp(fidx * (-2.0 / MLA_ROPE * math.log(ROPE_THETA)))
    ang = pos * inv_freq
    cos_ref[...] = jnp.cos(ang)
    sin_ref[...] = jnp.where(j < MLA_ROPE // 2, -1.0, 1.0) * jnp.sin(ang)


def _rope_table(positions):
    t = positions.size
    ts = _pick(t, (512, 256, 128))
    return pl.pallas_call(
        _rope_table_body,
        grid=(t // ts,),
        in_specs=[pl.BlockSpec((ts, 1), lambda i: (i, 0))],
        out_specs=[pl.BlockSpec((ts, V7X_LANES), lambda i: (i, 0))] * 2,
        out_shape=[jax.ShapeDtypeStruct((t, V7X_LANES), F32)] * 2,
        compiler_params=_cparams(("parallel",), 4 * 1024 * 1024),
        name="rope_table",
    )(positions.reshape(t, 1))


def _rms(x, gain):
    return x * lax.rsqrt(jnp.mean(x * x, axis=-1, keepdims=True) + NORM_EPS) * gain


def _prep_q_body(cq_ref, qn_ref, w_ref, gn_ref, gr_ref, grs_ref, seg_ref, cos_ref, sin_ref, o_ref,
                 *, heads, sm_scale):
    hn = heads * MLA_NOPE
    hr = heads * MLA_ROPE
    hq = _rms(cq_ref[...].astype(F32), qn_ref[...]).astype(BF16)
    y = _dot(hq, w_ref[...])
    yr = y[:, hn:hn + hr]
    ys = y[:, hn + hr:]
    sq_hi, sq_lo = _split2(yr * yr)
    seg = seg_ref[...]
    ss = _dot(sq_hi, seg) + _dot(sq_lo, seg)
    inv = lax.rsqrt(ss * (1.0 / MLA_ROPE) + NORM_EPS)
    reps = hr // V7X_LANES
    cosf = jnp.concatenate([cos_ref[...]] * reps, axis=1)
    sinf = jnp.concatenate([sin_ref[...]] * reps, axis=1)
    qr = (yr * inv * gr_ref[...]) * cosf + (ys * inv * grs_ref[...]) * sinf
    gn = gn_ref[...] * sm_scale
    for h in range(heads):
        qn = _rms(y[:, h * MLA_NOPE:(h + 1) * MLA_NOPE], gn)
        o_ref[h, :, 0:MLA_NOPE] = qn.astype(o_ref.dtype)
        o_ref[h, :, MLA_NOPE:] = (qr[:, h * MLA_ROPE:(h + 1) * MLA_ROPE] * sm_scale).astype(o_ref.dtype)


def _prep_kv_body(ckv_ref, kpe_ref, kvn_ref, w_ref, gn_ref, gr2_ref, cos_ref, sin_ref, k_ref, v_ref, *, heads):
    hk = _rms(ckv_ref[...].astype(F32), kvn_ref[...]).astype(BF16)
    y = _dot(hk, w_ref[...])
    kp = kpe_ref[...].astype(F32)
    kpe = kp[:, :MLA_ROPE]
    inv = lax.rsqrt(jnp.mean(kpe * kpe, axis=-1, keepdims=True) + NORM_EPS)
    kn = kp * inv * gr2_ref[...]
    kr = (kn[:, :MLA_ROPE] * cos_ref[:, :MLA_ROPE] + kn[:, MLA_ROPE:] * sin_ref[:, :MLA_ROPE]).astype(k_ref.dtype)
    per = MLA_NOPE + MLA_V
    for h in range(heads):
        k_ref[h, :, 0:MLA_NOPE] = _rms(y[:, h * per:h * per + MLA_NOPE], gn_ref[...]).astype(k_ref.dtype)
        k_ref[h, :, MLA_NOPE:] = kr
        v_ref[h, 0, :MLA_V, :] = y[:, h * per + MLA_NOPE:(h + 1) * per].T.astype(v_ref.dtype)
        v_ref[h, 0, MLA_V:, :] = jnp.ones((V7X_BF16_ROWS, y.shape[0]), v_ref.dtype)


def _flash_body(q_ref, k_ref, vt_ref, o_ref, acc_ref, st_a, st_b, *, tile, group):
    qi = pl.program_id(2)
    acc_ref[...] = jnp.zeros_like(acc_ref)

    def scores(j, dst):
        ks = pl.ds(pl.multiple_of(j * tile, tile), tile)
        for g in range(group):
            dst[g] = _dot_nt(k_ref[g, ks, :], q_ref[g])

    def consume(j, src, ms, masked):
        m_new = []
        for g in range(group):
            st = src[g]
            if masked:
                kidx = lax.broadcasted_iota(jnp.int32, (tile, tile), 0)
                qidx = lax.broadcasted_iota(jnp.int32, (tile, tile), 1)
                st = jnp.where(kidx <= qidx, st, NEG_BIG)
            m = jnp.maximum(ms[g], jnp.max(st, axis=0, keepdims=True))
            p = jnp.exp2(st - m).astype(BF16)
            acc_ref[g] = jnp.exp2(ms[g] - m) * acc_ref[g] + _dot(vt_ref[g, j], p)
            m_new.append(m)
        return tuple(m_new)

    def finish():
        for g in range(group):
            o = acc_ref[g, :MLA_V, :] / acc_ref[g, MLA_V:MLA_V + 1, :]
            o_ref[:, g * MLA_V:(g + 1) * MLA_V] = o.T.astype(o_ref.dtype)

    scores(0, st_a)

    def pair(pi, ms):
        j = 2 * pi
        scores(j + 1, st_b)
        ms = consume(j, st_a, ms, False)
        scores(j + 2, st_a)
        return consume(j + 1, st_b, ms, False)

    m0 = tuple(jnp.full((1, tile), NEG_BIG, F32) for _ in range(group))
    ms = lax.fori_loop(0, qi // 2, pair, m0)
    odd = qi % 2 == 1

    @pl.when(odd)
    def _():
        scores(qi, st_b)
        consume(qi, st_b, consume(qi - 1, st_a, ms, False), True)
        finish()

    @pl.when(jnp.logical_not(odd))
    def _():
        consume(qi, st_a, ms, True)
        finish()


def _mla(proj_c, cos_t, sin_t, q_norm, w_q, kv_norm, w_kv, g_qn, g_qr, g_kn, g_kr,
         n_batch, seq, heads, q_rank, kv_rank):
    t = n_batch * seq
    half = MLA_ROPE // 2
    hn, hr = heads * MLA_NOPE, heads * MLA_ROPE
    dq = MLA_NOPE + MLA_ROPE
    sm_scale = float(dq) ** -0.5 * math.log2(math.e)
    assert q_rank % kv_rank == 0 and (q_rank + kv_rank) % V7X_LANES == 0 and hr % V7X_LANES == 0
    tile = ATTN_TILE
    assert seq % tile == 0
    tm = tile
    vrows = MLA_V + V7X_BF16_ROWS

    def swap(g):
        return jnp.concatenate([g[half:], g[:half]])

    gr = jnp.tile(g_qr, heads).reshape(1, hr)
    grs = jnp.tile(swap(g_qr), heads).reshape(1, hr)
    lane = jnp.arange(hr) // MLA_ROPE
    seg = (lane[:, None] == lane[None, :]).astype(BF16)
    row = lambda i: (i, 0)
    const2 = lambda i: (0, 0)
    qcat = pl.pallas_call(
        functools.partial(_prep_q_body, heads=heads, sm_scale=sm_scale),
        grid=(t // tm,),
        in_specs=[pl.BlockSpec((tm, q_rank), row),
                  pl.BlockSpec((1, q_rank), const2),
                  pl.BlockSpec((q_rank, hn + 2 * hr), const2),
                  pl.BlockSpec((1, MLA_NOPE), const2),
                  pl.BlockSpec((1, hr), const2),
                  pl.BlockSpec((1, hr), const2),
                  pl.BlockSpec((hr, hr), const2),
                  pl.BlockSpec((tm, V7X_LANES), row),
                  pl.BlockSpec((tm, V7X_LANES), row)],
        out_specs=pl.BlockSpec((heads, tm, dq), lambda i: (0, i, 0)),
        out_shape=jax.ShapeDtypeStruct((heads, t, dq), BF16),
        compiler_params=_cparams(("parallel",), 40 * 1024 * 1024),
        name="mla_prep_q",
    )(proj_c, q_norm.reshape(1, q_rank), w_q, g_qn.reshape(1, MLA_NOPE), gr, grs, seg, cos_t, sin_t)

    gr2 = jnp.concatenate([g_kr, swap(g_kr)]).reshape(1, 2 * MLA_ROPE)
    kcat, vt = pl.pallas_call(
        functools.partial(_prep_kv_body, heads=heads),
        grid=(t // tm,),
        in_specs=[pl.BlockSpec((tm, kv_rank), lambda i: (i, q_rank // kv_rank)),
                  pl.BlockSpec((tm, V7X_LANES), lambda i: (i, (q_rank + kv_rank) // V7X_LANES)),
                  pl.BlockSpec((1, kv_rank), const2),
                  pl.BlockSpec((kv_rank, heads * (MLA_NOPE + MLA_V)), const2),
                  pl.BlockSpec((1, MLA_NOPE), const2),
                  pl.BlockSpec((1, 2 * MLA_ROPE), const2),
                  pl.BlockSpec((tm, V7X_LANES), row),
                  pl.BlockSpec((tm, V7X_LANES), row)],
        out_specs=[pl.BlockSpec((heads, tm, dq), lambda i: (0, i, 0)),
                   pl.BlockSpec((heads, 1, vrows, tile), lambda i: (0, i, 0, 0))],
        out_shape=[jax.ShapeDtypeStruct((heads, t, dq), BF16),
                   jax.ShapeDtypeStruct((heads, t // tile, vrows, tile), BF16)],
        compiler_params=_cparams(("parallel",), 40 * 1024 * 1024),
        name="mla_prep_kv",
    )(proj_c, proj_c, kv_norm.reshape(1, kv_rank), w_kv, g_kn.reshape(1, MLA_NOPE), gr2, cos_t, sin_t)

    nq = seq // tile
    group = ATTN_GROUP if heads % ATTN_GROUP == 0 else 1
    return pl.pallas_call(
        functools.partial(_flash_body, tile=tile, group=group),
        grid=(heads // group, n_batch, nq),
        in_specs=[pl.BlockSpec((group, tile, dq), lambda h, b, i: (h, b * nq + i, 0)),
                  pl.BlockSpec((group, seq, dq), lambda h, b, i: (h, b, 0)),
                  pl.BlockSpec((group, nq, vrows, tile), lambda h, b, i: (h, b, 0, 0))],
        out_specs=pl.BlockSpec((tile, group * MLA_V), lambda h, b, i: (b * nq + i, h)),
        out_shape=jax.ShapeDtypeStruct((t, heads * MLA_V), BF16),
        scratch_shapes=[pltpu.VMEM((group, vrows, tile), F32),
                        pltpu.VMEM((group, tile, tile), F32), pltpu.VMEM((group, tile, tile), F32)],
        compiler_params=_cparams(("parallel", "parallel", "arbitrary"), 48 * 1024 * 1024),
        name="mla_flash",
    )(qcat, kcat, vt)


def kernel(x, c, positions, ada_w, ada_b, ada_layer, mix_norm, ffn_norm, w_in, hgrn_lower_bounds,
           hgrn_out_norm, pool_w, pool_scale, mla_q_norm, mla_w_uq, mla_kv_norm, mla_w_ukv,
           mla_qk_norm_q_nope, mla_qk_norm_q_rope, mla_qk_norm_k_nope, mla_qk_norm_k_rope,
           w_branch_a, w_branch_b, w_branch_c, w_o, ffn_w_gate, ffn_w_up, ffn_w_down,
           moe_router, moe_w_gate, moe_w_up, moe_w_down):
    n_batch, seq, d = x.shape
    depth = w_in.shape[0]
    t = n_batch * seq
    hg_width = hgrn_lower_bounds.shape[1]
    pool_width = pool_scale.shape[1]
    q_rank = mla_q_norm.shape[1]
    kv_rank = mla_kv_norm.shape[1]
    heads = mla_w_ukv.shape[2] // (MLA_NOPE + MLA_V)
    half = MLA_ROPE // 2
    n_a = 4 * hg_width + pool_width
    n_c = q_rank + kv_rank + MLA_ROPE
    assert w_in.shape[2] == n_a + n_c + 3 * d

    mod = _ada(c, ada_w, ada_b, ada_layer)
    cos_t, sin_t = _rope_table(positions)
    xf = x.reshape(t, d)

    for l in range(depth):
        mod_l = mod[l]
        gate1 = mod_l[:, 2:3, :]
        gate2 = mod_l[:, 5:6, :]

        w_l = w_in[l]
        w_a = w_l[:, :n_a].astype(BF16)
        kpe_w = w_l[:, n_a + q_rank + kv_rank:n_a + n_c]
        w_c = jnp.concatenate([w_l[:, n_a:n_a + n_c], kpe_w[:, half:], kpe_w[:, :half]], axis=1).astype(BF16)
        w_g = w_l[:, n_a + n_c:].astype(BF16)
        wq = mla_w_uq[l].reshape(q_rank, heads, MLA_NOPE + MLA_ROPE)
        wq_r = wq[:, :, MLA_NOPE:]
        w_q = jnp.concatenate([
            wq[:, :, :MLA_NOPE].reshape(q_rank, heads * MLA_NOPE),
            wq_r.reshape(q_rank, heads * MLA_ROPE),
            jnp.concatenate([wq_r[:, :, half:], wq_r[:, :, :half]], axis=2).reshape(q_rank, heads * MLA_ROPE),
        ], axis=1).astype(BF16)

        h = _modulate(xf.reshape(n_batch, seq, d), mix_norm[l], mod_l, 0, 1)
        proj_a = _mm_cast(h, w_a, name="proj_a")
        proj_c = _mm_cast(h, w_c, name="proj_c")
        gates = _mm_cast(h, w_g, name="proj_gates")

        o_a = _hgrn(proj_a, hgrn_lower_bounds, hgrn_out_norm[l], l, n_batch, seq, hg_width)
        o_b = _pool(proj_a, 4 * hg_width, pool_w[l].astype(BF16), pool_scale[l], n_batch, seq, pool_width)
        o_c = _mla(proj_c, cos_t, sin_t, mla_q_norm[l], w_q, mla_kv_norm[l], mla_w_ukv[l].astype(BF16),
                   mla_qk_norm_q_nope[l], mla_qk_norm_q_rope[l], mla_qk_norm_k_nope[l], mla_qk_norm_k_rope[l],
                   n_batch, seq, heads, q_rank, kv_rank)
        merged = _merge(o_a, o_b, o_c, w_branch_a[l].astype(BF16), w_branch_b[l].astype(BF16),
                        w_branch_c[l].astype(BF16), gates)
        xf = _mm_residual(merged, w_o[l].astype(BF16), xf, gate1, seq, name="out_proj")

        j = l // 2
        if l % 2 == 0:
            h = _modulate(xf.reshape(n_batch, seq, d), ffn_norm[l], mod_l, 3, 4)
            act = _mm_swiglu(h, ffn_w_gate[j].astype(BF16), ffn_w_up[j].astype(BF16), name="ffn_up")
            xf = _mm_residual(act, ffn_w_down[j].astype(BF16), xf, gate2, seq, name="ffn_down")
        else:
            n_experts = moe_w_gate.shape[1]
            h, route = _modulate(xf.reshape(n_batch, seq, d), ffn_norm[l], mod_l, 3, 4, router=moe_router[j])
            row_token, tile_e, n_valid, pos1, pos2 = _route_metadata(route, n_experts)
            h_sorted = _gather_rows(h, row_token)
            act = _gmm_swiglu(h_sorted, moe_w_gate[j], moe_w_up[j], tile_e, n_valid)
            y_sorted = _gmm_down(act, moe_w_down[j], tile_e, n_valid)
            xf = _moe_combine(y_sorted, pos1, pos2, xf, gate2, route, seq)
    return xf.reshape(n_batch, seq, d)
```

```python
import functools
import math

import numpy as np
import jax
import jax.numpy as jnp
from jax import lax
from jax.experimental import pallas as pl
from jax.experimental.pallas import tpu as pltpu

F32 = jnp.float32
BF16 = jnp.bfloat16

HG_DK = 128
POOL_WINDOWS = (2, 4, 8, 16)
MLA_NOPE = 128
MLA_ROPE = 64
MLA_V = 128
ROPE_THETA = 10000.0
MIN_FORGET = 1e-30
NORM_EPS = 1e-6
N_MOD = 6
TOP_K = 2
NEG_BIG = -1e30

V7X_LANES = 128
V7X_SUBLANES = 8
V7X_BF16_ROWS = 16
V7X_VMEM_BYTES = 64 * 1024 * 1024
VMEM_CAP = V7X_VMEM_BYTES - 8 * 1024 * 1024

HG_CHUNK = 128
HG_SUB = 8
HG_GROUP = 4
POOL_HALO = 16
ADA_CHUNK = 512
ATTN_TILE = 1024
ATTN_GROUP = 2
ROUTE_E1, ROUTE_E2, ROUTE_W1, ROUTE_W2 = 0, 1, 2, 3
MOE_TILE = 512
GATHER_ROWS = 512


def _pick(n, prefs):
    for p in prefs:
        if n % p == 0:
            return p
    raise ValueError(f"no tile in {prefs} divides {n}")


def _cparams(sem, vmem_bytes):
    limit = int(min(VMEM_CAP, max(32 * 1024 * 1024, vmem_bytes * 5 // 4)))
    return pltpu.CompilerParams(dimension_semantics=sem, vmem_limit_bytes=limit)


def _sigmoid(x):
    return 1.0 / (1.0 + jnp.exp(-x))


def _silu(x):
    return x * _sigmoid(x)


def _dot(a, b):
    return jnp.dot(a, b, preferred_element_type=F32)


def _dot_nt(a, b):
    return lax.dot_general(a, b, (((1,), (1,)), ((), ())), preferred_element_type=F32)


def _dot_tn(a, b):
    return lax.dot_general(a, b, (((0,), (0,)), ((), ())), preferred_element_type=F32)


def _split3(x):
    hi = x.astype(BF16)
    r1 = x - hi.astype(F32)
    mid = r1.astype(BF16)
    lo = (r1 - mid.astype(F32)).astype(BF16)
    return hi, mid, lo


def _split2(x):
    hi = x.astype(BF16)
    lo = (x - hi.astype(F32)).astype(BF16)
    return hi, lo


def _ada_body(ct_ref, w_ref, b_ref, lay_ref, o_ref, acc_ref, *, n_batch, depth):
    kk = pl.program_id(1)
    tk, tn = w_ref.shape
    sub = V7X_SUBLANES

    @pl.when(kk == 0)
    def _():
        acc_ref[...] = jnp.zeros_like(acc_ref)

    s = _silu(ct_ref[pl.ds(pl.multiple_of(kk * tk, tk), tk), :])
    for c0 in range(0, tn, ADA_CHUNK):
        w = w_ref[:, c0:c0 + ADA_CHUNK]
        for b in range(n_batch):
            prod = (w * s[:, b:b + 1]).reshape(tk // sub, sub, ADA_CHUNK)
            acc_ref[b, :, c0:c0 + ADA_CHUNK] += jnp.sum(prod, axis=0)

    @pl.when(kk == pl.num_programs(1) - 1)
    def _():
        for b in range(n_batch):
            r = jnp.sum(acc_ref[b], axis=0, keepdims=True) + b_ref[...]
            for l in range(depth):
                o_ref[l, b:b + 1, :] = r + lay_ref[l:l + 1, :]


def _ada(c, ada_w, ada_b, ada_layer):
    n_batch, d = c.shape
    depth = ada_layer.shape[0]
    n = ada_w.shape[1]
    tn = n
    tk = _pick(d, (64, 32, 16, 8))
    assert tn % ADA_CHUNK == 0
    ct = c.T
    lay = ada_layer.reshape(depth, n)
    out = pl.pallas_call(
        functools.partial(_ada_body, n_batch=n_batch, depth=depth),
        scratch_shapes=[pltpu.VMEM((n_batch, V7X_SUBLANES, tn), F32)],
        grid=(n // tn, d // tk),
        in_specs=[
            pl.BlockSpec((d, n_batch), lambda j, k: (0, 0)),
            pl.BlockSpec((tk, tn), lambda j, k: (k, j)),
            pl.BlockSpec((1, tn), lambda j, k: (0, j)),
            pl.BlockSpec((depth, tn), lambda j, k: (0, j)),
        ],
        out_specs=pl.BlockSpec((depth, n_batch, tn), lambda j, k: (0, 0, j)),
        out_shape=jax.ShapeDtypeStruct((depth, n_batch, n), F32),
        compiler_params=_cparams(("parallel", "arbitrary"), 4 * tk * tn * 4),
        name="ada",
    )(ct, ada_w, ada_b.reshape(1, n), lay)
    return out.reshape(depth, n_batch, N_MOD, d)


def _modulated(x_ref, g_ref, mod_ref, shift_idx, scale_idx):
    x = x_ref[...]
    ms = jnp.mean(x * x, axis=-1, keepdims=True)
    y = x * lax.rsqrt(ms + NORM_EPS) * g_ref[...]
    return y * (1.0 + mod_ref[scale_idx:scale_idx + 1, :]) + mod_ref[shift_idx:shift_idx + 1, :]


def _modulate_body(x_ref, g_ref, mod_ref, o_ref, *, shift_idx, scale_idx):
    o_ref[...] = _modulated(x_ref, g_ref, mod_ref, shift_idx, scale_idx).astype(o_ref.dtype)


def _modulate_route_body(x_ref, g_ref, mod_ref, r_ref, o_ref, route_ref, *, shift_idx, scale_idx, n_experts):
    h = _modulated(x_ref, g_ref, mod_ref, shift_idx, scale_idx)
    o_ref[...] = h
    h_hi, h_mid, h_lo = _split3(h)
    r = r_ref[...]
    r_hi, r_mid, r_lo = _split3(r)
    logits = (_dot(h_hi, r_hi) + _dot(h_hi, r_mid) + _dot(h_mid, r_hi)
              + _dot(h_hi, r_lo) + _dot(h_mid, r_mid) + _dot(h_lo, r_hi))
    lane = lax.broadcasted_iota(jnp.int32, logits.shape, 1).astype(F32)
    lg = jnp.where(lane < n_experts, logits, -jnp.inf)
    m1 = jnp.max(lg, axis=-1, keepdims=True)
    i1 = jnp.min(jnp.where(lg == m1, lane, float(V7X_LANES)), axis=-1, keepdims=True)
    lg2 = jnp.where(lane == i1, -jnp.inf, lg)
    m2 = jnp.max(lg2, axis=-1, keepdims=True)
    i2 = jnp.min(jnp.where(lg2 == m2, lane, float(V7X_LANES)), axis=-1, keepdims=True)
    e2 = jnp.exp(m2 - m1)
    w1 = 1.0 / (1.0 + e2)
    w2 = e2 / (1.0 + e2)
    route_ref[...] = (jnp.where(lane == ROUTE_E1, i1, 0.0) + jnp.where(lane == ROUTE_E2, i2, 0.0)
                      + jnp.where(lane == ROUTE_W1, w1, 0.0) + jnp.where(lane == ROUTE_W2, w2, 0.0))


def _modulate(x3, gain, mod_l, shift_idx, scale_idx, router=None):
    n_batch, seq, d = x3.shape
    ts = _pick(seq, (512, 256, 128))
    grid = (n_batch, seq // ts)
    x_spec = pl.BlockSpec((None, ts, d), lambda b, i: (b, i, 0))
    g_spec = pl.BlockSpec((1, d), lambda b, i: (0, 0))
    mod_spec = pl.BlockSpec((None, N_MOD, d), lambda b, i: (b, 0, 0))
    h_spec = pl.BlockSpec((None, ts, d), lambda b, i: (b, i, 0))
    vmem = 2 * ts * d * (4 + 2) + 4 * ts * d * 4
    if router is None:
        h = pl.pallas_call(
            functools.partial(_modulate_body, shift_idx=shift_idx, scale_idx=scale_idx),
            grid=grid,
            in_specs=[x_spec, g_spec, mod_spec],
            out_specs=h_spec,
            out_shape=jax.ShapeDtypeStruct((n_batch, seq, d), BF16),
            compiler_params=_cparams(("parallel", "parallel"), vmem),
            name="modulate",
        )(x3, gain.reshape(1, d), mod_l)
        return h.reshape(n_batch * seq, d)
    n_experts = router.shape[1]
    assert n_experts <= V7X_LANES
    r_pad = jnp.zeros((d, V7X_LANES), F32).at[:, :n_experts].set(router)
    h, route = pl.pallas_call(
        functools.partial(_modulate_route_body, shift_idx=shift_idx, scale_idx=scale_idx, n_experts=n_experts),
        grid=grid,
        in_specs=[x_spec, g_spec, mod_spec, pl.BlockSpec((d, V7X_LANES), lambda b, i: (0, 0))],
        out_specs=[h_spec, pl.BlockSpec((None, ts, V7X_LANES), lambda b, i: (b, i, 0))],
        out_shape=[jax.ShapeDtypeStruct((n_batch, seq, d), F32),
                   jax.ShapeDtypeStruct((n_batch, seq, V7X_LANES), F32)],
        compiler_params=_cparams(("parallel", "parallel"), vmem + 2 * ts * d * 2 + 6 * ts * d * 2),
        name="modulate_route",
    )(x3, gain.reshape(1, d), mod_l, r_pad)
    return h.reshape(n_batch * seq, d), route.reshape(n_batch * seq, V7X_LANES)


def _mm_cast_body(a_ref, w_ref, o_ref):
    o_ref[...] = _dot(a_ref[...], w_ref[...]).astype(o_ref.dtype)


def _mm_cast(a, w, out_dtype=BF16, name="mm"):
    m, k = a.shape
    n = w.shape[1]
    tn = n if n <= 2048 else _pick(n, (1024, 512, 256, 128))
    need = lambda tm_: 2 * (tm_ * k * 2 + k * tn * 2 + tm_ * tn * 2) + tm_ * tn * 4
    tm = next(t_ for t_ in (1024, 512, 256, 128) if m % t_ == 0 and need(t_) * 5 // 4 <= VMEM_CAP)
    vmem = need(tm)
    return pl.pallas_call(
        _mm_cast_body,
        grid=(m // tm, n // tn),
        in_specs=[pl.BlockSpec((tm, k), lambda i, j: (i, 0)),
                  pl.BlockSpec((k, tn), lambda i, j: (0, j))],
        out_specs=pl.BlockSpec((tm, tn), lambda i, j: (i, j)),
        out_shape=jax.ShapeDtypeStruct((m, n), out_dtype),
        compiler_params=_cparams(("parallel", "arbitrary"), vmem),
        name=name,
    )(a, w)


def _mm_swiglu_body(a_ref, wg_ref, wu_ref, o_ref):
    a = a_ref[...]
    g = _dot(a, wg_ref[...])
    u = _dot(a, wu_ref[...])
    o_ref[...] = (_silu(g) * u).astype(o_ref.dtype)


def _mm_swiglu(a, wg, wu, name="swiglu"):
    m, k = a.shape
    n = wg.shape[1]
    tn = _pick(n, (512, 256, 128))
    need = lambda tm_: 2 * (tm_ * k * 2 + 2 * k * tn * 2 + tm_ * tn * 2) + 3 * tm_ * tn * 4
    tm = next(t_ for t_ in (2048, 1024, 512, 256, 128) if m % t_ == 0 and need(t_) * 10 // 9 <= VMEM_CAP)
    vmem = need(tm)
    return pl.pallas_call(
        _mm_swiglu_body,
        grid=(m // tm, n // tn),
        in_specs=[pl.BlockSpec((tm, k), lambda i, j: (i, 0)),
                  pl.BlockSpec((k, tn), lambda i, j: (0, j)),
                  pl.BlockSpec((k, tn), lambda i, j: (0, j))],
        out_specs=pl.BlockSpec((tm, tn), lambda i, j: (i, j)),
        out_shape=jax.ShapeDtypeStruct((m, n), BF16),
        compiler_params=_cparams(("parallel", "arbitrary"), vmem),
        name=name,
    )(a, wg, wu)


def _mm_residual_body(a_ref, w_ref, x_ref, gate_ref, o_ref, *, nk):
    scale = gate_ref[...]
    part = _dot(a_ref[...], w_ref[...])
    if nk == 1:
        o_ref[...] = x_ref[...] + scale * part
    else:
        kk = pl.program_id(2)

        @pl.when(kk == 0)
        def _():
            o_ref[...] = part

        @pl.when(jnp.logical_and(kk > 0, kk < nk - 1))
        def _():
            o_ref[...] += part

        @pl.when(kk == nk - 1)
        def _():
            o_ref[...] = x_ref[...] + scale * (o_ref[...] + part)


def _mm_residual(a, w, x, gate, seq, name="mm_res"):
    m, k = a.shape
    n = w.shape[1]
    tm = _pick(seq, (1024, 512, 256, 128))
    tn = _pick(n, (1024, 512, 256, 128))
    if k <= 4096:
        tk = k
    else:
        tk = next(t for t in range(4096 // V7X_LANES * V7X_LANES, 0, -V7X_LANES) if k % t == 0)
        if tk < 512:
            tk = next(t for t in range(k // 2 // V7X_LANES * V7X_LANES, 0, -V7X_LANES) if k % t == 0)
    nk = k // tk
    if tk > 4096:
        tn = _pick(n, (512, 256, 128))
    per_batch = seq // tm
    in_specs = [pl.BlockSpec((tm, tk), lambda i, j, kk: (i, kk)),
                pl.BlockSpec((tk, tn), lambda i, j, kk: (kk, j)),
                pl.BlockSpec((tm, tn), lambda i, j, kk: (i, j)),
                pl.BlockSpec((None, 1, tn), lambda i, j, kk: (i // per_batch, 0, j))]
    vmem = 2 * (tm * tk * 2 + tk * tn * 2 + 2 * tm * tn * 4) + 2 * tm * tn * 4
    return pl.pallas_call(
        functools.partial(_mm_residual_body, nk=nk),
        grid=(m // tm, n // tn, nk),
        in_specs=in_specs,
        out_specs=pl.BlockSpec((tm, tn), lambda i, j, kk: (i, j)),
        out_shape=jax.ShapeDtypeStruct((m, n), F32),
        compiler_params=_cparams(("parallel", "parallel", "arbitrary"), vmem),
        name=name,
    )(a, w, x, gate)


def _merge_body(a_ref, b_ref, c_ref, wa_ref, wb_ref, wc_ref, ga_ref, gb_ref, gc_ref, o_ref):
    ya = _dot(a_ref[...], wa_ref[...])
    yb = _dot(b_ref[...], wb_ref[...])
    yc = _dot(c_ref[...], wc_ref[...])
    out = (_sigmoid(ga_ref[...].astype(F32)) * ya + _sigmoid(gb_ref[...].astype(F32)) * yb
           + _sigmoid(gc_ref[...].astype(F32)) * yc)
    o_ref[...] = out.astype(o_ref.dtype)


def _merge(o_a, o_b, o_c, w_a, w_b, w_c, gates):
    m = o_a.shape[0]
    d = w_a.shape[1]
    tm = _pick(m, (1024, 512, 256, 128))
    tn = _pick(d, (512, 256, 128))
    nj = d // tn
    ka, kb, kc = o_a.shape[1], o_b.shape[1], o_c.shape[1]
    vmem = 2 * 2 * (tm * (ka + kb + kc) + (ka + kb + kc) * tn + 4 * tm * tn) + 6 * tm * tn * 4
    return pl.pallas_call(
        _merge_body,
        grid=(m // tm, nj),
        in_specs=[pl.BlockSpec((tm, ka), lambda i, j: (i, 0)),
                  pl.BlockSpec((tm, kb), lambda i, j: (i, 0)),
                  pl.BlockSpec((tm, kc), lambda i, j: (i, 0)),
                  pl.BlockSpec((ka, tn), lambda i, j: (0, j)),
                  pl.BlockSpec((kb, tn), lambda i, j: (0, j)),
                  pl.BlockSpec((kc, tn), lambda i, j: (0, j)),
                  pl.BlockSpec((tm, tn), lambda i, j: (i, j)),
                  pl.BlockSpec((tm, tn), lambda i, j: (i, nj + j)),
                  pl.BlockSpec((tm, tn), lambda i, j: (i, 2 * nj + j))],
        out_specs=pl.BlockSpec((tm, tn), lambda i, j: (i, j)),
        out_shape=jax.ShapeDtypeStruct((m, d), BF16),
        compiler_params=_cparams(("parallel", "arbitrary"), vmem),
        name="merge",
    )(o_a, o_b, o_c, w_a, w_b, w_c, gates, gates, gates)


def _route_metadata(route, n_experts):
    t = route.shape[0]
    a_tot = TOP_K * t
    a_pad = a_tot + n_experts * MOE_TILE
    e = jnp.concatenate([route[:, ROUTE_E1], route[:, ROUTE_E2]]).astype(jnp.int32)
    order = jnp.argsort(e, stable=True).astype(jnp.int32)
    counts = jnp.sum((e[:, None] == jnp.arange(n_experts, dtype=jnp.int32)[None, :]).astype(jnp.int32), axis=0)
    padded = (counts + MOE_TILE - 1) // MOE_TILE * MOE_TILE
    ends_u = jnp.cumsum(counts)
    ends_p = jnp.cumsum(padded)
    start_u = ends_u - counts
    start_p = ends_p - padded
    p = jnp.arange(a_pad, dtype=jnp.int32)
    ep = jnp.minimum(jnp.searchsorted(ends_p, p, side="right"), n_experts - 1).astype(jnp.int32)
    rank = p - start_p[ep]
    valid = jnp.logical_and(rank < counts[ep], p < ends_p[-1])
    src = jnp.clip(start_u[ep] + rank, 0, a_tot - 1)
    row_token = jnp.where(valid, order[src] % t, 0).astype(jnp.int32)
    rank_sorted = jnp.argsort(order).astype(jnp.int32)
    pos = rank_sorted + (start_p - start_u)[e]
    n_tiles = a_pad // MOE_TILE
    n_valid = (ends_p[-1] // MOE_TILE).astype(jnp.int32)
    tile_start = jnp.arange(n_tiles, dtype=jnp.int32) * MOE_TILE
    tile_e = jnp.minimum(jnp.searchsorted(ends_p, tile_start, side="right"), n_experts - 1).astype(jnp.int32)
    tile_e = jnp.where(tile_start < ends_p[-1], tile_e, tile_e[jnp.maximum(n_valid - 1, 0)])
    return row_token, tile_e, n_valid.reshape(1), pos[:t], pos[t:]


def _row_gather_start(idx_ref, base, src_ref, dst_ref, sem, rows):
    def issue(r, carry):
        pltpu.make_async_copy(src_ref.at[idx_ref[base + r]], dst_ref.at[r], sem).start()
        return carry

    lax.fori_loop(0, rows, issue, 0, unroll=8)


def _row_gather_wait(src_ref, dst_ref, sem, rows):
    pltpu.make_async_copy(src_ref.at[pl.ds(0, rows)], dst_ref, sem).wait()


def _prefetched_gather(starts, waits):
    i = pl.program_id(0)
    slot = i % 2

    @pl.when(i == 0)
    def _():
        starts(0, 0)

    @pl.when(i + 1 < pl.num_programs(0))
    def _():
        starts(i + 1, 1 - slot)

    waits(slot)
    return slot


def _gather_rows_body(idx_ref, src_ref, o_ref, buf, sem, *, rows):
    def starts(step, slot):
        _row_gather_start(idx_ref, step * rows, src_ref, buf.at[slot], sem.at[slot], rows)

    def waits(slot):
        _row_gather_wait(src_ref, buf.at[slot], sem.at[slot], rows)

    slot = _prefetched_gather(starts, waits)
    o_ref[...] = buf[slot].astype(o_ref.dtype)


def _gather_rows(src, idx):
    m = idx.shape[0]
    w = src.shape[1]
    rows = GATHER_ROWS
    assert m % rows == 0 and src.shape[0] >= rows
    return pl.pallas_call(
        functools.partial(_gather_rows_body, rows=rows),
        grid_spec=pltpu.PrefetchScalarGridSpec(
            num_scalar_prefetch=1,
            grid=(m // rows,),
            in_specs=[pl.BlockSpec(memory_space=pl.ANY)],
            out_specs=pl.BlockSpec((rows, w), lambda i, idx_ref: (i, 0)),
            scratch_shapes=[pltpu.VMEM((2, rows, w), src.dtype), pltpu.SemaphoreType.DMA((2,))]),
        out_shape=jax.ShapeDtypeStruct((m, w), BF16),
        compiler_params=_cparams(("arbitrary",), rows * w * (2 * 4 + 2 * 2 + 4)),
        name="moe_gather",
    )(idx, src)


def _gmm_new_weights(te_ref):
    i = pl.program_id(1)
    return jnp.logical_or(i == 0, te_ref[i] != te_ref[jnp.maximum(i - 1, 0)])


def _gmm_swiglu_body(te_ref, nv_ref, x_ref, wg_ref, wu_ref, o_ref, wg_bf, wu_bf):
    valid = pl.program_id(1) < nv_ref[0]

    @pl.when(_gmm_new_weights(te_ref))
    def _():
        wg_bf[...] = wg_ref[...].astype(BF16)
        wu_bf[...] = wu_ref[...].astype(BF16)

    @pl.when(valid)
    def _():
        a = x_ref[...]
        g = _dot(a, wg_bf[...])
        u = _dot(a, wu_bf[...])
        o_ref[...] = (_silu(g) * u).astype(o_ref.dtype)

    @pl.when(jnp.logical_not(valid))
    def _():
        o_ref[...] = jnp.zeros_like(o_ref)


def _gmm_down_body(te_ref, nv_ref, a_ref, w_ref, o_ref, w_bf):
    valid = pl.program_id(1) < nv_ref[0]

    @pl.when(_gmm_new_weights(te_ref))
    def _():
        w_bf[...] = w_ref[...].astype(BF16)

    @pl.when(valid)
    def _():
        o_ref[...] = _dot(a_ref[...], w_bf[...])

    @pl.when(jnp.logical_not(valid))
    def _():
        o_ref[...] = jnp.zeros_like(o_ref)


def _gmm_maps():
    def rows(j, i, te, nv):
        return (jnp.minimum(i, nv[0] - 1), 0)

    def weights(j, i, te, nv):
        return (te[i], 0, j)

    def out(j, i, te, nv):
        return (i, j)

    return rows, weights, out


def _gmm_swiglu(x_sorted, wg, wu, tile_e, n_valid):
    m, k = x_sorted.shape
    n = wg.shape[2]
    tm = MOE_TILE
    tn = _pick(n, (512, 256, 128))
    rows, weights, out = _gmm_maps()
    vmem = 2 * (tm * k * 2 + 2 * k * tn * 4 + tm * tn * 2) + 2 * k * tn * 2 + 3 * tm * tn * 4
    return pl.pallas_call(
        _gmm_swiglu_body,
        grid_spec=pltpu.PrefetchScalarGridSpec(
            num_scalar_prefetch=2,
            grid=(n // tn, m // tm),
            in_specs=[pl.BlockSpec((tm, k), rows),
                      pl.BlockSpec((None, k, tn), weights),
                      pl.BlockSpec((None, k, tn), weights)],
            out_specs=pl.BlockSpec((tm, tn), out),
            scratch_shapes=[pltpu.VMEM((k, tn), BF16), pltpu.VMEM((k, tn), BF16)]),
        out_shape=jax.ShapeDtypeStruct((m, n), BF16),
        compiler_params=_cparams(("arbitrary", "arbitrary"), vmem),
        name="moe_up",
    )(tile_e, n_valid, x_sorted, wg, wu)


def _gmm_down(a_sorted, wd, tile_e, n_valid):
    m, k = a_sorted.shape
    n = wd.shape[2]
    tm = MOE_TILE
    tn = _pick(n, (1024, 512, 256, 128))
    rows, weights, out = _gmm_maps()
    vmem = 2 * (tm * k * 2 + k * tn * 4 + tm * tn * 4) + k * tn * 2 + tm * tn * 4
    return pl.pallas_call(
        _gmm_down_body,
        grid_spec=pltpu.PrefetchScalarGridSpec(
            num_scalar_prefetch=2,
            grid=(n // tn, m // tm),
            in_specs=[pl.BlockSpec((tm, k), rows),
                      pl.BlockSpec((None, k, tn), weights)],
            out_specs=pl.BlockSpec((tm, tn), out),
            scratch_shapes=[pltpu.VMEM((k, tn), BF16)]),
        out_shape=jax.ShapeDtypeStruct((m, n), F32),
        compiler_params=_cparams(("arbitrary", "arbitrary"), vmem),
        name="moe_down",
    )(tile_e, n_valid, a_sorted, wd)


def _moe_combine_body(p1_ref, p2_ref, y_ref, x_ref, gate_ref, route_ref, o_ref, buf1, buf2, sem, *, rows):
    def starts(step, slot):
        _row_gather_start(p1_ref, step * rows, y_ref, buf1.at[slot], sem.at[0, slot], rows)
        _row_gather_start(p2_ref, step * rows, y_ref, buf2.at[slot], sem.at[1, slot], rows)

    def waits(slot):
        _row_gather_wait(y_ref, buf1.at[slot], sem.at[0, slot], rows)
        _row_gather_wait(y_ref, buf2.at[slot], sem.at[1, slot], rows)

    slot = _prefetched_gather(starts, waits)
    route = route_ref[...]
    w1 = route[:, ROUTE_W1:ROUTE_W1 + 1]
    w2 = route[:, ROUTE_W2:ROUTE_W2 + 1]
    o_ref[...] = x_ref[...] + gate_ref[...] * (w1 * buf1[slot] + w2 * buf2[slot])


def _moe_combine(y_sorted, pos1, pos2, x, gate, route, seq):
    t, d = x.shape
    rows = _pick(seq, (128,))
    per_batch = seq // rows
    return pl.pallas_call(
        functools.partial(_moe_combine_body, rows=rows),
        grid_spec=pltpu.PrefetchScalarGridSpec(
            num_scalar_prefetch=2,
            grid=(t // rows,),
            in_specs=[pl.BlockSpec(memory_space=pl.ANY),
                      pl.BlockSpec((rows, d), lambda i, p1, p2: (i, 0)),
                      pl.BlockSpec((None, 1, d), lambda i, p1, p2: (i // per_batch, 0, 0)),
                      pl.BlockSpec((rows, V7X_LANES), lambda i, p1, p2: (i, 0))],
            out_specs=pl.BlockSpec((rows, d), lambda i, p1, p2: (i, 0)),
            scratch_shapes=[pltpu.VMEM((2, rows, d), F32), pltpu.VMEM((2, rows, d), F32),
                            pltpu.SemaphoreType.DMA((2, 2))]),
        out_shape=jax.ShapeDtypeStruct((t, d), F32),
        compiler_params=_cparams(("arbitrary",), 10 * rows * d * 4),
        name="moe_combine",
    )(pos1, pos2, y_sorted, x, gate, route)


_HG_LEVELS = (HG_CHUNK // HG_SUB).bit_length() - 1


def _hgrn_level_masks():
    ti = np.arange(HG_CHUNK)[:, None]
    si = np.arange(HG_CHUNK)[None, :]
    out = []
    for lvl in range(_HG_LEVELS):
        half = HG_SUB << lvl
        blk = 2 * half
        out.append((ti // blk == si // blk) & (ti % blk >= half) & (si % blk < half))
    return jnp.asarray(np.stack(out), F32)


def _hgrn_chunk(q_in, f_in, v, g_in, lb, gain, state_t, tri, ones, lvl_mask_ref):
    c = HG_CHUNK
    fg = lb + (1.0 - lb) * _sigmoid(f_in)
    log_f = jnp.log2(jnp.maximum(fg, MIN_FORGET))
    k = 1.0 - fg
    q = _silu(q_in)
    lf_hi, lf_mid, lf_lo = _split3(log_f)
    b = _dot(tri, lf_hi) + _dot(tri, lf_mid) + _dot(tri, lf_lo)
    b_last = b[c - 1:c, :]

    o = _dot_nt((q * jnp.exp2(b)).astype(BF16), state_t.astype(BF16))

    row = lax.broadcasted_iota(jnp.int32, (c, 1), 0)
    scores = jnp.zeros((c, c), F32)
    for lvl in range(_HG_LEVELS):
        half = HG_SUB << lvl
        blk = 2 * half
        bref = jnp.concatenate(
            [jnp.broadcast_to(b[p * blk + half - 1:p * blk + half, :], (blk, HG_DK)) for p in range(c // blk)],
            axis=0)
        is_q = (row & half) != 0
        e = jnp.exp2(-jnp.abs(b - bref))
        xk = jnp.where(is_q, q, k) * e
        qd = jnp.where(is_q, xk, 0.0).astype(BF16)
        kd = jnp.where(is_q, 0.0, xk).astype(BF16)
        scores = scores + _dot_nt(qd, kd) * lvl_mask_ref[lvl]
    o = o + _dot(scores.astype(BF16), v.astype(BF16))

    nb = c // HG_SUB
    b3 = b.reshape(nb, HG_SUB, HG_DK)
    q3 = q.reshape(nb, HG_SUB, HG_DK)
    k3 = k.reshape(nb, HG_SUB, HG_DK)
    v3 = v.reshape(nb, HG_SUB, HG_DK)
    t_in = lax.broadcasted_iota(jnp.int32, (nb, HG_SUB, HG_DK), 1)
    for s in range(HG_SUB):
        diff = b3 - b3[:, s:s + 1, :]
        dec = jnp.exp2(diff if s == 0 else jnp.where(t_in >= s, diff, NEG_BIG))
        m = (q3 * (k3[:, s:s + 1, :] * dec)).reshape(c, HG_DK)
        r = _dot(m.astype(BF16), ones)
        o = o + r * jnp.broadcast_to(v3[:, s:s + 1, :], (nb, HG_SUB, HG_DK)).reshape(c, HG_DK)

    kdec = (k * jnp.exp2(b_last - b)).astype(BF16)
    new_state_t = state_t * jnp.exp2(b_last) + _dot_tn(v.astype(BF16), kdec)

    ms = jnp.mean(o * o, axis=-1, keepdims=True)
    out = o * lax.rsqrt(ms + NORM_EPS) * gain * _silu(g_in)
    return out, new_state_t


def _hgrn_body(q_ref, f_ref, i_ref, g_ref, lbraw_ref, gain_ref, lvl_mask_ref, o_ref, state_ref,
               *, layer, n_chunks, group):
    @pl.when(pl.program_id(2) == 0)
    def _():
        state_ref[...] = jnp.zeros_like(state_ref)

    lbr = lbraw_ref[...]
    ex = jnp.exp(lbr - jnp.max(lbr, axis=0, keepdims=True))
    soft = ex / jnp.sum(ex, axis=0, keepdims=True)
    lb = jnp.zeros((1, group * HG_DK), F32)
    for j in range(1, layer + 1):
        lb = lb + soft[j:j + 1, :]
    gain = gain_ref[...]
    c = HG_CHUNK
    tri = (lax.broadcasted_iota(jnp.int32, (c, c), 0) >= lax.broadcasted_iota(jnp.int32, (c, c), 1)).astype(BF16)
    ones = jnp.ones((HG_DK, HG_DK), BF16)

    def chunk(ci, carry):
        sl = pl.ds(pl.multiple_of(ci * c, c), c)
        for hh in range(group):
            cs = slice(hh * HG_DK, (hh + 1) * HG_DK)
            out, new_state = _hgrn_chunk(q_ref[sl, cs].astype(F32), f_ref[sl, cs].astype(F32),
                                         i_ref[sl, cs].astype(F32), g_ref[sl, cs].astype(F32),
                                         lb[:, cs], gain, state_ref[hh], tri, ones, lvl_mask_ref)
            o_ref[sl, cs] = out.astype(o_ref.dtype)
            state_ref[hh] = new_state
        return carry

    lax.fori_loop(0, n_chunks, chunk, 0)


def _hgrn(proj, lb_raw, out_gain, layer, n_batch, seq, width):
    heads = width // HG_DK
    group = HG_GROUP if heads % HG_GROUP == 0 else 1
    hgroups = heads // group
    gw = group * HG_DK
    lc = _pick(seq, (512, 256, 128))
    per_batch = seq // lc
    depth = lb_raw.shape[0]

    def col(off):
        return pl.BlockSpec((lc, gw), lambda b, h, i: (b * per_batch + i, off * hgroups + h))

    return pl.pallas_call(
        functools.partial(_hgrn_body, layer=layer, n_chunks=lc // HG_CHUNK, group=group),
        grid=(n_batch, hgroups, per_batch),
        in_specs=[col(0), col(1), col(2), col(3),
                  pl.BlockSpec((depth, gw), lambda b, h, i: (0, h)),
                  pl.BlockSpec((1, HG_DK), lambda b, h, i: (0, 0)),
                  pl.BlockSpec((_HG_LEVELS, HG_CHUNK, HG_CHUNK), lambda b, h, i: (0, 0, 0))],
        out_specs=pl.BlockSpec((lc, gw), lambda b, h, i: (b * per_batch + i, h)),
        out_shape=jax.ShapeDtypeStruct((n_batch * seq, width), BF16),
        scratch_shapes=[pltpu.VMEM((group, HG_DK, HG_DK), F32)],
        compiler_params=_cparams(("parallel", "parallel", "arbitrary"), 16 * 1024 * 1024),
        name="hgrn2",
    )(proj, proj, proj, proj, lb_raw, out_gain.reshape(1, HG_DK), _hgrn_level_masks())


def _pool_body(u_ref, halo_ref, w_ref, scale_ref, o_ref, ext_ref, *, ts, gdim):
    i = pl.program_id(1)
    halo = halo_ref[...].astype(F32)
    ext_ref[0:POOL_HALO, :] = jnp.where(i > 0, halo, 0.0)
    ext_ref[POOL_HALO:, :] = u_ref[...].astype(F32)
    pos = (i * ts + lax.broadcasted_iota(jnp.int32, (ts, 1), 0) + 1).astype(F32)
    for g, win in enumerate(POOL_WINDOWS):
        cs = slice(g * gdim, (g + 1) * gdim)
        acc = ext_ref[POOL_HALO:, cs]
        for j in range(1, win):
            acc = acc + ext_ref[POOL_HALO - j:POOL_HALO - j + ts, cs]
        pooled = acc / jnp.minimum(pos, float(win)) - ext_ref[POOL_HALO:, cs]
        y = _dot(pooled.astype(BF16), w_ref[g])
        o_ref[:, cs] = (y * scale_ref[:, cs]).astype(o_ref.dtype)


def _pool(proj, col_off, pool_w, pool_scale, n_batch, seq, width):
    groups = len(POOL_WINDOWS)
    gdim = width // groups
    assert gdim % V7X_LANES == 0 and col_off % width == 0 and max(POOL_WINDOWS) <= POOL_HALO
    ts = _pick(seq, (512, 256, 128))
    per_batch = seq // ts
    cb = col_off // width
    hb = ts // POOL_HALO
    return pl.pallas_call(
        functools.partial(_pool_body, ts=ts, gdim=gdim),
        grid=(n_batch, per_batch),
        in_specs=[pl.BlockSpec((ts, width), lambda b, i: (b * per_batch + i, cb)),
                  pl.BlockSpec((POOL_HALO, width),
                               lambda b, i: (jnp.maximum((b * per_batch + i) * hb - 1, 0), cb)),
                  pl.BlockSpec((groups, gdim, gdim), lambda b, i: (0, 0, 0)),
                  pl.BlockSpec((1, width), lambda b, i: (0, 0))],
        out_specs=pl.BlockSpec((ts, width), lambda b, i: (b * per_batch + i, 0)),
        out_shape=jax.ShapeDtypeStruct((n_batch * seq, width), BF16),
        scratch_shapes=[pltpu.VMEM((ts + POOL_HALO, width), F32)],
        compiler_params=_cparams(("parallel", "parallel"), 16 * 1024 * 1024),
        name="pool",
    )(proj, proj, pool_w, pool_scale.reshape(1, width))


def _rope_table_body(pos_ref, cos_ref, sin_ref):
    pos = pos_ref[...].astype(F32)
    lane = lax.broadcasted_iota(jnp.int32, (1, V7X_LANES), 1)
    j = lane % MLA_ROPE
    fidx = (j % (MLA_ROPE // 2)).astype(F32)
    inv_freq = jnp.exp(fidx * (-2.0 / MLA_ROPE * math.log(ROPE_THETA)))
    ang = pos * inv_freq
    cos_ref[...] = jnp.cos(ang)
    sin_ref[...] = jnp.where(j < MLA_ROPE // 2, -1.0, 1.0) * jnp.sin(ang)


def _rope_table(positions):
    t = positions.size
    ts = _pick(t, (512, 256, 128))
    return pl.pallas_call(
        _rope_table_body,
        grid=(t // ts,),
        in_specs=[pl.BlockSpec((ts, 1), lambda i: (i, 0))],
        out_specs=[pl.BlockSpec((ts, V7X_LANES), lambda i: (i, 0))] * 2,
        out_shape=[jax.ShapeDtypeStruct((t, V7X_LANES), F32)] * 2,
        compiler_params=_cparams(("parallel",), 4 * 1024 * 1024),
        name="rope_table",
    )(positions.reshape(t, 1))


def _rms(x, gain):
    return x * lax.rsqrt(jnp.mean(x * x, axis=-1, keepdims=True) + NORM_EPS) * gain


def _prep_q_body(cq_ref, qn_ref, w_ref, gn_ref, gr_ref, grs_ref, seg_ref, cos_ref, sin_ref, o_ref,
                 *, heads, sm_scale):
    hn = heads * MLA_NOPE
    hr = heads * MLA_ROPE
    hq = _rms(cq_ref[...].astype(F32), qn_ref[...]).astype(BF16)
    y = _dot(hq, w_ref[...])
    yr = y[:, hn:hn + hr]
    ys = y[:, hn + hr:]
    sq_hi, sq_lo = _split2(yr * yr)
    seg = seg_ref[...]
    ss = _dot(sq_hi, seg) + _dot(sq_lo, seg)
    inv = lax.rsqrt(ss * (1.0 / MLA_ROPE) + NORM_EPS)
    reps = hr // V7X_LANES
    cosf = jnp.concatenate([cos_ref[...]] * reps, axis=1)
    sinf = jnp.concatenate([sin_ref[...]] * reps, axis=1)
    qr = (yr * inv * gr_ref[...]) * cosf + (ys * inv * grs_ref[...]) * sinf
    gn = gn_ref[...] * sm_scale
    for h in range(heads):
        qn = _rms(y[:, h * MLA_NOPE:(h + 1) * MLA_NOPE], gn)
        o_ref[h, :, 0:MLA_NOPE] = qn.astype(o_ref.dtype)
        o_ref[h, :, MLA_NOPE:] = (qr[:, h * MLA_ROPE:(h + 1) * MLA_ROPE] * sm_scale).astype(o_ref.dtype)


def _prep_kv_body(ckv_ref, kpe_ref, kvn_ref, w_ref, gn_ref, gr2_ref, cos_ref, sin_ref, k_ref, v_ref, *, heads):
    hk = _rms(ckv_ref[...].astype(F32), kvn_ref[...]).astype(BF16)
    y = _dot(hk, w_ref[...])
    kp = kpe_ref[...].astype(F32)
    kpe = kp[:, :MLA_ROPE]
    inv = lax.rsqrt(jnp.mean(kpe * kpe, axis=-1, keepdims=True) + NORM_EPS)
    kn = kp * inv * gr2_ref[...]
    kr = (kn[:, :MLA_ROPE] * cos_ref[:, :MLA_ROPE] + kn[:, MLA_ROPE:] * sin_ref[:, :MLA_ROPE]).astype(k_ref.dtype)
    per = MLA_NOPE + MLA_V
    for h in range(heads):
        k_ref[h, :, 0:MLA_NOPE] = _rms(y[:, h * per:h * per + MLA_NOPE], gn_ref[...]).astype(k_ref.dtype)
        k_ref[h, :, MLA_NOPE:] = kr
        v_ref[h, 0, :MLA_V, :] = y[:, h * per + MLA_NOPE:(h + 1) * per].T.astype(v_ref.dtype)
        v_ref[h, 0, MLA_V:, :] = jnp.ones((V7X_BF16_ROWS, y.shape[0]), v_ref.dtype)


def _flash_body(q_ref, k_ref, vt_ref, o_ref, acc_ref, st_a, st_b, *, tile, group):
    qi = pl.program_id(2)
    acc_ref[...] = jnp.zeros_like(acc_ref)

    def scores(j, dst):
        ks = pl.ds(pl.multiple_of(j * tile, tile), tile)
        for g in range(group):
            dst[g] = _dot_nt(k_ref[g, ks, :], q_ref[g])

    def consume(j, src, ms, masked):
        m_new = []
        for g in range(group):
            st = src[g]
            if masked:
                kidx = lax.broadcasted_iota(jnp.int32, (tile, tile), 0)
                qidx = lax.broadcasted_iota(jnp.int32, (tile, tile), 1)
                st = jnp.where(kidx <= qidx, st, NEG_BIG)
            m = jnp.maximum(ms[g], jnp.max(st, axis=0, keepdims=True))
            p = jnp.exp2(st - m).astype(BF16)
            acc_ref[g] = jnp.exp2(ms[g] - m) * acc_ref[g] + _dot(vt_ref[g, j], p)
            m_new.append(m)
        return tuple(m_new)

    def finish():
        for g in range(group):
            o = acc_ref[g, :MLA_V, :] / acc_ref[g, MLA_V:MLA_V + 1, :]
            o_ref[:, g * MLA_V:(g + 1) * MLA_V] = o.T.astype(o_ref.dtype)

    scores(0, st_a)

    def pair(pi, ms):
        j = 2 * pi
        scores(j + 1, st_b)
        ms = consume(j, st_a, ms, False)
        scores(j + 2, st_a)
        return consume(j + 1, st_b, ms, False)

    m0 = tuple(jnp.full((1, tile), NEG_BIG, F32) for _ in range(group))
    ms = lax.fori_loop(0, qi // 2, pair, m0)
    odd = qi % 2 == 1

    @pl.when(odd)
    def _():
        scores(qi, st_b)
        consume(qi, st_b, consume(qi - 1, st_a, ms, False), True)
        finish()

    @pl.when(jnp.logical_not(odd))
    def _():
        consume(qi, st_a, ms, True)
        finish()


def _mla(proj_c, cos_t, sin_t, q_norm, w_q, kv_norm, w_kv, g_qn, g_qr, g_kn, g_kr,
         n_batch, seq, heads, q_rank, kv_rank):
    t = n_batch * seq
    half = MLA_ROPE // 2
    hn, hr = heads * MLA_NOPE, heads * MLA_ROPE
    dq = MLA_NOPE + MLA_ROPE
    sm_scale = float(dq) ** -0.5 * math.log2(math.e)
    assert q_rank % kv_rank == 0 and (q_rank + kv_rank) % V7X_LANES == 0 and hr % V7X_LANES == 0
    tile = ATTN_TILE
    assert seq % tile == 0
    tm = tile
    vrows = MLA_V + V7X_BF16_ROWS

    def swap(g):
        return jnp.concatenate([g[half:], g[:half]])

    gr = jnp.tile(g_qr, heads).reshape(1, hr)
    grs = jnp.tile(swap(g_qr), heads).reshape(1, hr)
    lane = jnp.arange(hr) // MLA_ROPE
    seg = (lane[:, None] == lane[None, :]).astype(BF16)
    row = lambda i: (i, 0)
    const2 = lambda i: (0, 0)
    qcat = pl.pallas_call(
        functools.partial(_prep_q_body, heads=heads, sm_scale=sm_scale),
        grid=(t // tm,),
        in_specs=[pl.BlockSpec((tm, q_rank), row),
                  pl.BlockSpec((1, q_rank), const2),
                  pl.BlockSpec((q_rank, hn + 2 * hr), const2),
                  pl.BlockSpec((1, MLA_NOPE), const2),
                  pl.BlockSpec((1, hr), const2),
                  pl.BlockSpec((1, hr), const2),
                  pl.BlockSpec((hr, hr), const2),
                  pl.BlockSpec((tm, V7X_LANES), row),
                  pl.BlockSpec((tm, V7X_LANES), row)],
        out_specs=pl.BlockSpec((heads, tm, dq), lambda i: (0, i, 0)),
        out_shape=jax.ShapeDtypeStruct((heads, t, dq), BF16),
        compiler_params=_cparams(("parallel",), 40 * 1024 * 1024),
        name="mla_prep_q",
    )(proj_c, q_norm.reshape(1, q_rank), w_q, g_qn.reshape(1, MLA_NOPE), gr, grs, seg, cos_t, sin_t)

    gr2 = jnp.concatenate([g_kr, swap(g_kr)]).reshape(1, 2 * MLA_ROPE)
    kcat, vt = pl.pallas_call(
        functools.partial(_prep_kv_body, heads=heads),
        grid=(t // tm,),
        in_specs=[pl.BlockSpec((tm, kv_rank), lambda i: (i, q_rank // kv_rank)),
                  pl.BlockSpec((tm, V7X_LANES), lambda i: (i, (q_rank + kv_rank) // V7X_LANES)),
                  pl.BlockSpec((1, kv_rank), const2),
                  pl.BlockSpec((kv_rank, heads * (MLA_NOPE + MLA_V)), const2),
                  pl.BlockSpec((1, MLA_NOPE), const2),
                  pl.BlockSpec((1, 2 * MLA_ROPE), const2),
                  pl.BlockSpec((tm, V7X_LANES), row),
                  pl.BlockSpec((tm, V7X_LANES), row)],
        out_specs=[pl.BlockSpec((heads, tm, dq), lambda i: (0, i, 0)),
                   pl.BlockSpec((heads, 1, vrows, tile), lambda i: (0, i, 0, 0))],
        out_shape=[jax.ShapeDtypeStruct((heads, t, dq), BF16),
                   jax.ShapeDtypeStruct((heads, t // tile, vrows, tile), BF16)],
        compiler_params=_cparams(("parallel",), 40 * 1024 * 1024),
        name="mla_prep_kv",
    )(proj_c, proj_c, kv_norm.reshape(1, kv_rank), w_kv, g_kn.reshape(1, MLA_NOPE), gr2, cos_t, sin_t)

    nq = seq // tile
    group = ATTN_GROUP if heads % ATTN_GROUP == 0 else 1
    return pl.pallas_call(
        functools.partial(_flash_body, tile=tile, group=group),
        grid=(heads // group, n_batch, nq),
        in_specs=[pl.BlockSpec((group, tile, dq), lambda h, b, i: (h, b * nq + i, 0)),
                  pl.BlockSpec((group, seq, dq), lambda h, b, i: (h, b, 0)),
                  pl.BlockSpec((group, nq, vrows, tile), lambda h, b, i: (h, b, 0, 0))],
        out_specs=pl.BlockSpec((tile, group * MLA_V), lambda h, b, i: (b * nq + i, h)),
        out_shape=jax.ShapeDtypeStruct((t, heads * MLA_V), BF16),
        scratch_shapes=[pltpu.VMEM((group, vrows, tile), F32),
                        pltpu.VMEM((group, tile, tile), F32), pltpu.VMEM((group, tile, tile), F32)],
        compiler_params=_cparams(("parallel", "parallel", "arbitrary"), 48 * 1024 * 1024),
        name="mla_flash",
    )(qcat, kcat, vt)


def kernel(x, c, positions, ada_w, ada_b, ada_layer, mix_norm, ffn_norm, w_in, hgrn_lower_bounds,
           hgrn_out_norm, pool_w, pool_scale, mla_q_norm, mla_w_uq, mla_kv_norm, mla_w_ukv,
           mla_qk_norm_q_nope, mla_qk_norm_q_rope, mla_qk_norm_k_nope, mla_qk_norm_k_rope,
           w_branch_a, w_branch_b, w_branch_c, w_o, ffn_w_gate, ffn_w_up, ffn_w_down,
           moe_router, moe_w_gate, moe_w_up, moe_w_down):
    n_batch, seq, d = x.shape
    depth = w_in.shape[0]
    t = n_batch * seq
    hg_width = hgrn_lower_bounds.shape[1]
    pool_width = pool_scale.shape[1]
    q_rank = mla_q_norm.shape[1]
    kv_rank = mla_kv_norm.shape[1]
    heads = mla_w_ukv.shape[2] // (MLA_NOPE + MLA_V)
    half = MLA_ROPE // 2
    n_a = 4 * hg_width + pool_width
    n_c = q_rank + kv_rank + MLA_ROPE
    assert w_in.shape[2] == n_a + n_c + 3 * d

    mod = _ada(c, ada_w, ada_b, ada_layer)
    cos_t, sin_t = _rope_table(positions)
    xf = x.reshape(t, d)

    for l in range(depth):
        mod_l = mod[l]
        gate1 = mod_l[:, 2:3, :]
        gate2 = mod_l[:, 5:6, :]

        w_l = w_in[l]
        w_a = w_l[:, :n_a].astype(BF16)
        kpe_w = w_l[:, n_a + q_rank + kv_rank:n_a + n_c]
        w_c = jnp.concatenate([w_l[:, n_a:n_a + n_c], kpe_w[:, half:], kpe_w[:, :half]], axis=1).astype(BF16)
        w_g = w_l[:, n_a + n_c:].astype(BF16)
        wq = mla_w_uq[l].reshape(q_rank, heads, MLA_NOPE + MLA_ROPE)
        wq_r = wq[:, :, MLA_NOPE:]
        w_q = jnp.concatenate([
            wq[:, :, :MLA_NOPE].reshape(q_rank, heads * MLA_NOPE),
            wq_r.reshape(q_rank, heads * MLA_ROPE),
            jnp.concatenate([wq_r[:, :, half:], wq_r[:, :, :half]], axis=2).reshape(q_rank, heads * MLA_ROPE),
        ], axis=1).astype(BF16)

        h = _modulate(xf.reshape(n_batch, seq, d), mix_norm[l], mod_l, 0, 1)
        proj_a = _mm_cast(h, w_a, name="proj_a")
        proj_c = _mm_cast(h, w_c, name="proj_c")
        gates = _mm_cast(h, w_g, name="proj_gates")

        o_a = _hgrn(proj_a, hgrn_lower_bounds, hgrn_out_norm[l], l, n_batch, seq, hg_width)
        o_b = _pool(proj_a, 4 * hg_width, pool_w[l].astype(BF16), pool_scale[l], n_batch, seq, pool_width)
        o_c = _mla(proj_c, cos_t, sin_t, mla_q_norm[l], w_q, mla_kv_norm[l], mla_w_ukv[l].astype(BF16),
                   mla_qk_norm_q_nope[l], mla_qk_norm_q_rope[l], mla_qk_norm_k_nope[l], mla_qk_norm_k_rope[l],
                   n_batch, seq, heads, q_rank, kv_rank)
        merged = _merge(o_a, o_b, o_c, w_branch_a[l].astype(BF16), w_branch_b[l].astype(BF16),
                        w_branch_c[l].astype(BF16), gates)
        xf = _mm_residual(merged, w_o[l].astype(BF16), xf, gate1, seq, name="out_proj")

        j = l // 2
        if l % 2 == 0:
            h = _modulate(xf.reshape(n_batch, seq, d), ffn_norm[l], mod_l, 3, 4)
            act = _mm_swiglu(h, ffn_w_gate[j].astype(BF16), ffn_w_up[j].astype(BF16), name="ffn_up")
            xf = _mm_residual(act, ffn_w_down[j].astype(BF16), xf, gate2, seq, name="ffn_down")
        else:
            n_experts = moe_w_gate.shape[1]
            h, route = _modulate(xf.reshape(n_batch, seq, d), ffn_norm[l], mod_l, 3, 4, router=moe_router[j])
            row_token, tile_e, n_valid, pos1, pos2 = _route_metadata(route, n_experts)
            h_sorted = _gather_rows(h, row_token)
            act = _gmm_swiglu(h_sorted, moe_w_gate[j], moe_w_up[j], tile_e, n_valid)
            y_sorted = _gmm_down(act, moe_w_down[j], tile_e, n_valid)
            xf = _moe_combine(y_sorted, pos1, pos2, xf, gate2, route, seq)
    return xf.reshape(n_batch, seq, d)
```

```python
import functools
import math

import numpy as np
import jax
import jax.numpy as jnp
from jax import lax
from jax.experimental import pallas as pl
from jax.experimental.pallas import tpu as pltpu

F32 = jnp.float32
BF16 = jnp.bfloat16

HG_DK = 128
POOL_WINDOWS = (2, 4, 8, 16)
MLA_NOPE = 128
MLA_ROPE = 64
MLA_V = 128
ROPE_THETA = 10000.0
MIN_FORGET = 1e-30
NORM_EPS = 1e-6
N_MOD = 6
TOP_K = 2
NEG_BIG = -1e30

V7X_LANES = 128
V7X_SUBLANES = 8
V7X_BF16_ROWS = 16
V7X_VMEM_BYTES = 64 * 1024 * 1024
VMEM_CAP = V7X_VMEM_BYTES - 8 * 1024 * 1024

HG_CHUNK = 128
HG_SUB = 8
HG_GROUP = 8
POOL_HALO = 16
ADA_CHUNK = 512
ATTN_TILE = 1024
ATTN_GROUP = 2
ROUTE_E1, ROUTE_E2, ROUTE_W1, ROUTE_W2 = 0, 1, 2, 3
MOE_TILE = 512
GATHER_ROWS = 512


def _pick(n, prefs):
    for p in prefs:
        if n % p == 0:
            return p
    raise ValueError(f"no tile in {prefs} divides {n}")


def _cparams(sem, vmem_bytes):
    limit = int(min(VMEM_CAP, max(32 * 1024 * 1024, vmem_bytes * 5 // 4)))
    return pltpu.CompilerParams(dimension_semantics=sem, vmem_limit_bytes=limit)


def _sigmoid(x):
    return 1.0 / (1.0 + jnp.exp(-x))


def _silu(x):
    return x * _sigmoid(x)


def _dot(a, b):
    return jnp.dot(a, b, preferred_element_type=F32)


def _dot_nt(a, b):
    return lax.dot_general(a, b, (((1,), (1,)), ((), ())), preferred_element_type=F32)


def _dot_tn(a, b):
    return lax.dot_general(a, b, (((0,), (0,)), ((), ())), preferred_element_type=F32)


def _split3(x):
    hi = x.astype(BF16)
    r1 = x - hi.astype(F32)
    mid = r1.astype(BF16)
    lo = (r1 - mid.astype(F32)).astype(BF16)
    return hi, mid, lo


def _split2(x):
    hi = x.astype(BF16)
    lo = (x - hi.astype(F32)).astype(BF16)
    return hi, lo


def _ada_body(ct_ref, w_ref, b_ref, lay_ref, o_ref, acc_ref, *, n_batch, depth):
    kk = pl.program_id(1)
    tk, tn = w_ref.shape
    sub = V7X_SUBLANES

    @pl.when(kk == 0)
    def _():
        acc_ref[...] = jnp.zeros_like(acc_ref)

    s = _silu(ct_ref[pl.ds(pl.multiple_of(kk * tk, tk), tk), :])
    for c0 in range(0, tn, ADA_CHUNK):
        w = w_ref[:, c0:c0 + ADA_CHUNK]
        for b in range(n_batch):
            prod = (w * s[:, b:b + 1]).reshape(tk // sub, sub, ADA_CHUNK)
            acc_ref[b, :, c0:c0 + ADA_CHUNK] += jnp.sum(prod, axis=0)

    @pl.when(kk == pl.num_programs(1) - 1)
    def _():
        for b in range(n_batch):
            r = jnp.sum(acc_ref[b], axis=0, keepdims=True) + b_ref[...]
            for l in range(depth):
                o_ref[l, b:b + 1, :] = r + lay_ref[l:l + 1, :]


def _ada(c, ada_w, ada_b, ada_layer):
    n_batch, d = c.shape
    depth = ada_layer.shape[0]
    n = ada_w.shape[1]
    tn = n
    tk = _pick(d, (64, 32, 16, 8))
    assert tn % ADA_CHUNK == 0
    ct = c.T
    lay = ada_layer.reshape(depth, n)
    out = pl.pallas_call(
        functools.partial(_ada_body, n_batch=n_batch, depth=depth),
        scratch_shapes=[pltpu.VMEM((n_batch, V7X_SUBLANES, tn), F32)],
        grid=(n // tn, d // tk),
        in_specs=[
            pl.BlockSpec((d, n_batch), lambda j, k: (0, 0)),
            pl.BlockSpec((tk, tn), lambda j, k: (k, j)),
            pl.BlockSpec((1, tn), lambda j, k: (0, j)),
            pl.BlockSpec((depth, tn), lambda j, k: (0, j)),
        ],
        out_specs=pl.BlockSpec((depth, n_batch, tn), lambda j, k: (0, 0, j)),
        out_shape=jax.ShapeDtypeStruct((depth, n_batch, n), F32),
        compiler_params=_cparams(("parallel", "arbitrary"), 4 * tk * tn * 4),
        name="ada",
    )(ct, ada_w, ada_b.reshape(1, n), lay)
    return out.reshape(depth, n_batch, N_MOD, d)


def _modulated(x_ref, g_ref, mod_ref, shift_idx, scale_idx):
    x = x_ref[...]
    ms = jnp.mean(x * x, axis=-1, keepdims=True)
    y = x * lax.rsqrt(ms + NORM_EPS) * g_ref[...]
    return y * (1.0 + mod_ref[scale_idx:scale_idx + 1, :]) + mod_ref[shift_idx:shift_idx + 1, :]


def _modulate_body(x_ref, g_ref, mod_ref, o_ref, *, shift_idx, scale_idx):
    o_ref[...] = _modulated(x_ref, g_ref, mod_ref, shift_idx, scale_idx).astype(o_ref.dtype)


def _modulate_route_body(x_ref, g_ref, mod_ref, r_ref, o_ref, route_ref, *, shift_idx, scale_idx, n_experts):
    h = _modulated(x_ref, g_ref, mod_ref, shift_idx, scale_idx)
    o_ref[...] = h
    h_hi, h_mid, h_lo = _split3(h)
    r = r_ref[...]
    r_hi, r_mid, r_lo = _split3(r)
    logits = (_dot(h_hi, r_hi) + _dot(h_hi, r_mid) + _dot(h_mid, r_hi)
              + _dot(h_hi, r_lo) + _dot(h_mid, r_mid) + _dot(h_lo, r_hi))
    lane = lax.broadcasted_iota(jnp.int32, logits.shape, 1).astype(F32)
    lg = jnp.where(lane < n_experts, logits, -jnp.inf)
    m1 = jnp.max(lg, axis=-1, keepdims=True)
    i1 = jnp.min(jnp.where(lg == m1, lane, float(V7X_LANES)), axis=-1, keepdims=True)
    lg2 = jnp.where(lane == i1, -jnp.inf, lg)
    m2 = jnp.max(lg2, axis=-1, keepdims=True)
    i2 = jnp.min(jnp.where(lg2 == m2, lane, float(V7X_LANES)), axis=-1, keepdims=True)
    e2 = jnp.exp(m2 - m1)
    w1 = 1.0 / (1.0 + e2)
    w2 = e2 / (1.0 + e2)
    route_ref[...] = (jnp.where(lane == ROUTE_E1, i1, 0.0) + jnp.where(lane == ROUTE_E2, i2, 0.0)
                      + jnp.where(lane == ROUTE_W1, w1, 0.0) + jnp.where(lane == ROUTE_W2, w2, 0.0))


def _modulate(x3, gain, mod_l, shift_idx, scale_idx, router=None):
    n_batch, seq, d = x3.shape
    ts = _pick(seq, (512, 256, 128))
    grid = (n_batch, seq // ts)
    x_spec = pl.BlockSpec((None, ts, d), lambda b, i: (b, i, 0))
    g_spec = pl.BlockSpec((1, d), lambda b, i: (0, 0))
    mod_spec = pl.BlockSpec((None, N_MOD, d), lambda b, i: (b, 0, 0))
    h_spec = pl.BlockSpec((None, ts, d), lambda b, i: (b, i, 0))
    vmem = 2 * ts * d * (4 + 2) + 4 * ts * d * 4
    if router is None:
        h = pl.pallas_call(
            functools.partial(_modulate_body, shift_idx=shift_idx, scale_idx=scale_idx),
            grid=grid,
            in_specs=[x_spec, g_spec, mod_spec],
            out_specs=h_spec,
            out_shape=jax.ShapeDtypeStruct((n_batch, seq, d), BF16),
            compiler_params=_cparams(("parallel", "parallel"), vmem),
            name="modulate",
        )(x3, gain.reshape(1, d), mod_l)
        return h.reshape(n_batch * seq, d)
    n_experts = router.shape[1]
    assert n_experts <= V7X_LANES
    r_pad = jnp.zeros((d, V7X_LANES), F32).at[:, :n_experts].set(router)
    h, route = pl.pallas_call(
        functools.partial(_modulate_route_body, shift_idx=shift_idx, scale_idx=scale_idx, n_experts=n_experts),
        grid=grid,
        in_specs=[x_spec, g_spec, mod_spec, pl.BlockSpec((d, V7X_LANES), lambda b, i: (0, 0))],
        out_specs=[h_spec, pl.BlockSpec((None, ts, V7X_LANES), lambda b, i: (b, i, 0))],
        out_shape=[jax.ShapeDtypeStruct((n_batch, seq, d), F32),
                   jax.ShapeDtypeStruct((n_batch, seq, V7X_LANES), F32)],
        compiler_params=_cparams(("parallel", "parallel"), vmem + 2 * ts * d * 2 + 6 * ts * d * 2),
        name="modulate_route",
    )(x3, gain.reshape(1, d), mod_l, r_pad)
    return h.reshape(n_batch * seq, d), route.reshape(n_batch * seq, V7X_LANES)


def _mm_cast_body(a_ref, w_ref, o_ref):
    o_ref[...] = _dot(a_ref[...], w_ref[...]).astype(o_ref.dtype)


def _mm_cast(a, w, out_dtype=BF16, name="mm"):
    m, k = a.shape
    n = w.shape[1]
    tn = n if n <= 2048 else _pick(n, (1024, 512, 256, 128))
    need = lambda tm_: 2 * (tm_ * k * 2 + k * tn * 2 + tm_ * tn * 2) + tm_ * tn * 4
    tm = next(t_ for t_ in (1024, 512, 256, 128) if m % t_ == 0 and need(t_) * 5 // 4 <= VMEM_CAP)
    vmem = need(tm)
    return pl.pallas_call(
        _mm_cast_body,
        grid=(m // tm, n // tn),
        in_specs=[pl.BlockSpec((tm, k), lambda i, j: (i, 0)),
                  pl.BlockSpec((k, tn), lambda i, j: (0, j))],
        out_specs=pl.BlockSpec((tm, tn), lambda i, j: (i, j)),
        out_shape=jax.ShapeDtypeStruct((m, n), out_dtype),
        compiler_params=_cparams(("parallel", "arbitrary"), vmem),
        name=name,
    )(a, w)


def _mm_swiglu_body(a_ref, wg_ref, wu_ref, o_ref):
    a = a_ref[...]
    g = _dot(a, wg_ref[...])
    u = _dot(a, wu_ref[...])
    o_ref[...] = (_silu(g) * u).astype(o_ref.dtype)


def _mm_swiglu(a, wg, wu, name="swiglu"):
    m, k = a.shape
    n = wg.shape[1]
    tn = _pick(n, (512, 256, 128))
    need = lambda tm_: 2 * (tm_ * k * 2 + 2 * k * tn * 2 + tm_ * tn * 2) + 3 * tm_ * tn * 4
    tm = next(t_ for t_ in (2048, 1024, 512, 256, 128) if m % t_ == 0 and need(t_) * 10 // 9 <= VMEM_CAP)
    vmem = need(tm)
    return pl.pallas_call(
        _mm_swiglu_body,
        grid=(m // tm, n // tn),
        in_specs=[pl.BlockSpec((tm, k), lambda i, j: (i, 0)),
                  pl.BlockSpec((k, tn), lambda i, j: (0, j)),
                  pl.BlockSpec((k, tn), lambda i, j: (0, j))],
        out_specs=pl.BlockSpec((tm, tn), lambda i, j: (i, j)),
        out_shape=jax.ShapeDtypeStruct((m, n), BF16),
        compiler_params=_cparams(("parallel", "arbitrary"), vmem),
        name=name,
    )(a, wg, wu)


def _mm_residual_body(a_ref, w_ref, x_ref, gate_ref, o_ref, *, nk):
    scale = gate_ref[...]
    part = _dot(a_ref[...], w_ref[...])
    if nk == 1:
        o_ref[...] = x_ref[...] + scale * part
    else:
        kk = pl.program_id(2)

        @pl.when(kk == 0)
        def _():
            o_ref[...] = part

        @pl.when(jnp.logical_and(kk > 0, kk < nk - 1))
        def _():
            o_ref[...] += part

        @pl.when(kk == nk - 1)
        def _():
            o_ref[...] = x_ref[...] + scale * (o_ref[...] + part)


def _mm_residual(a, w, x, gate, seq, name="mm_res"):
    m, k = a.shape
    n = w.shape[1]
    tm = _pick(seq, (1024, 512, 256, 128))
    tn = _pick(n, (1024, 512, 256, 128))
    if k <= 4096:
        tk = k
    else:
        tk = next(t for t in range(4096 // V7X_LANES * V7X_LANES, 0, -V7X_LANES) if k % t == 0)
        if tk < 512:
            tk = next(t for t in range(k // 2 // V7X_LANES * V7X_LANES, 0, -V7X_LANES) if k % t == 0)
    nk = k // tk
    if tk > 4096:
        tn = _pick(n, (512, 256, 128))
    per_batch = seq // tm
    in_specs = [pl.BlockSpec((tm, tk), lambda i, j, kk: (i, kk)),
                pl.BlockSpec((tk, tn), lambda i, j, kk: (kk, j)),
                pl.BlockSpec((tm, tn), lambda i, j, kk: (i, j)),
                pl.BlockSpec((None, 1, tn), lambda i, j, kk: (i // per_batch, 0, j))]
    vmem = 2 * (tm * tk * 2 + tk * tn * 2 + 2 * tm * tn * 4) + 2 * tm * tn * 4
    return pl.pallas_call(
        functools.partial(_mm_residual_body, nk=nk),
        grid=(m // tm, n // tn, nk),
        in_specs=in_specs,
        out_specs=pl.BlockSpec((tm, tn), lambda i, j, kk: (i, j)),
        out_shape=jax.ShapeDtypeStruct((m, n), F32),
        compiler_params=_cparams(("parallel", "parallel", "arbitrary"), vmem),
        name=name,
    )(a, w, x, gate)


def _merge_body(a_ref, b_ref, c_ref, wa_ref, wb_ref, wc_ref, ga_ref, gb_ref, gc_ref, o_ref):
    ya = _dot(a_ref[...], wa_ref[...])
    yb = _dot(b_ref[...], wb_ref[...])
    yc = _dot(c_ref[...], wc_ref[...])
    out = (_sigmoid(ga_ref[...].astype(F32)) * ya + _sigmoid(gb_ref[...].astype(F32)) * yb
           + _sigmoid(gc_ref[...].astype(F32)) * yc)
    o_ref[...] = out.astype(o_ref.dtype)


def _merge(o_a, o_b, o_c, w_a, w_b, w_c, gates):
    m = o_a.shape[0]
    d = w_a.shape[1]
    tm = _pick(m, (1024, 512, 256, 128))
    tn = _pick(d, (512, 256, 128))
    nj = d // tn
    ka, kb, kc = o_a.shape[1], o_b.shape[1], o_c.shape[1]
    vmem = 2 * 2 * (tm * (ka + kb + kc) + (ka + kb + kc) * tn + 4 * tm * tn) + 6 * tm * tn * 4
    return pl.pallas_call(
        _merge_body,
        grid=(m // tm, nj),
        in_specs=[pl.BlockSpec((tm, ka), lambda i, j: (i, 0)),
                  pl.BlockSpec((tm, kb), lambda i, j: (i, 0)),
                  pl.BlockSpec((tm, kc), lambda i, j: (i, 0)),
                  pl.BlockSpec((ka, tn), lambda i, j: (0, j)),
                  pl.BlockSpec((kb, tn), lambda i, j: (0, j)),
                  pl.BlockSpec((kc, tn), lambda i, j: (0, j)),
                  pl.BlockSpec((tm, tn), lambda i, j: (i, j)),
                  pl.BlockSpec((tm, tn), lambda i, j: (i, nj + j)),
                  pl.BlockSpec((tm, tn), lambda i, j: (i, 2 * nj + j))],
        out_specs=pl.BlockSpec((tm, tn), lambda i, j: (i, j)),
        out_shape=jax.ShapeDtypeStruct((m, d), BF16),
        compiler_params=_cparams(("parallel", "arbitrary"), vmem),
        name="merge",
    )(o_a, o_b, o_c, w_a, w_b, w_c, gates, gates, gates)


def _route_metadata(route, n_experts):
    t = route.shape[0]
    a_tot = TOP_K * t
    a_pad = a_tot + n_experts * MOE_TILE
    e = jnp.concatenate([route[:, ROUTE_E1], route[:, ROUTE_E2]]).astype(jnp.int32)
    order = jnp.argsort(e, stable=True).astype(jnp.int32)
    counts = jnp.sum((e[:, None] == jnp.arange(n_experts, dtype=jnp.int32)[None, :]).astype(jnp.int32), axis=0)
    padded = (counts + MOE_TILE - 1) // MOE_TILE * MOE_TILE
    ends_u = jnp.cumsum(counts)
    ends_p = jnp.cumsum(padded)
    start_u = ends_u - counts
    start_p = ends_p - padded
    p = jnp.arange(a_pad, dtype=jnp.int32)
    ep = jnp.minimum(jnp.searchsorted(ends_p, p, side="right"), n_experts - 1).astype(jnp.int32)
    rank = p - start_p[ep]
    valid = jnp.logical_and(rank < counts[ep], p < ends_p[-1])
    src = jnp.clip(start_u[ep] + rank, 0, a_tot - 1)
    row_token = jnp.where(valid, order[src] % t, 0).astype(jnp.int32)
    rank_sorted = jnp.argsort(order).astype(jnp.int32)
    pos = rank_sorted + (start_p - start_u)[e]
    n_tiles = a_pad // MOE_TILE
    n_valid = (ends_p[-1] // MOE_TILE).astype(jnp.int32)
    tile_start = jnp.arange(n_tiles, dtype=jnp.int32) * MOE_TILE
    tile_e = jnp.minimum(jnp.searchsorted(ends_p, tile_start, side="right"), n_experts - 1).astype(jnp.int32)
    tile_e = jnp.where(tile_start < ends_p[-1], tile_e, tile_e[jnp.maximum(n_valid - 1, 0)])
    return row_token, tile_e, n_valid.reshape(1), pos[:t], pos[t:]


def _row_gather_start(idx_ref, base, src_ref, dst_ref, sem, rows, both_queues=False):
    ways = 2 if both_queues else 1

    def issue(r, carry):
        for way in range(ways):
            row = r * ways + way
            pltpu.make_async_copy(src_ref.at[idx_ref[base + row]], dst_ref.at[row], sem).start(priority=way)
        return carry

    lax.fori_loop(0, rows // ways, issue, 0, unroll=8 // ways)


def _row_gather_wait(src_ref, dst_ref, sem, rows):
    pltpu.make_async_copy(src_ref.at[pl.ds(0, rows)], dst_ref, sem).wait()


def _prefetched_gather(starts, waits):
    i = pl.program_id(0)
    slot = i % 2

    @pl.when(i == 0)
    def _():
        starts(0, 0)

    @pl.when(i + 1 < pl.num_programs(0))
    def _():
        starts(i + 1, 1 - slot)

    waits(slot)
    return slot


def _gather_rows_body(idx_ref, src_ref, o_ref, buf, sem, *, rows):
    def starts(step, slot):
        _row_gather_start(idx_ref, step * rows, src_ref, buf.at[slot], sem.at[slot], rows, both_queues=True)

    def waits(slot):
        _row_gather_wait(src_ref, buf.at[slot], sem.at[slot], rows)

    slot = _prefetched_gather(starts, waits)
    o_ref[...] = buf[slot].astype(o_ref.dtype)


def _gather_rows(src, idx):
    m = idx.shape[0]
    w = src.shape[1]
    rows = GATHER_ROWS
    assert m % rows == 0 and src.shape[0] >= rows
    return pl.pallas_call(
        functools.partial(_gather_rows_body, rows=rows),
        grid_spec=pltpu.PrefetchScalarGridSpec(
            num_scalar_prefetch=1,
            grid=(m // rows,),
            in_specs=[pl.BlockSpec(memory_space=pl.ANY)],
            out_specs=pl.BlockSpec((rows, w), lambda i, idx_ref: (i, 0)),
            scratch_shapes=[pltpu.VMEM((2, rows, w), src.dtype), pltpu.SemaphoreType.DMA((2,))]),
        out_shape=jax.ShapeDtypeStruct((m, w), BF16),
        compiler_params=_cparams(("arbitrary",), rows * w * (2 * 4 + 2 * 2 + 4)),
        name="moe_gather",
    )(idx, src)


def _gmm_new_weights(te_ref):
    i = pl.program_id(1)
    return jnp.logical_or(i == 0, te_ref[i] != te_ref[jnp.maximum(i - 1, 0)])


def _gmm_swiglu_body(te_ref, nv_ref, x_ref, wg_ref, wu_ref, o_ref, wg_bf, wu_bf):
    valid = pl.program_id(1) < nv_ref[0]

    @pl.when(_gmm_new_weights(te_ref))
    def _():
        wg_bf[...] = wg_ref[...].astype(BF16)
        wu_bf[...] = wu_ref[...].astype(BF16)

    @pl.when(valid)
    def _():
        a = x_ref[...]
        g = _dot(a, wg_bf[...])
        u = _dot(a, wu_bf[...])
        o_ref[...] = (_silu(g) * u).astype(o_ref.dtype)

    @pl.when(jnp.logical_not(valid))
    def _():
        o_ref[...] = jnp.zeros_like(o_ref)


def _gmm_down_body(te_ref, nv_ref, a_ref, w_ref, o_ref, w_bf):
    valid = pl.program_id(1) < nv_ref[0]

    @pl.when(_gmm_new_weights(te_ref))
    def _():
        w_bf[...] = w_ref[...].astype(BF16)

    @pl.when(valid)
    def _():
        o_ref[...] = _dot(a_ref[...], w_bf[...])

    @pl.when(jnp.logical_not(valid))
    def _():
        o_ref[...] = jnp.zeros_like(o_ref)


def _gmm_maps():
    def rows(j, i, te, nv):
        return (jnp.minimum(i, nv[0] - 1), 0)

    def weights(j, i, te, nv):
        return (te[i], 0, j)

    def out(j, i, te, nv):
        return (i, j)

    return rows, weights, out


def _gmm_swiglu(x_sorted, wg, wu, tile_e, n_valid):
    m, k = x_sorted.shape
    n = wg.shape[2]
    tm = MOE_TILE
    tn = _pick(n, (512, 256, 128))
    rows, weights, out = _gmm_maps()
    vmem = 2 * (tm * k * 2 + 2 * k * tn * 4 + tm * tn * 2) + 2 * k * tn * 2 + 3 * tm * tn * 4
    return pl.pallas_call(
        _gmm_swiglu_body,
        grid_spec=pltpu.PrefetchScalarGridSpec(
            num_scalar_prefetch=2,
            grid=(n // tn, m // tm),
            in_specs=[pl.BlockSpec((tm, k), rows),
                      pl.BlockSpec((None, k, tn), weights),
                      pl.BlockSpec((None, k, tn), weights)],
            out_specs=pl.BlockSpec((tm, tn), out),
            scratch_shapes=[pltpu.VMEM((k, tn), BF16), pltpu.VMEM((k, tn), BF16)]),
        out_shape=jax.ShapeDtypeStruct((m, n), BF16),
        compiler_params=_cparams(("arbitrary", "arbitrary"), vmem),
        name="moe_up",
    )(tile_e, n_valid, x_sorted, wg, wu)


def _gmm_down(a_sorted, wd, tile_e, n_valid):
    m, k = a_sorted.shape
    n = wd.shape[2]
    tm = MOE_TILE
    tn = _pick(n, (1024, 512, 256, 128))
    rows, weights, out = _gmm_maps()
    vmem = 2 * (tm * k * 2 + k * tn * 4 + tm * tn * 4) + k * tn * 2 + tm * tn * 4
    return pl.pallas_call(
        _gmm_down_body,
        grid_spec=pltpu.PrefetchScalarGridSpec(
            num_scalar_prefetch=2,
            grid=(n // tn, m // tm),
            in_specs=[pl.BlockSpec((tm, k), rows),
                      pl.BlockSpec((None, k, tn), weights)],
            out_specs=pl.BlockSpec((tm, tn), out),
            scratch_shapes=[pltpu.VMEM((k, tn), BF16)]),
        out_shape=jax.ShapeDtypeStruct((m, n), F32),
        compiler_params=_cparams(("arbitrary", "arbitrary"), vmem),
        name="moe_down",
    )(tile_e, n_valid, a_sorted, wd)


def _moe_combine_body(p1_ref, p2_ref, y_ref, x_ref, gate_ref, route_ref, o_ref, buf1, buf2, sem, *, rows):
    def starts(step, slot):
        _row_gather_start(p1_ref, step * rows, y_ref, buf1.at[slot], sem.at[0, slot], rows)
        _row_gather_start(p2_ref, step * rows, y_ref, buf2.at[slot], sem.at[1, slot], rows)

    def waits(slot):
        _row_gather_wait(y_ref, buf1.at[slot], sem.at[0, slot], rows)
        _row_gather_wait(y_ref, buf2.at[slot], sem.at[1, slot], rows)

    slot = _prefetched_gather(starts, waits)
    route = route_ref[...]
    w1 = route[:, ROUTE_W1:ROUTE_W1 + 1]
    w2 = route[:, ROUTE_W2:ROUTE_W2 + 1]
    o_ref[...] = x_ref[...] + gate_ref[...] * (w1 * buf1[slot] + w2 * buf2[slot])


def _moe_combine(y_sorted, pos1, pos2, x, gate, route, seq):
    t, d = x.shape
    rows = _pick(seq, (128,))
    per_batch = seq // rows
    return pl.pallas_call(
        functools.partial(_moe_combine_body, rows=rows),
        grid_spec=pltpu.PrefetchScalarGridSpec(
            num_scalar_prefetch=2,
            grid=(t // rows,),
            in_specs=[pl.BlockSpec(memory_space=pl.ANY),
                      pl.BlockSpec((rows, d), lambda i, p1, p2: (i, 0)),
                      pl.BlockSpec((None, 1, d), lambda i, p1, p2: (i // per_batch, 0, 0)),
                      pl.BlockSpec((rows, V7X_LANES), lambda i, p1, p2: (i, 0))],
            out_specs=pl.BlockSpec((rows, d), lambda i, p1, p2: (i, 0)),
            scratch_shapes=[pltpu.VMEM((2, rows, d), F32), pltpu.VMEM((2, rows, d), F32),
                            pltpu.SemaphoreType.DMA((2, 2))]),
        out_shape=jax.ShapeDtypeStruct((t, d), F32),
        compiler_params=_cparams(("arbitrary",), 10 * rows * d * 4),
        name="moe_combine",
    )(pos1, pos2, y_sorted, x, gate, route)


_HG_LEVELS = (HG_CHUNK // HG_SUB).bit_length() - 1


def _hgrn_level_masks():
    ti = np.arange(HG_CHUNK)[:, None]
    si = np.arange(HG_CHUNK)[None, :]
    out = []
    for lvl in range(_HG_LEVELS):
        half = HG_SUB << lvl
        blk = 2 * half
        out.append((ti // blk == si // blk) & (ti % blk >= half) & (si % blk < half))
    return jnp.asarray(np.stack(out), F32)


def _hgrn_chunk(q_in, f_in, v, g_in, lb, gain, state_t, tri, ones, lvl_mask_ref):
    c = HG_CHUNK
    fg = lb + (1.0 - lb) * _sigmoid(f_in)
    log_f = jnp.log2(jnp.maximum(fg, MIN_FORGET))
    k = 1.0 - fg
    q = _silu(q_in)
    lf_hi, lf_mid, lf_lo = _split3(log_f)
    b = _dot(tri, lf_hi) + _dot(tri, lf_mid) + _dot(tri, lf_lo)
    b_last = b[c - 1:c, :]

    o = _dot_nt((q * jnp.exp2(b)).astype(BF16), state_t.astype(BF16))

    row = lax.broadcasted_iota(jnp.int32, (c, 1), 0)
    scores = jnp.zeros((c, c), F32)
    for lvl in range(_HG_LEVELS):
        half = HG_SUB << lvl
        blk = 2 * half
        bref = jnp.concatenate(
            [jnp.broadcast_to(b[p * blk + half - 1:p * blk + half, :], (blk, HG_DK)) for p in range(c // blk)],
            axis=0)
        is_q = (row & half) != 0
        e = jnp.exp2(-jnp.abs(b - bref))
        xk = jnp.where(is_q, q, k) * e
        qd = jnp.where(is_q, xk, 0.0).astype(BF16)
        kd = jnp.where(is_q, 0.0, xk).astype(BF16)
        scores = scores + _dot_nt(qd, kd) * lvl_mask_ref[lvl]
    o = o + _dot(scores.astype(BF16), v.astype(BF16))

    nb = c // HG_SUB
    b3 = b.reshape(nb, HG_SUB, HG_DK)
    q3 = q.reshape(nb, HG_SUB, HG_DK)
    k3 = k.reshape(nb, HG_SUB, HG_DK)
    v3 = v.reshape(nb, HG_SUB, HG_DK)
    t_in = lax.broadcasted_iota(jnp.int32, (nb, HG_SUB, HG_DK), 1)
    for s in range(HG_SUB):
        diff = b3 - b3[:, s:s + 1, :]
        dec = jnp.exp2(diff if s == 0 else jnp.where(t_in >= s, diff, NEG_BIG))
        m = (q3 * (k3[:, s:s + 1, :] * dec)).reshape(c, HG_DK)
        r = _dot(m.astype(BF16), ones)
        o = o + r * jnp.broadcast_to(v3[:, s:s + 1, :], (nb, HG_SUB, HG_DK)).reshape(c, HG_DK)

    kdec = (k * jnp.exp2(b_last - b)).astype(BF16)
    new_state_t = state_t * jnp.exp2(b_last) + _dot_tn(v.astype(BF16), kdec)

    ms = jnp.mean(o * o, axis=-1, keepdims=True)
    out = o * lax.rsqrt(ms + NORM_EPS) * gain * _silu(g_in)
    return out, new_state_t


def _hgrn_body(q_ref, f_ref, i_ref, g_ref, lbraw_ref, gain_ref, lvl_mask_ref, o_ref, state_ref,
               *, layer, n_chunks, group):
    @pl.when(pl.program_id(2) == 0)
    def _():
        state_ref[...] = jnp.zeros_like(state_ref)

    lbr = lbraw_ref[...]
    ex = jnp.exp(lbr - jnp.max(lbr, axis=0, keepdims=True))
    soft = ex / jnp.sum(ex, axis=0, keepdims=True)
    lb = jnp.zeros((1, group * HG_DK), F32)
    for j in range(1, layer + 1):
        lb = lb + soft[j:j + 1, :]
    gain = gain_ref[...]
    c = HG_CHUNK
    tri = (lax.broadcasted_iota(jnp.int32, (c, c), 0) >= lax.broadcasted_iota(jnp.int32, (c, c), 1)).astype(BF16)
    ones = jnp.ones((HG_DK, HG_DK), BF16)

    def chunk(ci, carry):
        sl = pl.ds(pl.multiple_of(ci * c, c), c)
        for hh in range(group):
            cs = slice(hh * HG_DK, (hh + 1) * HG_DK)
            out, new_state = _hgrn_chunk(q_ref[sl, cs].astype(F32), f_ref[sl, cs].astype(F32),
                                         i_ref[sl, cs].astype(F32), g_ref[sl, cs].astype(F32),
                                         lb[:, cs], gain, state_ref[hh], tri, ones, lvl_mask_ref)
            o_ref[sl, cs] = out.astype(o_ref.dtype)
            state_ref[hh] = new_state
        return carry

    lax.fori_loop(0, n_chunks, chunk, 0)


def _hgrn(proj, lb_raw, out_gain, layer, n_batch, seq, width):
    heads = width // HG_DK
    group = HG_GROUP if heads % HG_GROUP == 0 else 1
    hgroups = heads // group
    gw = group * HG_DK
    lc = _pick(seq, (512, 256, 128))
    per_batch = seq // lc
    depth = lb_raw.shape[0]

    def col(off):
        return pl.BlockSpec((lc, gw), lambda b, h, i: (b * per_batch + i, off * hgroups + h))

    return pl.pallas_call(
        functools.partial(_hgrn_body, layer=layer, n_chunks=lc // HG_CHUNK, group=group),
        grid=(n_batch, hgroups, per_batch),
        in_specs=[col(0), col(1), col(2), col(3),
                  pl.BlockSpec((depth, gw), lambda b, h, i: (0, h)),
                  pl.BlockSpec((1, HG_DK), lambda b, h, i: (0, 0)),
                  pl.BlockSpec((_HG_LEVELS, HG_CHUNK, HG_CHUNK), lambda b, h, i: (0, 0, 0))],
        out_specs=pl.BlockSpec((lc, gw), lambda b, h, i: (b * per_batch + i, h)),
        out_shape=jax.ShapeDtypeStruct((n_batch * seq, width), BF16),
        scratch_shapes=[pltpu.VMEM((group, HG_DK, HG_DK), F32)],
        compiler_params=_cparams(("parallel", "parallel", "arbitrary"), 16 * 1024 * 1024),
        name="hgrn2",
    )(proj, proj, proj, proj, lb_raw, out_gain.reshape(1, HG_DK), _hgrn_level_masks())


def _pool_body(u_ref, halo_ref, w_ref, scale_ref, o_ref, ext_ref, *, ts, gdim):
    i = pl.program_id(1)
    halo = halo_ref[...].astype(F32)
    ext_ref[0:POOL_HALO, :] = jnp.where(i > 0, halo, 0.0)
    ext_ref[POOL_HALO:, :] = u_ref[...].astype(F32)
    pos = (i * ts + lax.broadcasted_iota(jnp.int32, (ts, 1), 0) + 1).astype(F32)
    for g, win in enumerate(POOL_WINDOWS):
        cs = slice(g * gdim, (g + 1) * gdim)
        acc = ext_ref[POOL_HALO:, cs]
        for j in range(1, win):
            acc = acc + ext_ref[POOL_HALO - j:POOL_HALO - j + ts, cs]
        pooled = acc / jnp.minimum(pos, float(win)) - ext_ref[POOL_HALO:, cs]
        y = _dot(pooled.astype(BF16), w_ref[g])
        o_ref[:, cs] = (y * scale_ref[:, cs]).astype(o_ref.dtype)


def _pool(proj, col_off, pool_w, pool_scale, n_batch, seq, width):
    groups = len(POOL_WINDOWS)
    gdim = width // groups
    assert gdim % V7X_LANES == 0 and col_off % width == 0 and max(POOL_WINDOWS) <= POOL_HALO
    ts = _pick(seq, (512, 256, 128))
    per_batch = seq // ts
    cb = col_off // width
    hb = ts // POOL_HALO
    return pl.pallas_call(
        functools.partial(_pool_body, ts=ts, gdim=gdim),
        grid=(n_batch, per_batch),
        in_specs=[pl.BlockSpec((ts, width), lambda b, i: (b * per_batch + i, cb)),
                  pl.BlockSpec((POOL_HALO, width),
                               lambda b, i: (jnp.maximum((b * per_batch + i) * hb - 1, 0), cb)),
                  pl.BlockSpec((groups, gdim, gdim), lambda b, i: (0, 0, 0)),
                  pl.BlockSpec((1, width), lambda b, i: (0, 0))],
        out_specs=pl.BlockSpec((ts, width), lambda b, i: (b * per_batch + i, 0)),
        out_shape=jax.ShapeDtypeStruct((n_batch * seq, width), BF16),
        scratch_shapes=[pltpu.VMEM((ts + POOL_HALO, width), F32)],
        compiler_params=_cparams(("parallel", "parallel"), 16 * 1024 * 1024),
        name="pool",
    )(proj, proj, pool_w, pool_scale.reshape(1, width))


def _rope_table_body(pos_ref, cos_ref, sin_ref):
    pos = pos_ref[...].astype(F32)
    lane = lax.broadcasted_iota(jnp.int32, (1, V7X_LANES), 1)
    j = lane % MLA_ROPE
    fidx = (j % (MLA_ROPE // 2)).astype(F32)
    inv_freq = jnp.exp(fidx * (-2.0 / MLA_ROPE * math.log(ROPE_THETA)))
    ang = pos * inv_freq
    cos_ref[...] = jnp.cos(ang)
    sin_ref[...] = jnp.where(j < MLA_ROPE // 2, -1.0, 1.0) * jnp.sin(ang)


def _rope_table(positions):
    t = positions.size
    ts = _pick(t, (512, 256, 128))
    return pl.pallas_call(
        _rope_table_body,
        grid=(t // ts,),
        in_specs=[pl.BlockSpec((ts, 1), lambda i: (i, 0))],
        out_specs=[pl.BlockSpec((ts, V7X_LANES), lambda i: (i, 0))] * 2,
        out_shape=[jax.ShapeDtypeStruct((t, V7X_LANES), F32)] * 2,
        compiler_params=_cparams(("parallel",), 4 * 1024 * 1024),
        name="rope_table",
    )(positions.reshape(t, 1))


def _rms(x, gain):
    return x * lax.rsqrt(jnp.mean(x * x, axis=-1, keepdims=True) + NORM_EPS) * gain


def _prep_q_body(cq_ref, qn_ref, w_ref, gn_ref, gr_ref, grs_ref, seg_ref, cos_ref, sin_ref, o_ref,
                 *, heads, sm_scale):
    hn = heads * MLA_NOPE
    hr = heads * MLA_ROPE
    hq = _rms(cq_ref[...].astype(F32), qn_ref[...]).astype(BF16)
    y = _dot(hq, w_ref[...])
    yr = y[:, hn:hn + hr]
    ys = y[:, hn + hr:]
    sq_hi, sq_lo = _split2(yr * yr)
    seg = seg_ref[...]
    ss = _dot(sq_hi, seg) + _dot(sq_lo, seg)
    inv = lax.rsqrt(ss * (1.0 / MLA_ROPE) + NORM_EPS)
    reps = hr // V7X_LANES
    cosf = jnp.concatenate([cos_ref[...]] * reps, axis=1)
    sinf = jnp.concatenate([sin_ref[...]] * reps, axis=1)
    qr = (yr * inv * gr_ref[...]) * cosf + (ys * inv * grs_ref[...]) * sinf
    gn = gn_ref[...] * sm_scale
    for h in range(heads):
        qn = _rms(y[:, h * MLA_NOPE:(h + 1) * MLA_NOPE], gn)
        o_ref[h, :, 0:MLA_NOPE] = qn.astype(o_ref.dtype)
        o_ref[h, :, MLA_NOPE:] = (qr[:, h * MLA_ROPE:(h + 1) * MLA_ROPE] * sm_scale).astype(o_ref.dtype)


def _prep_kv_body(ckv_ref, kpe_ref, kvn_ref, w_ref, gn_ref, gr2_ref, cos_ref, sin_ref, k_ref, v_ref, *, heads):
    hk = _rms(ckv_ref[...].astype(F32), kvn_ref[...]).astype(BF16)
    y = _dot(hk, w_ref[...])
    kp = kpe_ref[...].astype(F32)
    kpe = kp[:, :MLA_ROPE]
    inv = lax.rsqrt(jnp.mean(kpe * kpe, axis=-1, keepdims=True) + NORM_EPS)
    kn = kp * inv * gr2_ref[...]
    kr = (kn[:, :MLA_ROPE] * cos_ref[:, :MLA_ROPE] + kn[:, MLA_ROPE:] * sin_ref[:, :MLA_ROPE]).astype(k_ref.dtype)
    per = MLA_NOPE + MLA_V
    for h in range(heads):
        k_ref[h, :, 0:MLA_NOPE] = _rms(y[:, h * per:h * per + MLA_NOPE], gn_ref[...]).astype(k_ref.dtype)
        k_ref[h, :, MLA_NOPE:] = kr
        v_ref[h, 0, :MLA_V, :] = y[:, h * per + MLA_NOPE:(h + 1) * per].T.astype(v_ref.dtype)
        v_ref[h, 0, MLA_V:, :] = jnp.ones((V7X_BF16_ROWS, y.shape[0]), v_ref.dtype)


def _flash_body(q_ref, k_ref, vt_ref, o_ref, acc_ref, st_a, st_b, *, tile, group):
    qi = pl.program_id(2)
    acc_ref[...] = jnp.zeros_like(acc_ref)

    def scores(j, dst):
        ks = pl.ds(pl.multiple_of(j * tile, tile), tile)
        for g in range(group):
            dst[g] = _dot_nt(k_ref[g, ks, :], q_ref[g])

    def consume(j, src, ms, masked):
        m_new = []
        for g in range(group):
            st = src[g]
            if masked:
                kidx = lax.broadcasted_iota(jnp.int32, (tile, tile), 0)
                qidx = lax.broadcasted_iota(jnp.int32, (tile, tile), 1)
                st = jnp.where(kidx <= qidx, st, NEG_BIG)
            m = jnp.maximum(ms[g], jnp.max(st, axis=0, keepdims=True))
            p = jnp.exp2(st - m).astype(BF16)
            acc_ref[g] = jnp.exp2(ms[g] - m) * acc_ref[g] + _dot(vt_ref[g, j], p)
            m_new.append(m)
        return tuple(m_new)

    def finish():
        for g in range(group):
            o = acc_ref[g, :MLA_V, :] / acc_ref[g, MLA_V:MLA_V + 1, :]
            o_ref[:, g * MLA_V:(g + 1) * MLA_V] = o.T.astype(o_ref.dtype)

    scores(0, st_a)

    def pair(pi, ms):
        j = 2 * pi
        scores(j + 1, st_b)
        ms = consume(j, st_a, ms, False)
        scores(j + 2, st_a)
        return consume(j + 1, st_b, ms, False)

    m0 = tuple(jnp.full((1, tile), NEG_BIG, F32) for _ in range(group))
    ms = lax.fori_loop(0, qi // 2, pair, m0)
    odd = qi % 2 == 1

    @pl.when(odd)
    def _():
        scores(qi, st_b)
        consume(qi, st_b, consume(qi - 1, st_a, ms, False), True)
        finish()

    @pl.when(jnp.logical_not(odd))
    def _():
        consume(qi, st_a, ms, True)
        finish()


def _mla(proj_c, cos_t, sin_t, q_norm, w_q, kv_norm, w_kv, g_qn, g_qr, g_kn, g_kr,
         n_batch, seq, heads, q_rank, kv_rank):
    t = n_batch * seq
    half = MLA_ROPE // 2
    hn, hr = heads * MLA_NOPE, heads * MLA_ROPE
    dq = MLA_NOPE + MLA_ROPE
    sm_scale = float(dq) ** -0.5 * math.log2(math.e)
    assert q_rank % kv_rank == 0 and (q_rank + kv_rank) % V7X_LANES == 0 and hr % V7X_LANES == 0
    tile = ATTN_TILE
    assert seq % tile == 0
    tm = tile
    vrows = MLA_V + V7X_BF16_ROWS

    def swap(g):
        return jnp.concatenate([g[half:], g[:half]])

    gr = jnp.tile(g_qr, heads).reshape(1, hr)
    grs = jnp.tile(swap(g_qr), heads).reshape(1, hr)
    lane = jnp.arange(hr) // MLA_ROPE
    seg = (lane[:, None] == lane[None, :]).astype(BF16)
    row = lambda i: (i, 0)
    const2 = lambda i: (0, 0)
    qcat = pl.pallas_call(
        functools.partial(_prep_q_body, heads=heads, sm_scale=sm_scale),
        grid=(t // tm,),
        in_specs=[pl.BlockSpec((tm, q_rank), row),
                  pl.BlockSpec((1, q_rank), const2),
                  pl.BlockSpec((q_rank, hn + 2 * hr), const2),
                  pl.BlockSpec((1, MLA_NOPE), const2),
                  pl.BlockSpec((1, hr), const2),
                  pl.BlockSpec((1, hr), const2),
                  pl.BlockSpec((hr, hr), const2),
                  pl.BlockSpec((tm, V7X_LANES), row),
                  pl.BlockSpec((tm, V7X_LANES), row)],
        out_specs=pl.BlockSpec((heads, tm, dq), lambda i: (0, i, 0)),
        out_shape=jax.ShapeDtypeStruct((heads, t, dq), BF16),
        compiler_params=_cparams(("parallel",), 40 * 1024 * 1024),
        name="mla_prep_q",
    )(proj_c, q_norm.reshape(1, q_rank), w_q, g_qn.reshape(1, MLA_NOPE), gr, grs, seg, cos_t, sin_t)

    gr2 = jnp.concatenate([g_kr, swap(g_kr)]).reshape(1, 2 * MLA_ROPE)
    kcat, vt = pl.pallas_call(
        functools.partial(_prep_kv_body, heads=heads),
        grid=(t // tm,),
        in_specs=[pl.BlockSpec((tm, kv_rank), lambda i: (i, q_rank // kv_rank)),
                  pl.BlockSpec((tm, V7X_LANES), lambda i: (i, (q_rank + kv_rank) // V7X_LANES)),
                  pl.BlockSpec((1, kv_rank), const2),
                  pl.BlockSpec((kv_rank, heads * (MLA_NOPE + MLA_V)), const2),
                  pl.BlockSpec((1, MLA_NOPE), const2),
                  pl.BlockSpec((1, 2 * MLA_ROPE), const2),
                  pl.BlockSpec((tm, V7X_LANES), row),
                  pl.BlockSpec((tm, V7X_LANES), row)],
        out_specs=[pl.BlockSpec((heads, tm, dq), lambda i: (0, i, 0)),
                   pl.BlockSpec((heads, 1, vrows, tile), lambda i: (0, i, 0, 0))],
        out_shape=[jax.ShapeDtypeStruct((heads, t, dq), BF16),
                   jax.ShapeDtypeStruct((heads, t // tile, vrows, tile), BF16)],
        compiler_params=_cparams(("parallel",), 40 * 1024 * 1024),
        name="mla_prep_kv",
    )(proj_c, proj_c, kv_norm.reshape(1, kv_rank), w_kv, g_kn.reshape(1, MLA_NOPE), gr2, cos_t, sin_t)

    nq = seq // tile
    group = ATTN_GROUP if heads % ATTN_GROUP == 0 else 1
    return pl.pallas_call(
        functools.partial(_flash_body, tile=tile, group=group),
        grid=(heads // group, n_batch, nq),
        in_specs=[pl.BlockSpec((group, tile, dq), lambda h, b, i: (h, b * nq + i, 0)),
                  pl.BlockSpec((group, seq, dq), lambda h, b, i: (h, b, 0)),
                  pl.BlockSpec((group, nq, vrows, tile), lambda h, b, i: (h, b, 0, 0))],
        out_specs=pl.BlockSpec((tile, group * MLA_V), lambda h, b, i: (b * nq + i, h)),
        out_shape=jax.ShapeDtypeStruct((t, heads * MLA_V), BF16),
        scratch_shapes=[pltpu.VMEM((group, vrows, tile), F32),
                        pltpu.VMEM((group, tile, tile), F32), pltpu.VMEM((group, tile, tile), F32)],
        compiler_params=_cparams(("parallel", "parallel", "arbitrary"), 48 * 1024 * 1024),
        name="mla_flash",
    )(qcat, kcat, vt)


def kernel(x, c, positions, ada_w, ada_b, ada_layer, mix_norm, ffn_norm, w_in, hgrn_lower_bounds,
           hgrn_out_norm, pool_w, pool_scale, mla_q_norm, mla_w_uq, mla_kv_norm, mla_w_ukv,
           mla_qk_norm_q_nope, mla_qk_norm_q_rope, mla_qk_norm_k_nope, mla_qk_norm_k_rope,
           w_branch_a, w_branch_b, w_branch_c, w_o, ffn_w_gate, ffn_w_up, ffn_w_down,
           moe_router, moe_w_gate, moe_w_up, moe_w_down):
    n_batch, seq, d = x.shape
    depth = w_in.shape[0]
    t = n_batch * seq
    hg_width = hgrn_lower_bounds.shape[1]
    pool_width = pool_scale.shape[1]
    q_rank = mla_q_norm.shape[1]
    kv_rank = mla_kv_norm.shape[1]
    heads = mla_w_ukv.shape[2] // (MLA_NOPE + MLA_V)
    half = MLA_ROPE // 2
    n_a = 4 * hg_width + pool_width
    n_c = q_rank + kv_rank + MLA_ROPE
    assert w_in.shape[2] == n_a + n_c + 3 * d

    mod = _ada(c, ada_w, ada_b, ada_layer)
    cos_t, sin_t = _rope_table(positions)
    xf = x.reshape(t, d)

    for l in range(depth):
        mod_l = mod[l]
        gate1 = mod_l[:, 2:3, :]
        gate2 = mod_l[:, 5:6, :]

        w_l = w_in[l]
        w_a = w_l[:, :n_a].astype(BF16)
        kpe_w = w_l[:, n_a + q_rank + kv_rank:n_a + n_c]
        w_c = jnp.concatenate([w_l[:, n_a:n_a + n_c], kpe_w[:, half:], kpe_w[:, :half]], axis=1).astype(BF16)
        w_g = w_l[:, n_a + n_c:].astype(BF16)
        wq = mla_w_uq[l].reshape(q_rank, heads, MLA_NOPE + MLA_ROPE)
        wq_r = wq[:, :, MLA_NOPE:]
        w_q = jnp.concatenate([
            wq[:, :, :MLA_NOPE].reshape(q_rank, heads * MLA_NOPE),
            wq_r.reshape(q_rank, heads * MLA_ROPE),
            jnp.concatenate([wq_r[:, :, half:], wq_r[:, :, :half]], axis=2).reshape(q_rank, heads * MLA_ROPE),
        ], axis=1).astype(BF16)

        h = _modulate(xf.reshape(n_batch, seq, d), mix_norm[l], mod_l, 0, 1)
        proj_a = _mm_cast(h, w_a, name="proj_a")
        proj_c = _mm_cast(h, w_c, name="proj_c")
        gates = _mm_cast(h, w_g, name="proj_gates")

        o_a = _hgrn(proj_a, hgrn_lower_bounds, hgrn_out_norm[l], l, n_batch, seq, hg_width)
        o_b = _pool(proj_a, 4 * hg_width, pool_w[l].astype(BF16), pool_scale[l], n_batch, seq, pool_width)
        o_c = _mla(proj_c, cos_t, sin_t, mla_q_norm[l], w_q, mla_kv_norm[l], mla_w_ukv[l].astype(BF16),
                   mla_qk_norm_q_nope[l], mla_qk_norm_q_rope[l], mla_qk_norm_k_nope[l], mla_qk_norm_k_rope[l],
                   n_batch, seq, heads, q_rank, kv_rank)
        merged = _merge(o_a, o_b, o_c, w_branch_a[l].astype(BF16), w_branch_b[l].astype(BF16),
                        w_branch_c[l].astype(BF16), gates)
        xf = _mm_residual(merged, w_o[l].astype(BF16), xf, gate1, seq, name="out_proj")

        j = l // 2
        if l % 2 == 0:
            h = _modulate(xf.reshape(n_batch, seq, d), ffn_norm[l], mod_l, 3, 4)
            act = _mm_swiglu(h, ffn_w_gate[j].astype(BF16), ffn_w_up[j].astype(BF16), name="ffn_up")
            xf = _mm_residual(act, ffn_w_down[j].astype(BF16), xf, gate2, seq, name="ffn_down")
        else:
            n_experts = moe_w_gate.shape[1]
            h, route = _modulate(xf.reshape(n_batch, seq, d), ffn_norm[l], mod_l, 3, 4, router=moe_router[j])
            row_token, tile_e, n_valid, pos1, pos2 = _route_metadata(route, n_experts)
            h_sorted = _gather_rows(h, row_token)
            act = _gmm_swiglu(h_sorted, moe_w_gate[j], moe_w_up[j], tile_e, n_valid)
            y_sorted = _gmm_down(act, moe_w_down[j], tile_e, n_valid)
            xf = _moe_combine(y_sorted, pos1, pos2, xf, gate2, route, seq)
    return xf.reshape(n_batch, seq, d)
```

```python
import functools
import math

import numpy as np
import jax
import jax.numpy as jnp
from jax import lax
from jax.experimental import pallas as pl
from jax.experimental.pallas import tpu as pltpu

F32 = jnp.float32
BF16 = jnp.bfloat16

HG_DK = 128
POOL_WINDOWS = (2, 4, 8, 16)
MLA_NOPE = 128
MLA_ROPE = 64
MLA_V = 128
ROPE_THETA = 10000.0
MIN_FORGET = 1e-30
NORM_EPS = 1e-6
N_MOD = 6
TOP_K = 2
NEG_BIG = -1e30

V7X_LANES = 128
V7X_SUBLANES = 8
V7X_BF16_ROWS = 16
V7X_VMEM_BYTES = 64 * 1024 * 1024
VMEM_CAP = V7X_VMEM_BYTES - 8 * 1024 * 1024

HG_CHUNK = 128
HG_SUB = 8
HG_GROUP = 8
POOL_HALO = 16
ADA_CHUNK = 512
ATTN_TILE = 1024
ATTN_GROUP = 2
ROUTE_E1, ROUTE_E2, ROUTE_W1, ROUTE_W2 = 0, 1, 2, 3
MOE_TILE = 512
GATHER_ROWS = 512


def _pick(n, prefs):
    for p in prefs:
        if n % p == 0:
            return p
    raise ValueError(f"no tile in {prefs} divides {n}")


def _cparams(sem, vmem_bytes):
    limit = int(min(VMEM_CAP, max(32 * 1024 * 1024, vmem_bytes * 5 // 4)))
    return pltpu.CompilerParams(dimension_semantics=sem, vmem_limit_bytes=limit)


def _sigmoid(x):
    return 1.0 / (1.0 + jnp.exp(-x))


def _silu(x):
    return x * _sigmoid(x)


def _dot(a, b):
    return jnp.dot(a, b, preferred_element_type=F32)


def _dot_nt(a, b):
    return lax.dot_general(a, b, (((1,), (1,)), ((), ())), preferred_element_type=F32)


def _dot_tn(a, b):
    return lax.dot_general(a, b, (((0,), (0,)), ((), ())), preferred_element_type=F32)


def _split3(x):
    hi = x.astype(BF16)
    r1 = x - hi.astype(F32)
    mid = r1.astype(BF16)
    lo = (r1 - mid.astype(F32)).astype(BF16)
    return hi, mid, lo


def _split2(x):
    hi = x.astype(BF16)
    lo = (x - hi.astype(F32)).astype(BF16)
    return hi, lo


def _ada_body(ct_ref, w_ref, b_ref, lay_ref, o_ref, acc_ref, *, n_batch, depth):
    kk = pl.program_id(1)
    tk, tn = w_ref.shape
    sub = V7X_SUBLANES

    @pl.when(kk == 0)
    def _():
        acc_ref[...] = jnp.zeros_like(acc_ref)

    s = _silu(ct_ref[pl.ds(pl.multiple_of(kk * tk, tk), tk), :])
    for c0 in range(0, tn, ADA_CHUNK):
        w = w_ref[:, c0:c0 + ADA_CHUNK]
        for b in range(n_batch):
            prod = (w * s[:, b:b + 1]).reshape(tk // sub, sub, ADA_CHUNK)
            acc_ref[b, :, c0:c0 + ADA_CHUNK] += jnp.sum(prod, axis=0)

    @pl.when(kk == pl.num_programs(1) - 1)
    def _():
        for b in range(n_batch):
            r = jnp.sum(acc_ref[b], axis=0, keepdims=True) + b_ref[...]
            for l in range(depth):
                o_ref[l, b:b + 1, :] = r + lay_ref[l:l + 1, :]


def _ada(c, ada_w, ada_b, ada_layer):
    n_batch, d = c.shape
    depth = ada_layer.shape[0]
    n = ada_w.shape[1]
    tn = n
    tk = _pick(d, (64, 32, 16, 8))
    assert tn % ADA_CHUNK == 0
    ct = c.T
    lay = ada_layer.reshape(depth, n)
    out = pl.pallas_call(
        functools.partial(_ada_body, n_batch=n_batch, depth=depth),
        scratch_shapes=[pltpu.VMEM((n_batch, V7X_SUBLANES, tn), F32)],
        grid=(n // tn, d // tk),
        in_specs=[
            pl.BlockSpec((d, n_batch), lambda j, k: (0, 0)),
            pl.BlockSpec((tk, tn), lambda j, k: (k, j)),
            pl.BlockSpec((1, tn), lambda j, k: (0, j)),
            pl.BlockSpec((depth, tn), lambda j, k: (0, j)),
        ],
        out_specs=pl.BlockSpec((depth, n_batch, tn), lambda j, k: (0, 0, j)),
        out_shape=jax.ShapeDtypeStruct((depth, n_batch, n), F32),
        compiler_params=_cparams(("parallel", "arbitrary"), 4 * tk * tn * 4),
        name="ada",
    )(ct, ada_w, ada_b.reshape(1, n), lay)
    return out.reshape(depth, n_batch, N_MOD, d)


def _modulated(x_ref, g_ref, mod_ref, shift_idx, scale_idx):
    x = x_ref[...]
    ms = jnp.mean(x * x, axis=-1, keepdims=True)
    y = x * lax.rsqrt(ms + NORM_EPS) * g_ref[...]
    return y * (1.0 + mod_ref[scale_idx:scale_idx + 1, :]) + mod_ref[shift_idx:shift_idx + 1, :]


def _modulate_body(x_ref, g_ref, mod_ref, o_ref, *, shift_idx, scale_idx):
    o_ref[...] = _modulated(x_ref, g_ref, mod_ref, shift_idx, scale_idx).astype(o_ref.dtype)


def _modulate_route_body(x_ref, g_ref, mod_ref, r_ref, o_ref, route_ref, *, shift_idx, scale_idx, n_experts):
    h = _modulated(x_ref, g_ref, mod_ref, shift_idx, scale_idx)
    o_ref[...] = h
    h_hi, h_mid, h_lo = _split3(h)
    r = r_ref[...]
    r_hi, r_mid, r_lo = _split3(r)
    logits = (_dot(h_hi, r_hi) + _dot(h_hi, r_mid) + _dot(h_mid, r_hi)
              + _dot(h_hi, r_lo) + _dot(h_mid, r_mid) + _dot(h_lo, r_hi))
    lane = lax.broadcasted_iota(jnp.int32, logits.shape, 1).astype(F32)
    lg = jnp.where(lane < n_experts, logits, -jnp.inf)
    m1 = jnp.max(lg, axis=-1, keepdims=True)
    i1 = jnp.min(jnp.where(lg == m1, lane, float(V7X_LANES)), axis=-1, keepdims=True)
    lg2 = jnp.where(lane == i1, -jnp.inf, lg)
    m2 = jnp.max(lg2, axis=-1, keepdims=True)
    i2 = jnp.min(jnp.where(lg2 == m2, lane, float(V7X_LANES)), axis=-1, keepdims=True)
    e2 = jnp.exp(m2 - m1)
    w1 = 1.0 / (1.0 + e2)
    w2 = e2 / (1.0 + e2)
    route_ref[...] = (jnp.where(lane == ROUTE_E1, i1, 0.0) + jnp.where(lane == ROUTE_E2, i2, 0.0)
                      + jnp.where(lane == ROUTE_W1, w1, 0.0) + jnp.where(lane == ROUTE_W2, w2, 0.0))


def _modulate(x3, gain, mod_l, shift_idx, scale_idx, router=None):
    n_batch, seq, d = x3.shape
    ts = _pick(seq, (512, 256, 128))
    grid = (n_batch, seq // ts)
    x_spec = pl.BlockSpec((None, ts, d), lambda b, i: (b, i, 0))
    g_spec = pl.BlockSpec((1, d), lambda b, i: (0, 0))
    mod_spec = pl.BlockSpec((None, N_MOD, d), lambda b, i: (b, 0, 0))
    h_spec = pl.BlockSpec((None, ts, d), lambda b, i: (b, i, 0))
    vmem = 2 * ts * d * (4 + 2) + 4 * ts * d * 4
    if router is None:
        h = pl.pallas_call(
            functools.partial(_modulate_body, shift_idx=shift_idx, scale_idx=scale_idx),
            grid=grid,
            in_specs=[x_spec, g_spec, mod_spec],
            out_specs=h_spec,
            out_shape=jax.ShapeDtypeStruct((n_batch, seq, d), BF16),
            compiler_params=_cparams(("parallel", "parallel"), vmem),
            name="modulate",
        )(x3, gain.reshape(1, d), mod_l)
        return h.reshape(n_batch * seq, d)
    n_experts = router.shape[1]
    assert n_experts <= V7X_LANES
    r_pad = jnp.zeros((d, V7X_LANES), F32).at[:, :n_experts].set(router)
    h, route = pl.pallas_call(
        functools.partial(_modulate_route_body, shift_idx=shift_idx, scale_idx=scale_idx, n_experts=n_experts),
        grid=grid,
        in_specs=[x_spec, g_spec, mod_spec, pl.BlockSpec((d, V7X_LANES), lambda b, i: (0, 0))],
        out_specs=[h_spec, pl.BlockSpec((None, ts, V7X_LANES), lambda b, i: (b, i, 0))],
        out_shape=[jax.ShapeDtypeStruct((n_batch, seq, d), F32),
                   jax.ShapeDtypeStruct((n_batch, seq, V7X_LANES), F32)],
        compiler_params=_cparams(("parallel", "parallel"), vmem + 2 * ts * d * 2 + 6 * ts * d * 2),
        name="modulate_route",
    )(x3, gain.reshape(1, d), mod_l, r_pad)
    return h.reshape(n_batch * seq, d), route.reshape(n_batch * seq, V7X_LANES)


def _mm_cast_body(a_ref, w_ref, o_ref):
    o_ref[...] = _dot(a_ref[...], w_ref[...]).astype(o_ref.dtype)


def _mm_cast(a, w, out_dtype=BF16, name="mm"):
    m, k = a.shape
    n = w.shape[1]
    tn = n if n <= 2048 else _pick(n, (1024, 512, 256, 128))
    need = lambda tm_: 2 * (tm_ * k * 2 + k * tn * 2 + tm_ * tn * 2) + tm_ * tn * 4
    tm = next(t_ for t_ in (1024, 512, 256, 128) if m % t_ == 0 and need(t_) * 5 // 4 <= VMEM_CAP)
    vmem = need(tm)
    return pl.pallas_call(
        _mm_cast_body,
        grid=(m // tm, n // tn),
        in_specs=[pl.BlockSpec((tm, k), lambda i, j: (i, 0)),
                  pl.BlockSpec((k, tn), lambda i, j: (0, j))],
        out_specs=pl.BlockSpec((tm, tn), lambda i, j: (i, j)),
        out_shape=jax.ShapeDtypeStruct((m, n), out_dtype),
        compiler_params=_cparams(("parallel", "arbitrary"), vmem),
        name=name,
    )(a, w)


def _mm_swiglu_body(a_ref, wg_ref, wu_ref, o_ref):
    a = a_ref[...]
    g = _dot(a, wg_ref[...])
    u = _dot(a, wu_ref[...])
    o_ref[...] = (_silu(g) * u).astype(o_ref.dtype)


def _mm_swiglu(a, wg, wu, name="swiglu"):
    m, k = a.shape
    n = wg.shape[1]
    tn = _pick(n, (512, 256, 128))
    need = lambda tm_: 2 * (tm_ * k * 2 + 2 * k * tn * 2 + tm_ * tn * 2) + 3 * tm_ * tn * 4
    tm = next(t_ for t_ in (2048, 1024, 512, 256, 128) if m % t_ == 0 and need(t_) * 10 // 9 <= VMEM_CAP)
    vmem = need(tm)
    return pl.pallas_call(
        _mm_swiglu_body,
        grid=(m // tm, n // tn),
        in_specs=[pl.BlockSpec((tm, k), lambda i, j: (i, 0)),
                  pl.BlockSpec((k, tn), lambda i, j: (0, j)),
                  pl.BlockSpec((k, tn), lambda i, j: (0, j))],
        out_specs=pl.BlockSpec((tm, tn), lambda i, j: (i, j)),
        out_shape=jax.ShapeDtypeStruct((m, n), BF16),
        compiler_params=_cparams(("parallel", "arbitrary"), vmem),
        name=name,
    )(a, wg, wu)


def _mm_residual_body(a_ref, w_ref, x_ref, gate_ref, o_ref, *, nk):
    scale = gate_ref[...]
    part = _dot(a_ref[...], w_ref[...])
    if nk == 1:
        o_ref[...] = x_ref[...] + scale * part
    else:
        kk = pl.program_id(2)

        @pl.when(kk == 0)
        def _():
            o_ref[...] = part

        @pl.when(jnp.logical_and(kk > 0, kk < nk - 1))
        def _():
            o_ref[...] += part

        @pl.when(kk == nk - 1)
        def _():
            o_ref[...] = x_ref[...] + scale * (o_ref[...] + part)


def _mm_residual(a, w, x, gate, seq, name="mm_res"):
    m, k = a.shape
    n = w.shape[1]
    tm = _pick(seq, (1024, 512, 256, 128))
    tn = _pick(n, (1024, 512, 256, 128))
    if k <= 4096:
        tk = k
    else:
        tk = next(t for t in range(4096 // V7X_LANES * V7X_LANES, 0, -V7X_LANES) if k % t == 0)
        if tk < 512:
            tk = next(t for t in range(k // 2 // V7X_LANES * V7X_LANES, 0, -V7X_LANES) if k % t == 0)
    nk = k // tk
    if tk > 4096:
        tn = _pick(n, (512, 256, 128))
    per_batch = seq // tm
    in_specs = [pl.BlockSpec((tm, tk), lambda i, j, kk: (i, kk)),
                pl.BlockSpec((tk, tn), lambda i, j, kk: (kk, j)),
                pl.BlockSpec((tm, tn), lambda i, j, kk: (i, j)),
                pl.BlockSpec((None, 1, tn), lambda i, j, kk: (i // per_batch, 0, j))]
    vmem = 2 * (tm * tk * 2 + tk * tn * 2 + 2 * tm * tn * 4) + 2 * tm * tn * 4
    return pl.pallas_call(
        functools.partial(_mm_residual_body, nk=nk),
        grid=(m // tm, n // tn, nk),
        in_specs=in_specs,
        out_specs=pl.BlockSpec((tm, tn), lambda i, j, kk: (i, j)),
        out_shape=jax.ShapeDtypeStruct((m, n), F32),
        compiler_params=_cparams(("parallel", "parallel", "arbitrary"), vmem),
        name=name,
    )(a, w, x, gate)


def _merge_body(a_ref, b_ref, c_ref, wa_ref, wb_ref, wc_ref, ga_ref, gb_ref, gc_ref, o_ref):
    ya = _dot(a_ref[...], wa_ref[...])
    yb = _dot(b_ref[...], wb_ref[...])
    yc = _dot(c_ref[...], wc_ref[...])
    out = (_sigmoid(ga_ref[...].astype(F32)) * ya + _sigmoid(gb_ref[...].astype(F32)) * yb
           + _sigmoid(gc_ref[...].astype(F32)) * yc)
    o_ref[...] = out.astype(o_ref.dtype)


def _merge(o_a, o_b, o_c, w_a, w_b, w_c, gates):
    m = o_a.shape[0]
    d = w_a.shape[1]
    tm = _pick(m, (1024, 512, 256, 128))
    tn = _pick(d, (512, 256, 128))
    nj = d // tn
    ka, kb, kc = o_a.shape[1], o_b.shape[1], o_c.shape[1]
    vmem = 2 * 2 * (tm * (ka + kb + kc) + (ka + kb + kc) * tn + 4 * tm * tn) + 6 * tm * tn * 4
    return pl.pallas_call(
        _merge_body,
        grid=(m // tm, nj),
        in_specs=[pl.BlockSpec((tm, ka), lambda i, j: (i, 0)),
                  pl.BlockSpec((tm, kb), lambda i, j: (i, 0)),
                  pl.BlockSpec((tm, kc), lambda i, j: (i, 0)),
                  pl.BlockSpec((ka, tn), lambda i, j: (0, j)),
                  pl.BlockSpec((kb, tn), lambda i, j: (0, j)),
                  pl.BlockSpec((kc, tn), lambda i, j: (0, j)),
                  pl.BlockSpec((tm, tn), lambda i, j: (i, j)),
                  pl.BlockSpec((tm, tn), lambda i, j: (i, nj + j)),
                  pl.BlockSpec((tm, tn), lambda i, j: (i, 2 * nj + j))],
        out_specs=pl.BlockSpec((tm, tn), lambda i, j: (i, j)),
        out_shape=jax.ShapeDtypeStruct((m, d), BF16),
        compiler_params=_cparams(("parallel", "arbitrary"), vmem),
        name="merge",
    )(o_a, o_b, o_c, w_a, w_b, w_c, gates, gates, gates)


def _route_metadata(route, n_experts):
    t = route.shape[0]
    a_tot = TOP_K * t
    a_pad = a_tot + n_experts * MOE_TILE
    e = jnp.concatenate([route[:, ROUTE_E1], route[:, ROUTE_E2]]).astype(jnp.int32)
    order = jnp.argsort(e, stable=True).astype(jnp.int32)
    counts = jnp.sum((e[:, None] == jnp.arange(n_experts, dtype=jnp.int32)[None, :]).astype(jnp.int32), axis=0)
    padded = (counts + MOE_TILE - 1) // MOE_TILE * MOE_TILE
    ends_u = jnp.cumsum(counts)
    ends_p = jnp.cumsum(padded)
    start_u = ends_u - counts
    start_p = ends_p - padded
    p = jnp.arange(a_pad, dtype=jnp.int32)
    ep = jnp.minimum(jnp.searchsorted(ends_p, p, side="right"), n_experts - 1).astype(jnp.int32)
    rank = p - start_p[ep]
    valid = jnp.logical_and(rank < counts[ep], p < ends_p[-1])
    src = jnp.clip(start_u[ep] + rank, 0, a_tot - 1)
    row_token = jnp.where(valid, order[src] % t, 0).astype(jnp.int32)
    rank_sorted = jnp.argsort(order).astype(jnp.int32)
    pos = rank_sorted + (start_p - start_u)[e]
    n_tiles = a_pad // MOE_TILE
    n_valid = (ends_p[-1] // MOE_TILE).astype(jnp.int32)
    tile_start = jnp.arange(n_tiles, dtype=jnp.int32) * MOE_TILE
    tile_e = jnp.minimum(jnp.searchsorted(ends_p, tile_start, side="right"), n_experts - 1).astype(jnp.int32)
    tile_e = jnp.where(tile_start < ends_p[-1], tile_e, tile_e[jnp.maximum(n_valid - 1, 0)])
    return row_token, tile_e, n_valid.reshape(1), pos[:t], pos[t:]


def _row_gather_start(idx_ref, base, src_ref, dst_ref, sem, rows):
    def issue(r, carry):
        pltpu.make_async_copy(src_ref.at[idx_ref[base + r]], dst_ref.at[r], sem).start()
        return carry

    lax.fori_loop(0, rows, issue, 0, unroll=8)


def _row_gather_wait(src_ref, dst_ref, sem, rows):
    pltpu.make_async_copy(src_ref.at[pl.ds(0, rows)], dst_ref, sem).wait()


def _prefetched_gather(starts, waits):
    i = pl.program_id(0)
    slot = i % 2

    @pl.when(i == 0)
    def _():
        starts(0, 0)

    @pl.when(i + 1 < pl.num_programs(0))
    def _():
        starts(i + 1, 1 - slot)

    waits(slot)
    return slot


def _gather_rows_body(idx_ref, src_ref, o_ref, buf, sem, *, rows):
    def starts(step, slot):
        _row_gather_start(idx_ref, step * rows, src_ref, buf.at[slot], sem.at[slot], rows)

    def waits(slot):
        _row_gather_wait(src_ref, buf.at[slot], sem.at[slot], rows)

    slot = _prefetched_gather(starts, waits)
    o_ref[...] = buf[slot].astype(o_ref.dtype)


def _gather_rows(src, idx):
    m = idx.shape[0]
    w = src.shape[1]
    rows = GATHER_ROWS
    assert m % rows == 0 and src.shape[0] >= rows
    return pl.pallas_call(
        functools.partial(_gather_rows_body, rows=rows),
        grid_spec=pltpu.PrefetchScalarGridSpec(
            num_scalar_prefetch=1,
            grid=(m // rows,),
            in_specs=[pl.BlockSpec(memory_space=pl.ANY)],
            out_specs=pl.BlockSpec((rows, w), lambda i, idx_ref: (i, 0)),
            scratch_shapes=[pltpu.VMEM((2, rows, w), src.dtype), pltpu.SemaphoreType.DMA((2,))]),
        out_shape=jax.ShapeDtypeStruct((m, w), BF16),
        compiler_params=_cparams(("arbitrary",), rows * w * (2 * 4 + 2 * 2 + 4)),
        name="moe_gather",
    )(idx, src)


def _gmm_new_weights(te_ref):
    i = pl.program_id(1)
    return jnp.logical_or(i == 0, te_ref[i] != te_ref[jnp.maximum(i - 1, 0)])


def _gmm_swiglu_body(te_ref, nv_ref, x_ref, wg_ref, wu_ref, o_ref, wg_bf, wu_bf):
    valid = pl.program_id(1) < nv_ref[0]

    @pl.when(_gmm_new_weights(te_ref))
    def _():
        wg_bf[...] = wg_ref[...].astype(BF16)
        wu_bf[...] = wu_ref[...].astype(BF16)

    @pl.when(valid)
    def _():
        a = x_ref[...]
        g = _dot(a, wg_bf[...])
        u = _dot(a, wu_bf[...])
        o_ref[...] = (_silu(g) * u).astype(o_ref.dtype)

    @pl.when(jnp.logical_not(valid))
    def _():
        o_ref[...] = jnp.zeros_like(o_ref)


def _gmm_down_body(te_ref, nv_ref, a_ref, w_ref, o_ref, w_bf):
    valid = pl.program_id(1) < nv_ref[0]

    @pl.when(_gmm_new_weights(te_ref))
    def _():
        w_bf[...] = w_ref[...].astype(BF16)

    @pl.when(valid)
    def _():
        o_ref[...] = _dot(a_ref[...], w_bf[...])

    @pl.when(jnp.logical_not(valid))
    def _():
        o_ref[...] = jnp.zeros_like(o_ref)


def _gmm_maps():
    def rows(j, i, te, nv):
        return (jnp.minimum(i, nv[0] - 1), 0)

    def weights(j, i, te, nv):
        return (te[i], 0, j)

    def out(j, i, te, nv):
        return (i, j)

    return rows, weights, out


def _gmm_swiglu(x_sorted, wg, wu, tile_e, n_valid):
    m, k = x_sorted.shape
    n = wg.shape[2]
    tm = MOE_TILE
    tn = _pick(n, (512, 256, 128))
    rows, weights, out = _gmm_maps()
    vmem = 2 * (tm * k * 2 + 2 * k * tn * 4 + tm * tn * 2) + 2 * k * tn * 2 + 3 * tm * tn * 4
    return pl.pallas_call(
        _gmm_swiglu_body,
        grid_spec=pltpu.PrefetchScalarGridSpec(
            num_scalar_prefetch=2,
            grid=(n // tn, m // tm),
            in_specs=[pl.BlockSpec((tm, k), rows),
                      pl.BlockSpec((None, k, tn), weights),
                      pl.BlockSpec((None, k, tn), weights)],
            out_specs=pl.BlockSpec((tm, tn), out),
            scratch_shapes=[pltpu.VMEM((k, tn), BF16), pltpu.VMEM((k, tn), BF16)]),
        out_shape=jax.ShapeDtypeStruct((m, n), BF16),
        compiler_params=_cparams(("arbitrary", "arbitrary"), vmem),
        name="moe_up",
    )(tile_e, n_valid, x_sorted, wg, wu)


def _gmm_down(a_sorted, wd, tile_e, n_valid):
    m, k = a_sorted.shape
    n = wd.shape[2]
    tm = MOE_TILE
    tn = _pick(n, (1024, 512, 256, 128))
    rows, weights, out = _gmm_maps()
    vmem = 2 * (tm * k * 2 + k * tn * 4 + tm * tn * 4) + k * tn * 2 + tm * tn * 4
    return pl.pallas_call(
        _gmm_down_body,
        grid_spec=pltpu.PrefetchScalarGridSpec(
            num_scalar_prefetch=2,
            grid=(n // tn, m // tm),
            in_specs=[pl.BlockSpec((tm, k), rows),
                      pl.BlockSpec((None, k, tn), weights)],
            out_specs=pl.BlockSpec((tm, tn), out),
            scratch_shapes=[pltpu.VMEM((k, tn), BF16)]),
        out_shape=jax.ShapeDtypeStruct((m, n), F32),
        compiler_params=_cparams(("arbitrary", "arbitrary"), vmem),
        name="moe_down",
    )(tile_e, n_valid, a_sorted, wd)


def _moe_combine_body(p1_ref, p2_ref, y_ref, x_ref, gate_ref, route_ref, o_ref, buf1, buf2, sem, *, rows):
    def starts(step, slot):
        _row_gather_start(p1_ref, step * rows, y_ref, buf1.at[slot], sem.at[0, slot], rows)
        _row_gather_start(p2_ref, step * rows, y_ref, buf2.at[slot], sem.at[1, slot], rows)

    def waits(slot):
        _row_gather_wait(y_ref, buf1.at[slot], sem.at[0, slot], rows)
        _row_gather_wait(y_ref, buf2.at[slot], sem.at[1, slot], rows)

    slot = _prefetched_gather(starts, waits)
    route = route_ref[...]
    w1 = route[:, ROUTE_W1:ROUTE_W1 + 1]
    w2 = route[:, ROUTE_W2:ROUTE_W2 + 1]
    o_ref[...] = x_ref[...] + gate_ref[...] * (w1 * buf1[slot] + w2 * buf2[slot])


def _moe_combine(y_sorted, pos1, pos2, x, gate, route, seq):
    t, d = x.shape
    rows = _pick(seq, (128,))
    per_batch = seq // rows
    return pl.pallas_call(
        functools.partial(_moe_combine_body, rows=rows),
        grid_spec=pltpu.PrefetchScalarGridSpec(
            num_scalar_prefetch=2,
            grid=(t // rows,),
            in_specs=[pl.BlockSpec(memory_space=pl.ANY),
                      pl.BlockSpec((rows, d), lambda i, p1, p2: (i, 0)),
                      pl.BlockSpec((None, 1, d), lambda i, p1, p2: (i // per_batch, 0, 0)),
                      pl.BlockSpec((rows, V7X_LANES), lambda i, p1, p2: (i, 0))],
            out_specs=pl.BlockSpec((rows, d), lambda i, p1, p2: (i, 0)),
            scratch_shapes=[pltpu.VMEM((2, rows, d), F32), pltpu.VMEM((2, rows, d), F32),
                            pltpu.SemaphoreType.DMA((2, 2))]),
        out_shape=jax.ShapeDtypeStruct((t, d), F32),
        compiler_params=_cparams(("arbitrary",), 10 * rows * d * 4),
        name="moe_combine",
    )(pos1, pos2, y_sorted, x, gate, route)


_HG_LEVELS = (HG_CHUNK // HG_SUB).bit_length() - 1


def _hgrn_level_masks():
    ti = np.arange(HG_CHUNK)[:, None]
    si = np.arange(HG_CHUNK)[None, :]
    out = []
    for lvl in range(_HG_LEVELS):
        half = HG_SUB << lvl
        blk = 2 * half
        out.append((ti // blk == si // blk) & (ti % blk >= half) & (si % blk < half))
    return jnp.asarray(np.stack(out), F32)


def _hgrn_chunk(q_in, f_in, v, g_in, lb, gain, state_t, tri, ones, lvl_mask_ref):
    c = HG_CHUNK
    fg = lb + (1.0 - lb) * _sigmoid(f_in)
    log_f = jnp.log2(jnp.maximum(fg, MIN_FORGET))
    k = 1.0 - fg
    q = _silu(q_in)
    lf_hi, lf_mid, lf_lo = _split3(log_f)
    b = _dot(tri, lf_hi) + _dot(tri, lf_mid) + _dot(tri, lf_lo)
    b_last = b[c - 1:c, :]

    o = _dot_nt((q * jnp.exp2(b)).astype(BF16), state_t.astype(BF16))

    row = lax.broadcasted_iota(jnp.int32, (c, 1), 0)
    scores = jnp.zeros((c, c), F32)
    for lvl in range(_HG_LEVELS):
        half = HG_SUB << lvl
        blk = 2 * half
        bref = jnp.concatenate(
            [jnp.broadcast_to(b[p * blk + half - 1:p * blk + half, :], (blk, HG_DK)) for p in range(c // blk)],
            axis=0)
        is_q = (row & half) != 0
        e = jnp.exp2(-jnp.abs(b - bref))
        xk = jnp.where(is_q, q, k) * e
        qd = jnp.where(is_q, xk, 0.0).astype(BF16)
        kd = jnp.where(is_q, 0.0, xk).astype(BF16)
        scores = scores + _dot_nt(qd, kd) * lvl_mask_ref[lvl]
    o = o + _dot(scores.astype(BF16), v.astype(BF16))

    nb = c // HG_SUB
    b3 = b.reshape(nb, HG_SUB, HG_DK)
    q3 = q.reshape(nb, HG_SUB, HG_DK)
    k3 = k.reshape(nb, HG_SUB, HG_DK)
    v3 = v.reshape(nb, HG_SUB, HG_DK)
    t_in = lax.broadcasted_iota(jnp.int32, (nb, HG_SUB, HG_DK), 1)
    for s in range(HG_SUB):
        diff = b3 - b3[:, s:s + 1, :]
        dec = jnp.exp2(diff if s == 0 else jnp.where(t_in >= s, diff, NEG_BIG))
        m = (q3 * (k3[:, s:s + 1, :] * dec)).reshape(c, HG_DK)
        r = _dot(m.astype(BF16), ones)
        o = o + r * jnp.broadcast_to(v3[:, s:s + 1, :], (nb, HG_SUB, HG_DK)).reshape(c, HG_DK)

    kdec = (k * jnp.exp2(b_last - b)).astype(BF16)
    new_state_t = state_t * jnp.exp2(b_last) + _dot_tn(v.astype(BF16), kdec)

    ms = jnp.mean(o * o, axis=-1, keepdims=True)
    out = o * lax.rsqrt(ms + NORM_EPS) * gain * _silu(g_in)
    return out, new_state_t


def _hgrn_body(q_ref, f_ref, i_ref, g_ref, lbraw_ref, gain_ref, lvl_mask_ref, o_ref, state_ref,
               *, layer, n_chunks, group):
    @pl.when(pl.program_id(2) == 0)
    def _():
        state_ref[...] = jnp.zeros_like(state_ref)

    lbr = lbraw_ref[...]
    ex = jnp.exp(lbr - jnp.max(lbr, axis=0, keepdims=True))
    soft = ex / jnp.sum(ex, axis=0, keepdims=True)
    lb = jnp.zeros((1, group * HG_DK), F32)
    for j in range(1, layer + 1):
        lb = lb + soft[j:j + 1, :]
    gain = gain_ref[...]
    c = HG_CHUNK
    tri = (lax.broadcasted_iota(jnp.int32, (c, c), 0) >= lax.broadcasted_iota(jnp.int32, (c, c), 1)).astype(BF16)
    ones = jnp.ones((HG_DK, HG_DK), BF16)

    def chunk(ci, carry):
        sl = pl.ds(pl.multiple_of(ci * c, c), c)
        for hh in range(group):
            cs = slice(hh * HG_DK, (hh + 1) * HG_DK)
            out, new_state = _hgrn_chunk(q_ref[sl, cs].astype(F32), f_ref[sl, cs].astype(F32),
                                         i_ref[sl, cs].astype(F32), g_ref[sl, cs].astype(F32),
                                         lb[:, cs], gain, state_ref[hh], tri, ones, lvl_mask_ref)
            o_ref[sl, cs] = out.astype(o_ref.dtype)
            state_ref[hh] = new_state
        return carry

    lax.fori_loop(0, n_chunks, chunk, 0)


def _hgrn(proj, lb_raw, out_gain, layer, n_batch, seq, width):
    heads = width // HG_DK
    group = HG_GROUP if heads % HG_GROUP == 0 else 1
    hgroups = heads // group
    gw = group * HG_DK
    lc = _pick(seq, (512, 256, 128))
    per_batch = seq // lc
    depth = lb_raw.shape[0]

    def col(off):
        return pl.BlockSpec((lc, gw), lambda b, h, i: (b * per_batch + i, off * hgroups + h))

    return pl.pallas_call(
        functools.partial(_hgrn_body, layer=layer, n_chunks=lc // HG_CHUNK, group=group),
        grid=(n_batch, hgroups, per_batch),
        in_specs=[col(0), col(1), col(2), col(3),
                  pl.BlockSpec((depth, gw), lambda b, h, i: (0, h)),
                  pl.BlockSpec((1, HG_DK), lambda b, h, i: (0, 0)),
                  pl.BlockSpec((_HG_LEVELS, HG_CHUNK, HG_CHUNK), lambda b, h, i: (0, 0, 0))],
        out_specs=pl.BlockSpec((lc, gw), lambda b, h, i: (b * per_batch + i, h)),
        out_shape=jax.ShapeDtypeStruct((n_batch * seq, width), BF16),
        scratch_shapes=[pltpu.VMEM((group, HG_DK, HG_DK), F32)],
        compiler_params=_cparams(("parallel", "parallel", "arbitrary"), 16 * 1024 * 1024),
        name="hgrn2",
    )(proj, proj, proj, proj, lb_raw, out_gain.reshape(1, HG_DK), _hgrn_level_masks())


def _pool_body(u_ref, halo_ref, w_ref, scale_ref, o_ref, ext_ref, *, ts, gdim):
    i = pl.program_id(1)
    halo = halo_ref[...].astype(F32)
    ext_ref[0:POOL_HALO, :] = jnp.where(i > 0, halo, 0.0)
    ext_ref[POOL_HALO:, :] = u_ref[...].astype(F32)
    pos = (i * ts + lax.broadcasted_iota(jnp.int32, (ts, 1), 0) + 1).astype(F32)
    for g, win in enumerate(POOL_WINDOWS):
        cs = slice(g * gdim, (g + 1) * gdim)
        acc = ext_ref[POOL_HALO:, cs]
        for j in range(1, win):
            acc = acc + ext_ref[POOL_HALO - j:POOL_HALO - j + ts, cs]
        pooled = acc / jnp.minimum(pos, float(win)) - ext_ref[POOL_HALO:, cs]
        y = _dot(pooled.astype(BF16), w_ref[g])
        o_ref[:, cs] = (y * scale_ref[:, cs]).astype(o_ref.dtype)


def _pool(proj, col_off, pool_w, pool_scale, n_batch, seq, width):
    groups = len(POOL_WINDOWS)
    gdim = width // groups
    assert gdim % V7X_LANES == 0 and col_off % width == 0 and max(POOL_WINDOWS) <= POOL_HALO
    ts = _pick(seq, (512, 256, 128))
    per_batch = seq // ts
    cb = col_off // width
    hb = ts // POOL_HALO
    return pl.pallas_call(
        functools.partial(_pool_body, ts=ts, gdim=gdim),
        grid=(n_batch, per_batch),
        in_specs=[pl.BlockSpec((ts, width), lambda b, i: (b * per_batch + i, cb)),
                  pl.BlockSpec((POOL_HALO, width),
                               lambda b, i: (jnp.maximum((b * per_batch + i) * hb - 1, 0), cb)),
                  pl.BlockSpec((groups, gdim, gdim), lambda b, i: (0, 0, 0)),
                  pl.BlockSpec((1, width), lambda b, i: (0, 0))],
        out_specs=pl.BlockSpec((ts, width), lambda b, i: (b * per_batch + i, 0)),
        out_shape=jax.ShapeDtypeStruct((n_batch * seq, width), BF16),
        scratch_shapes=[pltpu.VMEM((ts + POOL_HALO, width), F32)],
        compiler_params=_cparams(("parallel", "parallel"), 16 * 1024 * 1024),
        name="pool",
    )(proj, proj, pool_w, pool_scale.reshape(1, width))


def _rope_table_body(pos_ref, cos_ref, sin_ref):
    pos = pos_ref[...].astype(F32)
    lane = lax.broadcasted_iota(jnp.int32, (1, V7X_LANES), 1)
    j = lane % MLA_ROPE
    fidx = (j % (MLA_ROPE // 2)).astype(F32)
    inv_freq = jnp.exp(fidx * (-2.0 / MLA_ROPE * math.log(ROPE_THETA)))
    ang = pos * inv_freq
    cos_ref[...] = jnp.cos(ang)
    sin_ref[...] = jnp.where(j < MLA_ROPE // 2, -1.0, 1.0) * jnp.sin(ang)


def _rope_table(positions):
    t = positions.size
    ts = _pick(t, (512, 256, 128))
    return pl.pallas_call(
        _rope_table_body,
        grid=(t // ts,),
        in_specs=[pl.BlockSpec((ts, 1), lambda i: (i, 0))],
        out_specs=[pl.BlockSpec((ts, V7X_LANES), lambda i: (i, 0))] * 2,
        out_shape=[jax.ShapeDtypeStruct((t, V7X_LANES), F32)] * 2,
        compiler_params=_cparams(("parallel",), 4 * 1024 * 1024),
        name="rope_table",
    )(positions.reshape(t, 1))


def _rms(x, gain):
    return x * lax.rsqrt(jnp.mean(x * x, axis=-1, keepdims=True) + NORM_EPS) * gain


def _prep_q_body(cq_ref, qn_ref, w_ref, gn_ref, gr_ref, grs_ref, seg_ref, cos_ref, sin_ref, o_ref,
                 *, heads, sm_scale):
    hn = heads * MLA_NOPE
    hr = heads * MLA_ROPE
    hq = _rms(cq_ref[...].astype(F32), qn_ref[...]).astype(BF16)
    y = _dot(hq, w_ref[...])
    yr = y[:, hn:hn + hr]
    ys = y[:, hn + hr:]
    sq_hi, sq_lo = _split2(yr * yr)
    seg = seg_ref[...]
    ss = _dot(sq_hi, seg) + _dot(sq_lo, seg)
    inv = lax.rsqrt(ss * (1.0 / MLA_ROPE) + NORM_EPS)
    reps = hr // V7X_LANES
    cosf = jnp.concatenate([cos_ref[...]] * reps, axis=1)
    sinf = jnp.concatenate([sin_ref[...]] * reps, axis=1)
    qr = (yr * inv * gr_ref[...]) * cosf + (ys * inv * grs_ref[...]) * sinf
    gn = gn_ref[...] * sm_scale
    for h in range(heads):
        qn = _rms(y[:, h * MLA_NOPE:(h + 1) * MLA_NOPE], gn)
        o_ref[h, :, 0:MLA_NOPE] = qn.astype(o_ref.dtype)
        o_ref[h, :, MLA_NOPE:] = (qr[:, h * MLA_ROPE:(h + 1) * MLA_ROPE] * sm_scale).astype(o_ref.dtype)


def _prep_kv_body(ckv_ref, kpe_ref, kvn_ref, w_ref, gn_ref, gr2_ref, cos_ref, sin_ref, k_ref, v_ref, *, heads):
    hk = _rms(ckv_ref[...].astype(F32), kvn_ref[...]).astype(BF16)
    y = _dot(hk, w_ref[...])
    kp = kpe_ref[...].astype(F32)
    kpe = kp[:, :MLA_ROPE]
    inv = lax.rsqrt(jnp.mean(kpe * kpe, axis=-1, keepdims=True) + NORM_EPS)
    kn = kp * inv * gr2_ref[...]
    kr = (kn[:, :MLA_ROPE] * cos_ref[:, :MLA_ROPE] + kn[:, MLA_ROPE:] * sin_ref[:, :MLA_ROPE]).astype(k_ref.dtype)
    per = MLA_NOPE + MLA_V
    for h in range(heads):
        k_ref[h, :, 0:MLA_NOPE] = _rms(y[:, h * per:h * per + MLA_NOPE], gn_ref[...]).astype(k_ref.dtype)
        k_ref[h, :, MLA_NOPE:] = kr
        v_ref[h, 0, :MLA_V, :] = y[:, h * per + MLA_NOPE:(h + 1) * per].T.astype(v_ref.dtype)
        v_ref[h, 0, MLA_V:, :] = jnp.ones((V7X_BF16_ROWS, y.shape[0]), v_ref.dtype)


def _flash_body(q_ref, k_ref, vt_ref, o_ref, acc_ref, st_a, st_b, *, tile, group):
    qi = pl.program_id(2)
    acc_ref[...] = jnp.zeros_like(acc_ref)

    def scores(j, dst):
        ks = pl.ds(pl.multiple_of(j * tile, tile), tile)
        for g in range(group):
            dst[g] = _dot_nt(k_ref[g, ks, :], q_ref[g])

    def consume(j, src, ms, masked):
        m_new = []
        for g in range(group):
            st = src[g]
            if masked:
                kidx = lax.broadcasted_iota(jnp.int32, (tile, tile), 0)
                qidx = lax.broadcasted_iota(jnp.int32, (tile, tile), 1)
                st = jnp.where(kidx <= qidx, st, NEG_BIG)
            m = jnp.maximum(ms[g], jnp.max(st, axis=0, keepdims=True))
            p = jnp.exp2(st - m).astype(BF16)
            acc_ref[g] = jnp.exp2(ms[g] - m) * acc_ref[g] + _dot(vt_ref[g, j], p)
            m_new.append(m)
        return tuple(m_new)

    def finish():
        for g in range(group):
            o = acc_ref[g, :MLA_V, :] / acc_ref[g, MLA_V:MLA_V + 1, :]
            o_ref[:, g * MLA_V:(g + 1) * MLA_V] = o.T.astype(o_ref.dtype)

    scores(0, st_a)

    def pair(pi, ms):
        j = 2 * pi
        scores(j + 1, st_b)
        ms = consume(j, st_a, ms, False)
        scores(j + 2, st_a)
        return consume(j + 1, st_b, ms, False)

    m0 = tuple(jnp.full((1, tile), NEG_BIG, F32) for _ in range(group))
    ms = lax.fori_loop(0, qi // 2, pair, m0)
    odd = qi % 2 == 1

    @pl.when(odd)
    def _():
        scores(qi, st_b)
        consume(qi, st_b, consume(qi - 1, st_a, ms, False), True)
        finish()

    @pl.when(jnp.logical_not(odd))
    def _():
        consume(qi, st_a, ms, True)
        finish()


def _mla(proj_c, cos_t, sin_t, q_norm, w_q, kv_norm, w_kv, g_qn, g_qr, g_kn, g_kr,
         n_batch, seq, heads, q_rank, kv_rank):
    t = n_batch * seq
    half = MLA_ROPE // 2
    hn, hr = heads * MLA_NOPE, heads * MLA_ROPE
    dq = MLA_NOPE + MLA_ROPE
    sm_scale = float(dq) ** -0.5 * math.log2(math.e)
    assert q_rank % kv_rank == 0 and (q_rank + kv_rank) % V7X_LANES == 0 and hr % V7X_LANES == 0
    tile = ATTN_TILE
    assert seq % tile == 0
    tm = tile
    vrows = MLA_V + V7X_BF16_ROWS

    def swap(g):
        return jnp.concatenate([g[half:], g[:half]])

    gr = jnp.tile(g_qr, heads).reshape(1, hr)
    grs = jnp.tile(swap(g_qr), heads).reshape(1, hr)
    lane = jnp.arange(hr) // MLA_ROPE
    seg = (lane[:, None] == lane[None, :]).astype(BF16)
    row = lambda i: (i, 0)
    const2 = lambda i: (0, 0)
    qcat = pl.pallas_call(
        functools.partial(_prep_q_body, heads=heads, sm_scale=sm_scale),
        grid=(t // tm,),
        in_specs=[pl.BlockSpec((tm, q_rank), row),
                  pl.BlockSpec((1, q_rank), const2),
                  pl.BlockSpec((q_rank, hn + 2 * hr), const2),
                  pl.BlockSpec((1, MLA_NOPE), const2),
                  pl.BlockSpec((1, hr), const2),
                  pl.BlockSpec((1, hr), const2),
                  pl.BlockSpec((hr, hr), const2),
                  pl.BlockSpec((tm, V7X_LANES), row),
                  pl.BlockSpec((tm, V7X_LANES), row)],
        out_specs=pl.BlockSpec((heads, tm, dq), lambda i: (0, i, 0)),
        out_shape=jax.ShapeDtypeStruct((heads, t, dq), BF16),
        compiler_params=_cparams(("parallel",), 40 * 1024 * 1024),
        name="mla_prep_q",
    )(proj_c, q_norm.reshape(1, q_rank), w_q, g_qn.reshape(1, MLA_NOPE), gr, grs, seg, cos_t, sin_t)

    gr2 = jnp.concatenate([g_kr, swap(g_kr)]).reshape(1, 2 * MLA_ROPE)
    kcat, vt = pl.pallas_call(
        functools.partial(_prep_kv_body, heads=heads),
        grid=(t // tm,),
        in_specs=[pl.BlockSpec((tm, kv_rank), lambda i: (i, q_rank // kv_rank)),
                  pl.BlockSpec((tm, V7X_LANES), lambda i: (i, (q_rank + kv_rank) // V7X_LANES)),
                  pl.BlockSpec((1, kv_rank), const2),
                  pl.BlockSpec((kv_rank, heads * (MLA_NOPE + MLA_V)), const2),
                  pl.BlockSpec((1, MLA_NOPE), const2),
                  pl.BlockSpec((1, 2 * MLA_ROPE), const2),
                  pl.BlockSpec((tm, V7X_LANES), row),
                  pl.BlockSpec((tm, V7X_LANES), row)],
        out_specs=[pl.BlockSpec((heads, tm, dq), lambda i: (0, i, 0)),
                   pl.BlockSpec((heads, 1, vrows, tile), lambda i: (0, i, 0, 0))],
        out_shape=[jax.ShapeDtypeStruct((heads, t, dq), BF16),
                   jax.ShapeDtypeStruct((heads, t // tile, vrows, tile), BF16)],
        compiler_params=_cparams(("parallel",), 40 * 1024 * 1024),
        name="mla_prep_kv",
    )(proj_c, proj_c, kv_norm.reshape(1, kv_rank), w_kv, g_kn.reshape(1, MLA_NOPE), gr2, cos_t, sin_t)

    nq = seq // tile
    group = ATTN_GROUP if heads % ATTN_GROUP == 0 else 1
    return pl.pallas_call(
        functools.partial(_flash_body, tile=tile, group=group),
        grid=(heads // group, n_batch, nq),
        in_specs=[pl.BlockSpec((group, tile, dq), lambda h, b, i: (h, b * nq + i, 0)),
                  pl.BlockSpec((group, seq, dq), lambda h, b, i: (h, b, 0)),
                  pl.BlockSpec((group, nq, vrows, tile), lambda h, b, i: (h, b, 0, 0))],
        out_specs=pl.BlockSpec((tile, group * MLA_V), lambda h, b, i: (b * nq + i, h)),
        out_shape=jax.ShapeDtypeStruct((t, heads * MLA_V), BF16),
        scratch_shapes=[pltpu.VMEM((group, vrows, tile), F32),
                        pltpu.VMEM((group, tile, tile), F32), pltpu.VMEM((group, tile, tile), F32)],
        compiler_params=_cparams(("parallel", "parallel", "arbitrary"), 48 * 1024 * 1024),
        name="mla_flash",
    )(qcat, kcat, vt)


def kernel(x, c, positions, ada_w, ada_b, ada_layer, mix_norm, ffn_norm, w_in, hgrn_lower_bounds,
           hgrn_out_norm, pool_w, pool_scale, mla_q_norm, mla_w_uq, mla_kv_norm, mla_w_ukv,
           mla_qk_norm_q_nope, mla_qk_norm_q_rope, mla_qk_norm_k_nope, mla_qk_norm_k_rope,
           w_branch_a, w_branch_b, w_branch_c, w_o, ffn_w_gate, ffn_w_up, ffn_w_down,
           moe_router, moe_w_gate, moe_w_up, moe_w_down):
    n_batch, seq, d = x.shape
    depth = w_in.shape[0]
    t = n_batch * seq
    hg_width = hgrn_lower_bounds.shape[1]
    pool_width = pool_scale.shape[1]
    q_rank = mla_q_norm.shape[1]
    kv_rank = mla_kv_norm.shape[1]
    heads = mla_w_ukv.shape[2] // (MLA_NOPE + MLA_V)
    half = MLA_ROPE // 2
    n_a = 4 * hg_width + pool_width
    n_c = q_rank + kv_rank + MLA_ROPE
    assert w_in.shape[2] == n_a + n_c + 3 * d

    mod = _ada(c, ada_w, ada_b, ada_layer)
    cos_t, sin_t = _rope_table(positions)
    xf = x.reshape(t, d)

    for l in range(depth):
        mod_l = mod[l]
        gate1 = mod_l[:, 2:3, :]
        gate2 = mod_l[:, 5:6, :]

        w_l = w_in[l]
        w_a = w_l[:, :n_a].astype(BF16)
        kpe_w = w_l[:, n_a + q_rank + kv_rank:n_a + n_c]
        w_c = jnp.concatenate([w_l[:, n_a:n_a + n_c], kpe_w[:, half:], kpe_w[:, :half]], axis=1).astype(BF16)
        w_g = w_l[:, n_a + n_c:].astype(BF16)
        wq = mla_w_uq[l].reshape(q_rank, heads, MLA_NOPE + MLA_ROPE)
        wq_r = wq[:, :, MLA_NOPE:]
        w_q = jnp.concatenate([
            wq[:, :, :MLA_NOPE].reshape(q_rank, heads * MLA_NOPE),
            wq_r.reshape(q_rank, heads * MLA_ROPE),
            jnp.concatenate([wq_r[:, :, half:], wq_r[:, :, :half]], axis=2).reshape(q_rank, heads * MLA_ROPE),
        ], axis=1).astype(BF16)

        h = _modulate(xf.reshape(n_batch, seq, d), mix_norm[l], mod_l, 0, 1)
        proj_a = _mm_cast(h, w_a, name="proj_a")
        proj_c = _mm_cast(h, w_c, name="proj_c")
        gates = _mm_cast(h, w_g, name="proj_gates")

        o_a = _hgrn(proj_a, hgrn_lower_bounds, hgrn_out_norm[l], l, n_batch, seq, hg_width)
        o_b = _pool(proj_a, 4 * hg_width, pool_w[l].astype(BF16), pool_scale[l], n_batch, seq, pool_width)
        o_c = _mla(proj_c, cos_t, sin_t, mla_q_norm[l], w_q, mla_kv_norm[l], mla_w_ukv[l].astype(BF16),
                   mla_qk_norm_q_nope[l], mla_qk_norm_q_rope[l], mla_qk_norm_k_nope[l], mla_qk_norm_k_rope[l],
                   n_batch, seq, heads, q_rank, kv_rank)
        merged = _merge(o_a, o_b, o_c, w_branch_a[l].astype(BF16), w_branch_b[l].astype(BF16),
                        w_branch_c[l].astype(BF16), gates)
        xf = _mm_residual(merged, w_o[l].astype(BF16), xf, gate1, seq, name="out_proj")

        j = l // 2
        if l % 2 == 0:
            h = _modulate(xf.reshape(n_batch, seq, d), ffn_norm[l], mod_l, 3, 4)
            act = _mm_swiglu(h, ffn_w_gate[j].astype(BF16), ffn_w_up[j].astype(BF16), name="ffn_up")
            xf = _mm_residual(act, ffn_w_down[j].astype(BF16), xf, gate2, seq, name="ffn_down")
        else:
            n_experts = moe_w_gate.shape[1]
            h, route = _modulate(xf.reshape(n_batch, seq, d), ffn_norm[l], mod_l, 3, 4, router=moe_router[j])
            row_token, tile_e, n_valid, pos1, pos2 = _route_metadata(route, n_experts)
            h_sorted = _gather_rows(h, row_token)
            act = _gmm_swiglu(h_sorted, moe_w_gate[j], moe_w_up[j], tile_e, n_valid)
            y_sorted = _gmm_down(act, moe_w_down[j], tile_e, n_valid)
            xf = _moe_combine(y_sorted, pos1, pos2, xf, gate2, route, seq)
    return xf.reshape(n_batch, seq, d)
```

```python
import functools
import math

import numpy as np
import jax
import jax.numpy as jnp
from jax import lax
from jax.experimental import pallas as pl
from jax.experimental.pallas import tpu as pltpu

F32 = jnp.float32
BF16 = jnp.bfloat16

HG_DK = 128
POOL_WINDOWS = (2, 4, 8, 16)
MLA_NOPE = 128
MLA_ROPE = 64
MLA_V = 128
ROPE_THETA = 10000.0
MIN_FORGET = 1e-30
NORM_EPS = 1e-6
N_MOD = 6
TOP_K = 2
NEG_BIG = -1e30

V7X_LANES = 128
V7X_SUBLANES = 8
V7X_BF16_ROWS = 16
V7X_VMEM_BYTES = 64 * 1024 * 1024
VMEM_CAP = V7X_VMEM_BYTES - 8 * 1024 * 1024

HG_CHUNK = 128
HG_SUB = 8
HG_GROUP = 8
POOL_HALO = 16
ADA_CHUNK = 512
ATTN_TILE = 1024
ATTN_GROUP = 2
ROUTE_E1, ROUTE_E2, ROUTE_W1, ROUTE_W2 = 0, 1, 2, 3
MOE_TILE = 512
GATHER_ROWS = 512


def _pick(n, prefs):
    for p in prefs:
        if n % p == 0:
            return p
    raise ValueError(f"no tile in {prefs} divides {n}")


def _cparams(sem, vmem_bytes):
    limit = int(min(VMEM_CAP, max(32 * 1024 * 1024, vmem_bytes * 5 // 4)))
    return pltpu.CompilerParams(dimension_semantics=sem, vmem_limit_bytes=limit)


def _sigmoid(x):
    return 1.0 / (1.0 + jnp.exp(-x))


def _silu(x):
    return x * _sigmoid(x)


def _dot(a, b):
    return jnp.dot(a, b, preferred_element_type=F32)


def _dot_nt(a, b):
    return lax.dot_general(a, b, (((1,), (1,)), ((), ())), preferred_element_type=F32)


def _dot_tn(a, b):
    return lax.dot_general(a, b, (((0,), (0,)), ((), ())), preferred_element_type=F32)


def _split3(x):
    hi = x.astype(BF16)
    r1 = x - hi.astype(F32)
    mid = r1.astype(BF16)
    lo = (r1 - mid.astype(F32)).astype(BF16)
    return hi, mid, lo


def _split2(x):
    hi = x.astype(BF16)
    lo = (x - hi.astype(F32)).astype(BF16)
    return hi, lo


def _ada_body(ct_ref, w_ref, b_ref, lay_ref, o_ref, acc_ref, *, n_batch, depth):
    kk = pl.program_id(1)
    tk, tn = w_ref.shape
    sub = V7X_SUBLANES

    @pl.when(kk == 0)
    def _():
        acc_ref[...] = jnp.zeros_like(acc_ref)

    s = _silu(ct_ref[pl.ds(pl.multiple_of(kk * tk, tk), tk), :])
    for c0 in range(0, tn, ADA_CHUNK):
        w = w_ref[:, c0:c0 + ADA_CHUNK]
        for b in range(n_batch):
            prod = (w * s[:, b:b + 1]).reshape(tk // sub, sub, ADA_CHUNK)
            acc_ref[b, :, c0:c0 + ADA_CHUNK] += jnp.sum(prod, axis=0)

    @pl.when(kk == pl.num_programs(1) - 1)
    def _():
        for b in range(n_batch):
            r = jnp.sum(acc_ref[b], axis=0, keepdims=True) + b_ref[...]
            for l in range(depth):
                o_ref[l, b:b + 1, :] = r + lay_ref[l:l + 1, :]


def _ada(c, ada_w, ada_b, ada_layer):
    n_batch, d = c.shape
    depth = ada_layer.shape[0]
    n = ada_w.shape[1]
    tn = n
    tk = _pick(d, (64, 32, 16, 8))
    assert tn % ADA_CHUNK == 0
    ct = c.T
    lay = ada_layer.reshape(depth, n)
    out = pl.pallas_call(
        functools.partial(_ada_body, n_batch=n_batch, depth=depth),
        scratch_shapes=[pltpu.VMEM((n_batch, V7X_SUBLANES, tn), F32)],
        grid=(n // tn, d // tk),
        in_specs=[
            pl.BlockSpec((d, n_batch), lambda j, k: (0, 0)),
            pl.BlockSpec((tk, tn), lambda j, k: (k, j)),
            pl.BlockSpec((1, tn), lambda j, k: (0, j)),
            pl.BlockSpec((depth, tn), lambda j, k: (0, j)),
        ],
        out_specs=pl.BlockSpec((depth, n_batch, tn), lambda j, k: (0, 0, j)),
        out_shape=jax.ShapeDtypeStruct((depth, n_batch, n), F32),
        compiler_params=_cparams(("parallel", "arbitrary"), 4 * tk * tn * 4),
        name="ada",
    )(ct, ada_w, ada_b.reshape(1, n), lay)
    return out.reshape(depth, n_batch, N_MOD, d)


def _modulated(x_ref, g_ref, mod_ref, shift_idx, scale_idx):
    x = x_ref[...]
    ms = jnp.mean(x * x, axis=-1, keepdims=True)
    y = x * lax.rsqrt(ms + NORM_EPS) * g_ref[...]
    return y * (1.0 + mod_ref[scale_idx:scale_idx + 1, :]) + mod_ref[shift_idx:shift_idx + 1, :]


def _modulate_body(x_ref, g_ref, mod_ref, o_ref, *, shift_idx, scale_idx):
    o_ref[...] = _modulated(x_ref, g_ref, mod_ref, shift_idx, scale_idx).astype(o_ref.dtype)


def _modulate_route_body(x_ref, g_ref, mod_ref, r_ref, o_ref, route_ref, *, shift_idx, scale_idx, n_experts):
    h = _modulated(x_ref, g_ref, mod_ref, shift_idx, scale_idx)
    o_ref[...] = h
    h_hi, h_mid, h_lo = _split3(h)
    r = r_ref[...]
    r_hi, r_mid, r_lo = _split3(r)
    logits = (_dot(h_hi, r_hi) + _dot(h_hi, r_mid) + _dot(h_mid, r_hi)
              + _dot(h_hi, r_lo) + _dot(h_mid, r_mid) + _dot(h_lo, r_hi))
    lane = lax.broadcasted_iota(jnp.int32, logits.shape, 1).astype(F32)
    lg = jnp.where(lane < n_experts, logits, -jnp.inf)
    m1 = jnp.max(lg, axis=-1, keepdims=True)
    i1 = jnp.min(jnp.where(lg == m1, lane, float(V7X_LANES)), axis=-1, keepdims=True)
    lg2 = jnp.where(lane == i1, -jnp.inf, lg)
    m2 = jnp.max(lg2, axis=-1, keepdims=True)
    i2 = jnp.min(jnp.where(lg2 == m2, lane, float(V7X_LANES)), axis=-1, keepdims=True)
    e2 = jnp.exp(m2 - m1)
    w1 = 1.0 / (1.0 + e2)
    w2 = e2 / (1.0 + e2)
    route_ref[...] = (jnp.where(lane == ROUTE_E1, i1, 0.0) + jnp.where(lane == ROUTE_E2, i2, 0.0)
                      + jnp.where(lane == ROUTE_W1, w1, 0.0) + jnp.where(lane == ROUTE_W2, w2, 0.0))


def _modulate(x3, gain, mod_l, shift_idx, scale_idx, router=None):
    n_batch, seq, d = x3.shape
    ts = _pick(seq, (512, 256, 128))
    grid = (n_batch, seq // ts)
    x_spec = pl.BlockSpec((None, ts, d), lambda b, i: (b, i, 0))
    g_spec = pl.BlockSpec((1, d), lambda b, i: (0, 0))
    mod_spec = pl.BlockSpec((None, N_MOD, d), lambda b, i: (b, 0, 0))
    h_spec = pl.BlockSpec((None, ts, d), lambda b, i: (b, i, 0))
    vmem = 2 * ts * d * (4 + 2) + 4 * ts * d * 4
    if router is None:
        h = pl.pallas_call(
            functools.partial(_modulate_body, shift_idx=shift_idx, scale_idx=scale_idx),
            grid=grid,
            in_specs=[x_spec, g_spec, mod_spec],
            out_specs=h_spec,
            out_shape=jax.ShapeDtypeStruct((n_batch, seq, d), BF16),
            compiler_params=_cparams(("parallel", "parallel"), vmem),
            name="modulate",
        )(x3, gain.reshape(1, d), mod_l)
        return h.reshape(n_batch * seq, d)
    n_experts = router.shape[1]
    assert n_experts <= V7X_LANES
    r_pad = jnp.zeros((d, V7X_LANES), F32).at[:, :n_experts].set(router)
    h, route = pl.pallas_call(
        functools.partial(_modulate_route_body, shift_idx=shift_idx, scale_idx=scale_idx, n_experts=n_experts),
        grid=grid,
        in_specs=[x_spec, g_spec, mod_spec, pl.BlockSpec((d, V7X_LANES), lambda b, i: (0, 0))],
        out_specs=[h_spec, pl.BlockSpec((None, ts, V7X_LANES), lambda b, i: (b, i, 0))],
        out_shape=[jax.ShapeDtypeStruct((n_batch, seq, d), F32),
                   jax.ShapeDtypeStruct((n_batch, seq, V7X_LANES), F32)],
        compiler_params=_cparams(("parallel", "parallel"), vmem + 2 * ts * d * 2 + 6 * ts * d * 2),
        name="modulate_route",
    )(x3, gain.reshape(1, d), mod_l, r_pad)
    return h.reshape(n_batch * seq, d), route.reshape(n_batch * seq, V7X_LANES)


def _mm_cast_body(a_ref, w_ref, o_ref):
    o_ref[...] = _dot(a_ref[...], w_ref[...]).astype(o_ref.dtype)


def _mm_cast(a, w, out_dtype=BF16, name="mm"):
    m, k = a.shape
    n = w.shape[1]
    tn = n if n <= 2048 else _pick(n, (1024, 512, 256, 128))
    need = lambda tm_: 2 * (tm_ * k * 2 + k * tn * 2 + tm_ * tn * 2) + tm_ * tn * 4
    tm = next(t_ for t_ in (1024, 512, 256, 128) if m % t_ == 0 and need(t_) * 5 // 4 <= VMEM_CAP)
    vmem = need(tm)
    return pl.pallas_call(
        _mm_cast_body,
        grid=(m // tm, n // tn),
        in_specs=[pl.BlockSpec((tm, k), lambda i, j: (i, 0)),
                  pl.BlockSpec((k, tn), lambda i, j: (0, j))],
        out_specs=pl.BlockSpec((tm, tn), lambda i, j: (i, j)),
        out_shape=jax.ShapeDtypeStruct((m, n), out_dtype),
        compiler_params=_cparams(("parallel", "arbitrary"), vmem),
        name=name,
    )(a, w)


def _mm_swiglu_body(a_ref, wg_ref, wu_ref, o_ref):
    a = a_ref[...]
    g = _dot(a, wg_ref[...])
    u = _dot(a, wu_ref[...])
    o_ref[...] = (_silu(g) * u).astype(o_ref.dtype)


def _mm_swiglu(a, wg, wu, name="swiglu"):
    m, k = a.shape
    n = wg.shape[1]
    tn = _pick(n, (512, 256, 128))
    need = lambda tm_: 2 * (tm_ * k * 2 + 2 * k * tn * 2 + tm_ * tn * 2) + 3 * tm_ * tn * 4
    tm = next(t_ for t_ in (2048, 1024, 512, 256, 128) if m % t_ == 0 and need(t_) * 10 // 9 <= VMEM_CAP)
    vmem = need(tm)
    return pl.pallas_call(
        _mm_swiglu_body,
        grid=(m // tm, n // tn),
        in_specs=[pl.BlockSpec((tm, k), lambda i, j: (i, 0)),
                  pl.BlockSpec((k, tn), lambda i, j: (0, j)),
                  pl.BlockSpec((k, tn), lambda i, j: (0, j))],
        out_specs=pl.BlockSpec((tm, tn), lambda i, j: (i, j)),
        out_shape=jax.ShapeDtypeStruct((m, n), BF16),
        compiler_params=_cparams(("parallel", "arbitrary"), vmem),
        name=name,
    )(a, wg, wu)


def _mm_residual_body(a_ref, w_ref, x_ref, gate_ref, o_ref, *, nk):
    scale = gate_ref[...]
    part = _dot(a_ref[...], w_ref[...])
    if nk == 1:
        o_ref[...] = x_ref[...] + scale * part
    else:
        kk = pl.program_id(2)

        @pl.when(kk == 0)
        def _():
            o_ref[...] = part

        @pl.when(jnp.logical_and(kk > 0, kk < nk - 1))
        def _():
            o_ref[...] += part

        @pl.when(kk == nk - 1)
        def _():
            o_ref[...] = x_ref[...] + scale * (o_ref[...] + part)


def _mm_residual(a, w, x, gate, seq, name="mm_res"):
    m, k = a.shape
    n = w.shape[1]
    tm = _pick(seq, (1024, 512, 256, 128))
    tn = _pick(n, (1024, 512, 256, 128))
    if k <= 4096:
        tk = k
    else:
        tk = next(t for t in range(4096 // V7X_LANES * V7X_LANES, 0, -V7X_LANES) if k % t == 0)
        if tk < 512:
            tk = next(t for t in range(k // 2 // V7X_LANES * V7X_LANES, 0, -V7X_LANES) if k % t == 0)
    nk = k // tk
    if tk > 4096:
        tn = _pick(n, (512, 256, 128))
    per_batch = seq // tm
    in_specs = [pl.BlockSpec((tm, tk), lambda i, j, kk: (i, kk)),
                pl.BlockSpec((tk, tn), lambda i, j, kk: (kk, j)),
                pl.BlockSpec((tm, tn), lambda i, j, kk: (i, j)),
                pl.BlockSpec((None, 1, tn), lambda i, j, kk: (i // per_batch, 0, j))]
    vmem = 2 * (tm * tk * 2 + tk * tn * 2 + 2 * tm * tn * 4) + 2 * tm * tn * 4
    return pl.pallas_call(
        functools.partial(_mm_residual_body, nk=nk),
        grid=(m // tm, n // tn, nk),
        in_specs=in_specs,
        out_specs=pl.BlockSpec((tm, tn), lambda i, j, kk: (i, j)),
        out_shape=jax.ShapeDtypeStruct((m, n), F32),
        compiler_params=_cparams(("parallel", "parallel", "arbitrary"), vmem),
        name=name,
    )(a, w, x, gate)


def _merge_body(a_ref, b_ref, c_ref, wa_ref, wb_ref, wc_ref, ga_ref, gb_ref, gc_ref, o_ref):
    ya = _dot(a_ref[...], wa_ref[...])
    yb = _dot(b_ref[...], wb_ref[...])
    yc = _dot(c_ref[...], wc_ref[...])
    out = (_sigmoid(ga_ref[...].astype(F32)) * ya + _sigmoid(gb_ref[...].astype(F32)) * yb
           + _sigmoid(gc_ref[...].astype(F32)) * yc)
    o_ref[...] = out.astype(o_ref.dtype)


def _merge(o_a, o_b, o_c, w_a, w_b, w_c, gates):
    m = o_a.shape[0]
    d = w_a.shape[1]
    tm = _pick(m, (1024, 512, 256, 128))
    tn = _pick(d, (512, 256, 128))
    nj = d // tn
    ka, kb, kc = o_a.shape[1], o_b.shape[1], o_c.shape[1]
    vmem = 2 * 2 * (tm * (ka + kb + kc) + (ka + kb + kc) * tn + 4 * tm * tn) + 6 * tm * tn * 4
    return pl.pallas_call(
        _merge_body,
        grid=(m // tm, nj),
        in_specs=[pl.BlockSpec((tm, ka), lambda i, j: (i, 0)),
                  pl.BlockSpec((tm, kb), lambda i, j: (i, 0)),
                  pl.BlockSpec((tm, kc), lambda i, j: (i, 0)),
                  pl.BlockSpec((ka, tn), lambda i, j: (0, j)),
                  pl.BlockSpec((kb, tn), lambda i, j: (0, j)),
                  pl.BlockSpec((kc, tn), lambda i, j: (0, j)),
                  pl.BlockSpec((tm, tn), lambda i, j: (i, j)),
                  pl.BlockSpec((tm, tn), lambda i, j: (i, nj + j)),
                  pl.BlockSpec((tm, tn), lambda i, j: (i, 2 * nj + j))],
        out_specs=pl.BlockSpec((tm, tn), lambda i, j: (i, j)),
        out_shape=jax.ShapeDtypeStruct((m, d), BF16),
        compiler_params=_cparams(("parallel", "arbitrary"), vmem),
        name="merge",
    )(o_a, o_b, o_c, w_a, w_b, w_c, gates, gates, gates)


def _route_metadata(route, n_experts):
    t = route.shape[0]
    a_tot = TOP_K * t
    a_pad = a_tot + n_experts * MOE_TILE
    e = jnp.concatenate([route[:, ROUTE_E1], route[:, ROUTE_E2]]).astype(jnp.int32)
    order = jnp.argsort(e, stable=True).astype(jnp.int32)
    counts = jnp.sum((e[:, None] == jnp.arange(n_experts, dtype=jnp.int32)[None, :]).astype(jnp.int32), axis=0)
    padded = (counts + MOE_TILE - 1) // MOE_TILE * MOE_TILE
    ends_u = jnp.cumsum(counts)
    ends_p = jnp.cumsum(padded)
    start_u = ends_u - counts
    start_p = ends_p - padded
    p = jnp.arange(a_pad, dtype=jnp.int32)
    ep = jnp.minimum(jnp.searchsorted(ends_p, p, side="right"), n_experts - 1).astype(jnp.int32)
    rank = p - start_p[ep]
    valid = jnp.logical_and(rank < counts[ep], p < ends_p[-1])
    src = jnp.clip(start_u[ep] + rank, 0, a_tot - 1)
    row_token = jnp.where(valid, order[src] % t, 0).astype(jnp.int32)
    rank_sorted = jnp.argsort(order).astype(jnp.int32)
    pos = rank_sorted + (start_p - start_u)[e]
    n_tiles = a_pad // MOE_TILE
    n_valid = (ends_p[-1] // MOE_TILE).astype(jnp.int32)
    tile_start = jnp.arange(n_tiles, dtype=jnp.int32) * MOE_TILE
    tile_e = jnp.minimum(jnp.searchsorted(ends_p, tile_start, side="right"), n_experts - 1).astype(jnp.int32)
    tile_e = jnp.where(tile_start < ends_p[-1], tile_e, tile_e[jnp.maximum(n_valid - 1, 0)])
    return row_token, tile_e, n_valid.reshape(1), pos[:t], pos[t:]


def _row_gather_start(idx_ref, base, src_ref, dst_ref, sem, rows):
    def issue(r, carry):
        pltpu.make_async_copy(src_ref.at[idx_ref[base + r]], dst_ref.at[r], sem).start()
        return carry

    lax.fori_loop(0, rows, issue, 0, unroll=8)


def _row_gather_wait(src_ref, dst_ref, sem, rows):
    pltpu.make_async_copy(src_ref.at[pl.ds(0, rows)], dst_ref, sem).wait()


def _prefetched_gather(starts, waits):
    i = pl.program_id(0)
    slot = i % 2

    @pl.when(i == 0)
    def _():
        starts(0, 0)

    @pl.when(i + 1 < pl.num_programs(0))
    def _():
        starts(i + 1, 1 - slot)

    waits(slot)
    return slot


def _gather_rows_body(idx_ref, src_ref, o_ref, buf, sem, *, rows):
    def starts(step, slot):
        _row_gather_start(idx_ref, step * rows, src_ref, buf.at[slot], sem.at[slot], rows)

    def waits(slot):
        _row_gather_wait(src_ref, buf.at[slot], sem.at[slot], rows)

    slot = _prefetched_gather(starts, waits)
    o_ref[...] = buf[slot].astype(o_ref.dtype)


def _gather_rows(src, idx):
    m = idx.shape[0]
    w = src.shape[1]
    rows = GATHER_ROWS
    assert m % rows == 0 and src.shape[0] >= rows
    return pl.pallas_call(
        functools.partial(_gather_rows_body, rows=rows),
        grid_spec=pltpu.PrefetchScalarGridSpec(
            num_scalar_prefetch=1,
            grid=(m // rows,),
            in_specs=[pl.BlockSpec(memory_space=pl.ANY)],
            out_specs=pl.BlockSpec((rows, w), lambda i, idx_ref: (i, 0)),
            scratch_shapes=[pltpu.VMEM((2, rows, w), src.dtype), pltpu.SemaphoreType.DMA((2,))]),
        out_shape=jax.ShapeDtypeStruct((m, w), BF16),
        compiler_params=_cparams(("arbitrary",), rows * w * (2 * 4 + 2 * 2 + 4)),
        name="moe_gather",
    )(idx, src)


def _weight_runs(tile_e):
    n_tiles = tile_e.shape[0]
    idx = jnp.arange(n_tiles, dtype=jnp.int32)
    start = jnp.logical_or(idx == 0, tile_e != jnp.roll(tile_e, 1))
    run_id = jnp.cumsum(start.astype(jnp.int32)) - 1
    start_or_end = jnp.where(start, idx, n_tiles)
    later = jnp.concatenate([start_or_end[1:], jnp.full((1,), n_tiles, jnp.int32)])
    nxt = jnp.flip(lax.cummin(jnp.flip(later)))
    wraps = nxt >= n_tiles
    next_e = jnp.where(wraps, tile_e[0], tile_e[jnp.minimum(nxt, n_tiles - 1)])
    return (start.astype(jnp.int32), run_id.astype(jnp.int32), next_e.astype(jnp.int32),
            wraps.astype(jnp.int32), (run_id[-1] + 1).reshape(1).astype(jnp.int32))


def _expert_weights(te_ref, start_ref, run_ref, nexte_ref, wraps_ref, nruns_ref, w_hbms, wbuf, w_bfs, sem,
                    *, tn, nj):
    j = pl.program_id(0)
    i = pl.program_id(1)

    def copies(e, jj, slot):
        cols = pl.ds(pl.multiple_of(jj * tn, tn), tn)
        return [pltpu.make_async_copy(w.at[e, :, cols], wbuf.at[slot, n], sem.at[slot, n])
                for n, w in enumerate(w_hbms)]

    @pl.when(start_ref[i] == 1)
    def _():
        run = j * nruns_ref[0] + run_ref[i]
        slot = run % 2

        @pl.when(run == 0)
        def _():
            for c in copies(te_ref[i], j, slot):
                c.start()

        for c in copies(te_ref[i], j, slot):
            c.wait()
        j_next = j + wraps_ref[i]

        @pl.when(j_next < nj)
        def _():
            for c in copies(nexte_ref[i], j_next, 1 - slot):
                c.start()

        for n, w_bf in enumerate(w_bfs):
            w_bf[...] = wbuf[slot, n].astype(BF16)


def _gmm_swiglu_body(te_ref, nv_ref, start_ref, run_ref, nexte_ref, wraps_ref, nruns_ref,
                     x_ref, wg_hbm, wu_hbm, o_ref, wbuf, wg_bf, wu_bf, sem, *, tn, nj):
    valid = pl.program_id(1) < nv_ref[0]
    _expert_weights(te_ref, start_ref, run_ref, nexte_ref, wraps_ref, nruns_ref,
                    (wg_hbm, wu_hbm), wbuf, (wg_bf, wu_bf), sem, tn=tn, nj=nj)

    @pl.when(valid)
    def _():
        a = x_ref[...]
        g = _dot(a, wg_bf[...])
        u = _dot(a, wu_bf[...])
        o_ref[...] = (_silu(g) * u).astype(o_ref.dtype)

    @pl.when(jnp.logical_not(valid))
    def _():
        o_ref[...] = jnp.zeros_like(o_ref)


def _gmm_down_body(te_ref, nv_ref, start_ref, run_ref, nexte_ref, wraps_ref, nruns_ref,
                   a_ref, w_hbm, o_ref, wbuf, w_bf, sem, *, tn, nj):
    valid = pl.program_id(1) < nv_ref[0]
    _expert_weights(te_ref, start_ref, run_ref, nexte_ref, wraps_ref, nruns_ref,
                    (w_hbm,), wbuf, (w_bf,), sem, tn=tn, nj=nj)

    @pl.when(valid)
    def _():
        o_ref[...] = _dot(a_ref[...], w_bf[...])

    @pl.when(jnp.logical_not(valid))
    def _():
        o_ref[...] = jnp.zeros_like(o_ref)


def _gmm_maps():
    def rows(j, i, te, nv, *_):
        return (jnp.minimum(i, nv[0] - 1), 0)

    def weights(j, i, te, nv, *_):
        return (te[i], 0, j)

    def out(j, i, te, nv, *_):
        return (i, j)

    return rows, weights, out


def _gmm_swiglu(x_sorted, wg, wu, tile_e, n_valid):
    m, k = x_sorted.shape
    n = wg.shape[2]
    tm = MOE_TILE
    tn = _pick(n, (512, 256, 128))
    nj = n // tn
    rows, _, out = _gmm_maps()
    vmem = 2 * (tm * k * 2 + 2 * k * tn * 4 + tm * tn * 2) + 2 * k * tn * 2 + 3 * tm * tn * 4
    return pl.pallas_call(
        functools.partial(_gmm_swiglu_body, tn=tn, nj=nj),
        grid_spec=pltpu.PrefetchScalarGridSpec(
            num_scalar_prefetch=7,
            grid=(nj, m // tm),
            in_specs=[pl.BlockSpec((tm, k), rows),
                      pl.BlockSpec(memory_space=pl.ANY),
                      pl.BlockSpec(memory_space=pl.ANY)],
            out_specs=pl.BlockSpec((tm, tn), out),
            scratch_shapes=[pltpu.VMEM((2, 2, k, tn), F32), pltpu.VMEM((k, tn), BF16), pltpu.VMEM((k, tn), BF16),
                            pltpu.SemaphoreType.DMA((2, 2))]),
        out_shape=jax.ShapeDtypeStruct((m, n), BF16),
        compiler_params=_cparams(("arbitrary", "arbitrary"), vmem),
        name="moe_up",
    )(tile_e, n_valid, *_weight_runs(tile_e), x_sorted, wg, wu)


def _gmm_down(a_sorted, wd, tile_e, n_valid):
    m, k = a_sorted.shape
    n = wd.shape[2]
    tm = MOE_TILE
    tn = _pick(n, (1024, 512, 256, 128))
    nj = n // tn
    rows, _, out = _gmm_maps()
    vmem = 2 * (tm * k * 2 + k * tn * 4 + tm * tn * 4) + k * tn * 2 + tm * tn * 4
    return pl.pallas_call(
        functools.partial(_gmm_down_body, tn=tn, nj=nj),
        grid_spec=pltpu.PrefetchScalarGridSpec(
            num_scalar_prefetch=7,
            grid=(nj, m // tm),
            in_specs=[pl.BlockSpec((tm, k), rows),
                      pl.BlockSpec(memory_space=pl.ANY)],
            out_specs=pl.BlockSpec((tm, tn), out),
            scratch_shapes=[pltpu.VMEM((2, 1, k, tn), F32), pltpu.VMEM((k, tn), BF16),
                            pltpu.SemaphoreType.DMA((2, 1))]),
        out_shape=jax.ShapeDtypeStruct((m, n), F32),
        compiler_params=_cparams(("arbitrary", "arbitrary"), vmem),
        name="moe_down",
    )(tile_e, n_valid, *_weight_runs(tile_e), a_sorted, wd)


def _moe_combine_body(p1_ref, p2_ref, y_ref, x_ref, gate_ref, route_ref, o_ref, buf1, buf2, sem, *, rows):
    def starts(step, slot):
        _row_gather_start(p1_ref, step * rows, y_ref, buf1.at[slot], sem.at[0, slot], rows)
        _row_gather_start(p2_ref, step * rows, y_ref, buf2.at[slot], sem.at[1, slot], rows)

    def waits(slot):
        _row_gather_wait(y_ref, buf1.at[slot], sem.at[0, slot], rows)
        _row_gather_wait(y_ref, buf2.at[slot], sem.at[1, slot], rows)

    slot = _prefetched_gather(starts, waits)
    route = route_ref[...]
    w1 = route[:, ROUTE_W1:ROUTE_W1 + 1]
    w2 = route[:, ROUTE_W2:ROUTE_W2 + 1]
    o_ref[...] = x_ref[...] + gate_ref[...] * (w1 * buf1[slot] + w2 * buf2[slot])


def _moe_combine(y_sorted, pos1, pos2, x, gate, route, seq):
    t, d = x.shape
    rows = _pick(seq, (128,))
    per_batch = seq // rows
    return pl.pallas_call(
        functools.partial(_moe_combine_body, rows=rows),
        grid_spec=pltpu.PrefetchScalarGridSpec(
            num_scalar_prefetch=2,
            grid=(t // rows,),
            in_specs=[pl.BlockSpec(memory_space=pl.ANY),
                      pl.BlockSpec((rows, d), lambda i, p1, p2: (i, 0)),
                      pl.BlockSpec((None, 1, d), lambda i, p1, p2: (i // per_batch, 0, 0)),
                      pl.BlockSpec((rows, V7X_LANES), lambda i, p1, p2: (i, 0))],
            out_specs=pl.BlockSpec((rows, d), lambda i, p1, p2: (i, 0)),
            scratch_shapes=[pltpu.VMEM((2, rows, d), F32), pltpu.VMEM((2, rows, d), F32),
                            pltpu.SemaphoreType.DMA((2, 2))]),
        out_shape=jax.ShapeDtypeStruct((t, d), F32),
        compiler_params=_cparams(("arbitrary",), 10 * rows * d * 4),
        name="moe_combine",
    )(pos1, pos2, y_sorted, x, gate, route)


_HG_LEVELS = (HG_CHUNK // HG_SUB).bit_length() - 1


def _hgrn_level_masks():
    ti = np.arange(HG_CHUNK)[:, None]
    si = np.arange(HG_CHUNK)[None, :]
    out = []
    for lvl in range(_HG_LEVELS):
        half = HG_SUB << lvl
        blk = 2 * half
        out.append((ti // blk == si // blk) & (ti % blk >= half) & (si % blk < half))
    return jnp.asarray(np.stack(out), F32)


def _hgrn_chunk(q_in, f_in, v, g_in, lb, gain, state_t, tri, ones, lvl_mask_ref):
    c = HG_CHUNK
    fg = lb + (1.0 - lb) * _sigmoid(f_in)
    log_f = jnp.log2(jnp.maximum(fg, MIN_FORGET))
    k = 1.0 - fg
    q = _silu(q_in)
    lf_hi, lf_mid, lf_lo = _split3(log_f)
    b = _dot(tri, lf_hi) + _dot(tri, lf_mid) + _dot(tri, lf_lo)
    b_last = b[c - 1:c, :]

    o = _dot_nt((q * jnp.exp2(b)).astype(BF16), state_t.astype(BF16))

    row = lax.broadcasted_iota(jnp.int32, (c, 1), 0)
    scores = jnp.zeros((c, c), F32)
    for lvl in range(_HG_LEVELS):
        half = HG_SUB << lvl
        blk = 2 * half
        bref = jnp.concatenate(
            [jnp.broadcast_to(b[p * blk + half - 1:p * blk + half, :], (blk, HG_DK)) for p in range(c // blk)],
            axis=0)
        is_q = (row & half) != 0
        e = jnp.exp2(-jnp.abs(b - bref))
        xk = jnp.where(is_q, q, k) * e
        qd = jnp.where(is_q, xk, 0.0).astype(BF16)
        kd = jnp.where(is_q, 0.0, xk).astype(BF16)
        scores = scores + _dot_nt(qd, kd) * lvl_mask_ref[lvl]
    o = o + _dot(scores.astype(BF16), v.astype(BF16))

    nb = c // HG_SUB
    b3 = b.reshape(nb, HG_SUB, HG_DK)
    q3 = q.reshape(nb, HG_SUB, HG_DK)
    k3 = k.reshape(nb, HG_SUB, HG_DK)
    v3 = v.reshape(nb, HG_SUB, HG_DK)
    t_in = lax.broadcasted_iota(jnp.int32, (nb, HG_SUB, HG_DK), 1)
    for s in range(HG_SUB):
        diff = b3 - b3[:, s:s + 1, :]
        dec = jnp.exp2(diff if s == 0 else jnp.where(t_in >= s, diff, NEG_BIG))
        m = (q3 * (k3[:, s:s + 1, :] * dec)).reshape(c, HG_DK)
        r = _dot(m.astype(BF16), ones)
        o = o + r * jnp.broadcast_to(v3[:, s:s + 1, :], (nb, HG_SUB, HG_DK)).reshape(c, HG_DK)

    kdec = (k * jnp.exp2(b_last - b)).astype(BF16)
    new_state_t = state_t * jnp.exp2(b_last) + _dot_tn(v.astype(BF16), kdec)

    ms = jnp.mean(o * o, axis=-1, keepdims=True)
    out = o * lax.rsqrt(ms + NORM_EPS) * gain * _silu(g_in)
    return out, new_state_t


def _hgrn_body(q_ref, f_ref, i_ref, g_ref, lbraw_ref, gain_ref, lvl_mask_ref, o_ref, state_ref,
               *, layer, n_chunks, group):
    @pl.when(pl.program_id(2) == 0)
    def _():
        state_ref[...] = jnp.zeros_like(state_ref)

    lbr = lbraw_ref[...]
    ex = jnp.exp(lbr - jnp.max(lbr, axis=0, keepdims=True))
    soft = ex / jnp.sum(ex, axis=0, keepdims=True)
    lb = jnp.zeros((1, group * HG_DK), F32)
    for j in range(1, layer + 1):
        lb = lb + soft[j:j + 1, :]
    gain = gain_ref[...]
    c = HG_CHUNK
    tri = (lax.broadcasted_iota(jnp.int32, (c, c), 0) >= lax.broadcasted_iota(jnp.int32, (c, c), 1)).astype(BF16)
    ones = jnp.ones((HG_DK, HG_DK), BF16)

    def chunk(ci, carry):
        sl = pl.ds(pl.multiple_of(ci * c, c), c)
        for hh in range(group):
            cs = slice(hh * HG_DK, (hh + 1) * HG_DK)
            out, new_state = _hgrn_chunk(q_ref[sl, cs].astype(F32), f_ref[sl, cs].astype(F32),
                                         i_ref[sl, cs].astype(F32), g_ref[sl, cs].astype(F32),
                                         lb[:, cs], gain, state_ref[hh], tri, ones, lvl_mask_ref)
            o_ref[sl, cs] = out.astype(o_ref.dtype)
            state_ref[hh] = new_state
        return carry

    lax.fori_loop(0, n_chunks, chunk, 0)


def _hgrn(proj, lb_raw, out_gain, layer, n_batch, seq, width):
    heads = width // HG_DK
    group = HG_GROUP if heads % HG_GROUP == 0 else 1
    hgroups = heads // group
    gw = group * HG_DK
    lc = _pick(seq, (512, 256, 128))
    per_batch = seq // lc
    depth = lb_raw.shape[0]

    def col(off):
        return pl.BlockSpec((lc, gw), lambda b, h, i: (b * per_batch + i, off * hgroups + h))

    return pl.pallas_call(
        functools.partial(_hgrn_body, layer=layer, n_chunks=lc // HG_CHUNK, group=group),
        grid=(n_batch, hgroups, per_batch),
        in_specs=[col(0), col(1), col(2), col(3),
                  pl.BlockSpec((depth, gw), lambda b, h, i: (0, h)),
                  pl.BlockSpec((1, HG_DK), lambda b, h, i: (0, 0)),
                  pl.BlockSpec((_HG_LEVELS, HG_CHUNK, HG_CHUNK), lambda b, h, i: (0, 0, 0))],
        out_specs=pl.BlockSpec((lc, gw), lambda b, h, i: (b * per_batch + i, h)),
        out_shape=jax.ShapeDtypeStruct((n_batch * seq, width), BF16),
        scratch_shapes=[pltpu.VMEM((group, HG_DK, HG_DK), F32)],
        compiler_params=_cparams(("parallel", "parallel", "arbitrary"), 16 * 1024 * 1024),
        name="hgrn2",
    )(proj, proj, proj, proj, lb_raw, out_gain.reshape(1, HG_DK), _hgrn_level_masks())


def _pool_body(u_ref, halo_ref, w_ref, scale_ref, o_ref, ext_ref, *, ts, gdim):
    i = pl.program_id(1)
    halo = halo_ref[...].astype(F32)
    ext_ref[0:POOL_HALO, :] = jnp.where(i > 0, halo, 0.0)
    ext_ref[POOL_HALO:, :] = u_ref[...].astype(F32)
    pos = (i * ts + lax.broadcasted_iota(jnp.int32, (ts, 1), 0) + 1).astype(F32)
    for g, win in enumerate(POOL_WINDOWS):
        cs = slice(g * gdim, (g + 1) * gdim)
        acc = ext_ref[POOL_HALO:, cs]
        for j in range(1, win):
            acc = acc + ext_ref[POOL_HALO - j:POOL_HALO - j + ts, cs]
        pooled = acc / jnp.minimum(pos, float(win)) - ext_ref[POOL_HALO:, cs]
        y = _dot(pooled.astype(BF16), w_ref[g])
        o_ref[:, cs] = (y * scale_ref[:, cs]).astype(o_ref.dtype)


def _pool(proj, col_off, pool_w, pool_scale, n_batch, seq, width):
    groups = len(POOL_WINDOWS)
    gdim = width // groups
    assert gdim % V7X_LANES == 0 and col_off % width == 0 and max(POOL_WINDOWS) <= POOL_HALO
    ts = _pick(seq, (512, 256, 128))
    per_batch = seq // ts
    cb = col_off // width
    hb = ts // POOL_HALO
    return pl.pallas_call(
        functools.partial(_pool_body, ts=ts, gdim=gdim),
        grid=(n_batch, per_batch),
        in_specs=[pl.BlockSpec((ts, width), lambda b, i: (b * per_batch + i, cb)),
                  pl.BlockSpec((POOL_HALO, width),
                               lambda b, i: (jnp.maximum((b * per_batch + i) * hb - 1, 0), cb)),
                  pl.BlockSpec((groups, gdim, gdim), lambda b, i: (0, 0, 0)),
                  pl.BlockSpec((1, width), lambda b, i: (0, 0))],
        out_specs=pl.BlockSpec((ts, width), lambda b, i: (b * per_batch + i, 0)),
        out_shape=jax.ShapeDtypeStruct((n_batch * seq, width), BF16),
        scratch_shapes=[pltpu.VMEM((ts + POOL_HALO, width), F32)],
        compiler_params=_cparams(("parallel", "parallel"), 16 * 1024 * 1024),
        name="pool",
    )(proj, proj, pool_w, pool_scale.reshape(1, width))


def _rope_table_body(pos_ref, cos_ref, sin_ref):
    pos = pos_ref[...].astype(F32)
    lane = lax.broadcasted_iota(jnp.int32, (1, V7X_LANES), 1)
    j = lane % MLA_ROPE
    fidx = (j % (MLA_ROPE // 2)).astype(F32)
    inv_freq = jnp.exp(fidx * (-2.0 / MLA_ROPE * math.log(ROPE_THETA)))
    ang = pos * inv_freq
    cos_ref[...] = jnp.cos(ang)
    sin_ref[...] = jnp.where(j < MLA_ROPE // 2, -1.0, 1.0) * jnp.sin(ang)


def _rope_table(positions):
    t = positions.size
    ts = _pick(t, (512, 256, 128))
    return pl.pallas_call(
        _rope_table_body,
        grid=(t // ts,),
        in_specs=[pl.BlockSpec((ts, 1), lambda i: (i, 0))],
        out_specs=[pl.BlockSpec((ts, V7X_LANES), lambda i: (i, 0))] * 2,
        out_shape=[jax.ShapeDtypeStruct((t, V7X_LANES), F32)] * 2,
        compiler_params=_cparams(("parallel",), 4 * 1024 * 1024),
        name="rope_table",
    )(positions.reshape(t, 1))


def _rms(x, gain):
    return x * lax.rsqrt(jnp.mean(x * x, axis=-1, keepdims=True) + NORM_EPS) * gain


def _prep_q_body(cq_ref, qn_ref, w_ref, gn_ref, gr_ref, grs_ref, seg_ref, cos_ref, sin_ref, o_ref,
                 *, heads, sm_scale):
    hn = heads * MLA_NOPE
    hr = heads * MLA_ROPE
    hq = _rms(cq_ref[...].astype(F32), qn_ref[...]).astype(BF16)
    y = _dot(hq, w_ref[...])
    yr = y[:, hn:hn + hr]
    ys = y[:, hn + hr:]
    sq_hi, sq_lo = _split2(yr * yr)
    seg = seg_ref[...]
    ss = _dot(sq_hi, seg) + _dot(sq_lo, seg)
    inv = lax.rsqrt(ss * (1.0 / MLA_ROPE) + NORM_EPS)
    reps = hr // V7X_LANES
    cosf = jnp.concatenate([cos_ref[...]] * reps, axis=1)
    sinf = jnp.concatenate([sin_ref[...]] * reps, axis=1)
    qr = (yr * inv * gr_ref[...]) * cosf + (ys * inv * grs_ref[...]) * sinf
    gn = gn_ref[...] * sm_scale
    for h in range(heads):
        qn = _rms(y[:, h * MLA_NOPE:(h + 1) * MLA_NOPE], gn)
        o_ref[h, :, 0:MLA_NOPE] = qn.astype(o_ref.dtype)
        o_ref[h, :, MLA_NOPE:] = (qr[:, h * MLA_ROPE:(h + 1) * MLA_ROPE] * sm_scale).astype(o_ref.dtype)


def _prep_kv_body(ckv_ref, kpe_ref, kvn_ref, w_ref, gn_ref, gr2_ref, cos_ref, sin_ref, k_ref, v_ref, *, heads):
    hk = _rms(ckv_ref[...].astype(F32), kvn_ref[...]).astype(BF16)
    y = _dot(hk, w_ref[...])
    kp = kpe_ref[...].astype(F32)
    kpe = kp[:, :MLA_ROPE]
    inv = lax.rsqrt(jnp.mean(kpe * kpe, axis=-1, keepdims=True) + NORM_EPS)
    kn = kp * inv * gr2_ref[...]
    kr = (kn[:, :MLA_ROPE] * cos_ref[:, :MLA_ROPE] + kn[:, MLA_ROPE:] * sin_ref[:, :MLA_ROPE]).astype(k_ref.dtype)
    per = MLA_NOPE + MLA_V
    for h in range(heads):
        k_ref[h, :, 0:MLA_NOPE] = _rms(y[:, h * per:h * per + MLA_NOPE], gn_ref[...]).astype(k_ref.dtype)
        k_ref[h, :, MLA_NOPE:] = kr
        v_ref[h, 0, :MLA_V, :] = y[:, h * per + MLA_NOPE:(h + 1) * per].T.astype(v_ref.dtype)
        v_ref[h, 0, MLA_V:, :] = jnp.ones((V7X_BF16_ROWS, y.shape[0]), v_ref.dtype)


def _flash_body(q_ref, k_ref, vt_ref, o_ref, acc_ref, st_a, st_b, *, tile, group):
    qi = pl.program_id(2)
    acc_ref[...] = jnp.zeros_like(acc_ref)

    def scores(j, dst):
        ks = pl.ds(pl.multiple_of(j * tile, tile), tile)
        for g in range(group):
            dst[g] = _dot_nt(k_ref[g, ks, :], q_ref[g])

    def consume(j, src, ms, masked):
        m_new = []
        for g in range(group):
            st = src[g]
            if masked:
                kidx = lax.broadcasted_iota(jnp.int32, (tile, tile), 0)
                qidx = lax.broadcasted_iota(jnp.int32, (tile, tile), 1)
                st = jnp.where(kidx <= qidx, st, NEG_BIG)
            m = jnp.maximum(ms[g], jnp.max(st, axis=0, keepdims=True))
            p = jnp.exp2(st - m).astype(BF16)
            acc_ref[g] = jnp.exp2(ms[g] - m) * acc_ref[g] + _dot(vt_ref[g, j], p)
            m_new.append(m)
        return tuple(m_new)

    def finish():
        for g in range(group):
            o = acc_ref[g, :MLA_V, :] / acc_ref[g, MLA_V:MLA_V + 1, :]
            o_ref[:, g * MLA_V:(g + 1) * MLA_V] = o.T.astype(o_ref.dtype)

    scores(0, st_a)

    def pair(pi, ms):
        j = 2 * pi
        scores(j + 1, st_b)
        ms = consume(j, st_a, ms, False)
        scores(j + 2, st_a)
        return consume(j + 1, st_b, ms, False)

    m0 = tuple(jnp.full((1, tile), NEG_BIG, F32) for _ in range(group))
    ms = lax.fori_loop(0, qi // 2, pair, m0)
    odd = qi % 2 == 1

    @pl.when(odd)
    def _():
        scores(qi, st_b)
        consume(qi, st_b, consume(qi - 1, st_a, ms, False), True)
        finish()

    @pl.when(jnp.logical_not(odd))
    def _():
        consume(qi, st_a, ms, True)
        finish()


def _mla(proj_c, cos_t, sin_t, q_norm, w_q, kv_norm, w_kv, g_qn, g_qr, g_kn, g_kr,
         n_batch, seq, heads, q_rank, kv_rank):
    t = n_batch * seq
    half = MLA_ROPE // 2
    hn, hr = heads * MLA_NOPE, heads * MLA_ROPE
    dq = MLA_NOPE + MLA_ROPE
    sm_scale = float(dq) ** -0.5 * math.log2(math.e)
    assert q_rank % kv_rank == 0 and (q_rank + kv_rank) % V7X_LANES == 0 and hr % V7X_LANES == 0
    tile = ATTN_TILE
    assert seq % tile == 0
    tm = tile
    vrows = MLA_V + V7X_BF16_ROWS

    def swap(g):
        return jnp.concatenate([g[half:], g[:half]])

    gr = jnp.tile(g_qr, heads).reshape(1, hr)
    grs = jnp.tile(swap(g_qr), heads).reshape(1, hr)
    lane = jnp.arange(hr) // MLA_ROPE
    seg = (lane[:, None] == lane[None, :]).astype(BF16)
    row = lambda i: (i, 0)
    const2 = lambda i: (0, 0)
    qcat = pl.pallas_call(
        functools.partial(_prep_q_body, heads=heads, sm_scale=sm_scale),
        grid=(t // tm,),
        in_specs=[pl.BlockSpec((tm, q_rank), row),
                  pl.BlockSpec((1, q_rank), const2),
                  pl.BlockSpec((q_rank, hn + 2 * hr), const2),
                  pl.BlockSpec((1, MLA_NOPE), const2),
                  pl.BlockSpec((1, hr), const2),
                  pl.BlockSpec((1, hr), const2),
                  pl.BlockSpec((hr, hr), const2),
                  pl.BlockSpec((tm, V7X_LANES), row),
                  pl.BlockSpec((tm, V7X_LANES), row)],
        out_specs=pl.BlockSpec((heads, tm, dq), lambda i: (0, i, 0)),
        out_shape=jax.ShapeDtypeStruct((heads, t, dq), BF16),
        compiler_params=_cparams(("parallel",), 40 * 1024 * 1024),
        name="mla_prep_q",
    )(proj_c, q_norm.reshape(1, q_rank), w_q, g_qn.reshape(1, MLA_NOPE), gr, grs, seg, cos_t, sin_t)

    gr2 = jnp.concatenate([g_kr, swap(g_kr)]).reshape(1, 2 * MLA_ROPE)
    kcat, vt = pl.pallas_call(
        functools.partial(_prep_kv_body, heads=heads),
        grid=(t // tm,),
        in_specs=[pl.BlockSpec((tm, kv_rank), lambda i: (i, q_rank // kv_rank)),
                  pl.BlockSpec((tm, V7X_LANES), lambda i: (i, (q_rank + kv_rank) // V7X_LANES)),
                  pl.BlockSpec((1, kv_rank), const2),
                  pl.BlockSpec((kv_rank, heads * (MLA_NOPE + MLA_V)), const2),
                  pl.BlockSpec((1, MLA_NOPE), const2),
                  pl.BlockSpec((1, 2 * MLA_ROPE), const2),
                  pl.BlockSpec((tm, V7X_LANES), row),
                  pl.BlockSpec((tm, V7X_LANES), row)],
        out_specs=[pl.BlockSpec((heads, tm, dq), lambda i: (0, i, 0)),
                   pl.BlockSpec((heads, 1, vrows, tile), lambda i: (0, i, 0, 0))],
        out_shape=[jax.ShapeDtypeStruct((heads, t, dq), BF16),
                   jax.ShapeDtypeStruct((heads, t // tile, vrows, tile), BF16)],
        compiler_params=_cparams(("parallel",), 40 * 1024 * 1024),
        name="mla_prep_kv",
    )(proj_c, proj_c, kv_norm.reshape(1, kv_rank), w_kv, g_kn.reshape(1, MLA_NOPE), gr2, cos_t, sin_t)

    nq = seq // tile
    group = ATTN_GROUP if heads % ATTN_GROUP == 0 else 1
    return pl.pallas_call(
        functools.partial(_flash_body, tile=tile, group=group),
        grid=(heads // group, n_batch, nq),
        in_specs=[pl.BlockSpec((group, tile, dq), lambda h, b, i: (h, b * nq + i, 0)),
                  pl.BlockSpec((group, seq, dq), lambda h, b, i: (h, b, 0)),
                  pl.BlockSpec((group, nq, vrows, tile), lambda h, b, i: (h, b, 0, 0))],
        out_specs=pl.BlockSpec((tile, group * MLA_V), lambda h, b, i: (b * nq + i, h)),
        out_shape=jax.ShapeDtypeStruct((t, heads * MLA_V), BF16),
        scratch_shapes=[pltpu.VMEM((group, vrows, tile), F32),
                        pltpu.VMEM((group, tile, tile), F32), pltpu.VMEM((group, tile, tile), F32)],
        compiler_params=_cparams(("parallel", "parallel", "arbitrary"), 48 * 1024 * 1024),
        name="mla_flash",
    )(qcat, kcat, vt)


def kernel(x, c, positions, ada_w, ada_b, ada_layer, mix_norm, ffn_norm, w_in, hgrn_lower_bounds,
           hgrn_out_norm, pool_w, pool_scale, mla_q_norm, mla_w_uq, mla_kv_norm, mla_w_ukv,
           mla_qk_norm_q_nope, mla_qk_norm_q_rope, mla_qk_norm_k_nope, mla_qk_norm_k_rope,
           w_branch_a, w_branch_b, w_branch_c, w_o, ffn_w_gate, ffn_w_up, ffn_w_down,
           moe_router, moe_w_gate, moe_w_up, moe_w_down):
    n_batch, seq, d = x.shape
    depth = w_in.shape[0]
    t = n_batch * seq
    hg_width = hgrn_lower_bounds.shape[1]
    pool_width = pool_scale.shape[1]
    q_rank = mla_q_norm.shape[1]
    kv_rank = mla_kv_norm.shape[1]
    heads = mla_w_ukv.shape[2] // (MLA_NOPE + MLA_V)
    half = MLA_ROPE // 2
    n_a = 4 * hg_width + pool_width
    n_c = q_rank + kv_rank + MLA_ROPE
    assert w_in.shape[2] == n_a + n_c + 3 * d

    mod = _ada(c, ada_w, ada_b, ada_layer)
    cos_t, sin_t = _rope_table(positions)
    xf = x.reshape(t, d)

    for l in range(depth):
        mod_l = mod[l]
        gate1 = mod_l[:, 2:3, :]
        gate2 = mod_l[:, 5:6, :]

        w_l = w_in[l]
        w_a = w_l[:, :n_a].astype(BF16)
        kpe_w = w_l[:, n_a + q_rank + kv_rank:n_a + n_c]
        w_c = jnp.concatenate([w_l[:, n_a:n_a + n_c], kpe_w[:, half:], kpe_w[:, :half]], axis=1).astype(BF16)
        w_g = w_l[:, n_a + n_c:].astype(BF16)
        wq = mla_w_uq[l].reshape(q_rank, heads, MLA_NOPE + MLA_ROPE)
        wq_r = wq[:, :, MLA_NOPE:]
        w_q = jnp.concatenate([
            wq[:, :, :MLA_NOPE].reshape(q_rank, heads * MLA_NOPE),
            wq_r.reshape(q_rank, heads * MLA_ROPE),
            jnp.concatenate([wq_r[:, :, half:], wq_r[:, :, :half]], axis=2).reshape(q_rank, heads * MLA_ROPE),
        ], axis=1).astype(BF16)

        h = _modulate(xf.reshape(n_batch, seq, d), mix_norm[l], mod_l, 0, 1)
        proj_a = _mm_cast(h, w_a, name="proj_a")
        proj_c = _mm_cast(h, w_c, name="proj_c")
        gates = _mm_cast(h, w_g, name="proj_gates")

        o_a = _hgrn(proj_a, hgrn_lower_bounds, hgrn_out_norm[l], l, n_batch, seq, hg_width)
        o_b = _pool(proj_a, 4 * hg_width, pool_w[l].astype(BF16), pool_scale[l], n_batch, seq, pool_width)
        o_c = _mla(proj_c, cos_t, sin_t, mla_q_norm[l], w_q, mla_kv_norm[l], mla_w_ukv[l].astype(BF16),
                   mla_qk_norm_q_nope[l], mla_qk_norm_q_rope[l], mla_qk_norm_k_nope[l], mla_qk_norm_k_rope[l],
                   n_batch, seq, heads, q_rank, kv_rank)
        merged = _merge(o_a, o_b, o_c, w_branch_a[l].astype(BF16), w_branch_b[l].astype(BF16),
                        w_branch_c[l].astype(BF16), gates)
        xf = _mm_residual(merged, w_o[l].astype(BF16), xf, gate1, seq, name="out_proj")

        j = l // 2
        if l % 2 == 0:
            h = _modulate(xf.reshape(n_batch, seq, d), ffn_norm[l], mod_l, 3, 4)
            act = _mm_swiglu(h, ffn_w_gate[j].astype(BF16), ffn_w_up[j].astype(BF16), name="ffn_up")
            xf = _mm_residual(act, ffn_w_down[j].astype(BF16), xf, gate2, seq, name="ffn_down")
        else:
            n_experts = moe_w_gate.shape[1]
            h, route = _modulate(xf.reshape(n_batch, seq, d), ffn_norm[l], mod_l, 3, 4, router=moe_router[j])
            row_token, tile_e, n_valid, pos1, pos2 = _route_metadata(route, n_experts)
            h_sorted = _gather_rows(h, row_token)
            act = _gmm_swiglu(h_sorted, moe_w_gate[j], moe_w_up[j], tile_e, n_valid)
            y_sorted = _gmm_down(act, moe_w_down[j], tile_e, n_valid)
            xf = _moe_combine(y_sorted, pos1, pos2, xf, gate2, route, seq)
    return xf.reshape(n_batch, seq, d)
```

```python
import functools
import math

import numpy as np
import jax
import jax.numpy as jnp
from jax import lax
from jax.experimental import pallas as pl
from jax.experimental.pallas import tpu as pltpu

F32 = jnp.float32
BF16 = jnp.bfloat16

HG_DK = 128
POOL_WINDOWS = (2, 4, 8, 16)
MLA_NOPE = 128
MLA_ROPE = 64
MLA_V = 128
ROPE_THETA = 10000.0
MIN_FORGET = 1e-30
NORM_EPS = 1e-6
N_MOD = 6
TOP_K = 2
NEG_BIG = -1e30

V7X_LANES = 128
V7X_SUBLANES = 8
V7X_BF16_ROWS = 16
V7X_VMEM_BYTES = 64 * 1024 * 1024
VMEM_CAP = V7X_VMEM_BYTES - 8 * 1024 * 1024

HG_CHUNK = 128
HG_SUB = 8
HG_GROUP = 8
POOL_HALO = 16
ADA_CHUNK = 512
ATTN_TILE = 1024
ATTN_GROUP = 2
ROUTE_E1, ROUTE_E2, ROUTE_W1, ROUTE_W2 = 0, 1, 2, 3
MOE_TILE = 512
GATHER_ROWS = 512


def _pick(n, prefs):
    for p in prefs:
        if n % p == 0:
            return p
    raise ValueError(f"no tile in {prefs} divides {n}")


def _cparams(sem, vmem_bytes):
    limit = int(min(VMEM_CAP, max(32 * 1024 * 1024, vmem_bytes * 5 // 4)))
    return pltpu.CompilerParams(dimension_semantics=sem, vmem_limit_bytes=limit)


def _sigmoid(x):
    return 1.0 / (1.0 + jnp.exp(-x))


def _silu(x):
    return x * _sigmoid(x)


def _dot(a, b):
    return jnp.dot(a, b, preferred_element_type=F32)


def _dot_nt(a, b):
    return lax.dot_general(a, b, (((1,), (1,)), ((), ())), preferred_element_type=F32)


def _dot_tn(a, b):
    return lax.dot_general(a, b, (((0,), (0,)), ((), ())), preferred_element_type=F32)


def _split3(x):
    hi = x.astype(BF16)
    r1 = x - hi.astype(F32)
    mid = r1.astype(BF16)
    lo = (r1 - mid.astype(F32)).astype(BF16)
    return hi, mid, lo


def _split2(x):
    hi = x.astype(BF16)
    lo = (x - hi.astype(F32)).astype(BF16)
    return hi, lo


def _ada_body(ct_ref, w_ref, b_ref, lay_ref, o_ref, acc_ref, *, n_batch, depth):
    kk = pl.program_id(1)
    tk, tn = w_ref.shape
    sub = V7X_SUBLANES

    @pl.when(kk == 0)
    def _():
        acc_ref[...] = jnp.zeros_like(acc_ref)

    s = _silu(ct_ref[pl.ds(pl.multiple_of(kk * tk, tk), tk), :])
    for c0 in range(0, tn, ADA_CHUNK):
        w = w_ref[:, c0:c0 + ADA_CHUNK]
        for b in range(n_batch):
            prod = (w * s[:, b:b + 1]).reshape(tk // sub, sub, ADA_CHUNK)
            acc_ref[b, :, c0:c0 + ADA_CHUNK] += jnp.sum(prod, axis=0)

    @pl.when(kk == pl.num_programs(1) - 1)
    def _():
        for b in range(n_batch):
            r = jnp.sum(acc_ref[b], axis=0, keepdims=True) + b_ref[...]
            for l in range(depth):
                o_ref[l, b:b + 1, :] = r + lay_ref[l:l + 1, :]


def _ada(c, ada_w, ada_b, ada_layer):
    n_batch, d = c.shape
    depth = ada_layer.shape[0]
    n = ada_w.shape[1]
    tn = n
    tk = _pick(d, (64, 32, 16, 8))
    assert tn % ADA_CHUNK == 0
    ct = c.T
    lay = ada_layer.reshape(depth, n)
    out = pl.pallas_call(
        functools.partial(_ada_body, n_batch=n_batch, depth=depth),
        scratch_shapes=[pltpu.VMEM((n_batch, V7X_SUBLANES, tn), F32)],
        grid=(n // tn, d // tk),
        in_specs=[
            pl.BlockSpec((d, n_batch), lambda j, k: (0, 0)),
            pl.BlockSpec((tk, tn), lambda j, k: (k, j)),
            pl.BlockSpec((1, tn), lambda j, k: (0, j)),
            pl.BlockSpec((depth, tn), lambda j, k: (0, j)),
        ],
        out_specs=pl.BlockSpec((depth, n_batch, tn), lambda j, k: (0, 0, j)),
        out_shape=jax.ShapeDtypeStruct((depth, n_batch, n), F32),
        compiler_params=_cparams(("parallel", "arbitrary"), 4 * tk * tn * 4),
        name="ada",
    )(ct, ada_w, ada_b.reshape(1, n), lay)
    return out.reshape(depth, n_batch, N_MOD, d)


def _modulated(x_ref, g_ref, mod_ref, shift_idx, scale_idx):
    x = x_ref[...]
    ms = jnp.mean(x * x, axis=-1, keepdims=True)
    y = x * lax.rsqrt(ms + NORM_EPS) * g_ref[...]
    return y * (1.0 + mod_ref[scale_idx:scale_idx + 1, :]) + mod_ref[shift_idx:shift_idx + 1, :]


def _modulate_body(x_ref, g_ref, mod_ref, o_ref, *, shift_idx, scale_idx):
    o_ref[...] = _modulated(x_ref, g_ref, mod_ref, shift_idx, scale_idx).astype(o_ref.dtype)


def _modulate_route_body(x_ref, g_ref, mod_ref, r_ref, o_ref, route_ref, *, shift_idx, scale_idx, n_experts):
    h = _modulated(x_ref, g_ref, mod_ref, shift_idx, scale_idx)
    o_ref[...] = h
    h_hi, h_mid, h_lo = _split3(h)
    r = r_ref[...]
    r_hi, r_mid, r_lo = _split3(r)
    logits = (_dot(h_hi, r_hi) + _dot(h_hi, r_mid) + _dot(h_mid, r_hi)
              + _dot(h_hi, r_lo) + _dot(h_mid, r_mid) + _dot(h_lo, r_hi))
    lane = lax.broadcasted_iota(jnp.int32, logits.shape, 1).astype(F32)
    lg = jnp.where(lane < n_experts, logits, -jnp.inf)
    m1 = jnp.max(lg, axis=-1, keepdims=True)
    i1 = jnp.min(jnp.where(lg == m1, lane, float(V7X_LANES)), axis=-1, keepdims=True)
    lg2 = jnp.where(lane == i1, -jnp.inf, lg)
    m2 = jnp.max(lg2, axis=-1, keepdims=True)
    i2 = jnp.min(jnp.where(lg2 == m2, lane, float(V7X_LANES)), axis=-1, keepdims=True)
    e2 = jnp.exp(m2 - m1)
    w1 = 1.0 / (1.0 + e2)
    w2 = e2 / (1.0 + e2)
    route_ref[...] = (jnp.where(lane == ROUTE_E1, i1, 0.0) + jnp.where(lane == ROUTE_E2, i2, 0.0)
                      + jnp.where(lane == ROUTE_W1, w1, 0.0) + jnp.where(lane == ROUTE_W2, w2, 0.0))


def _modulate(x3, gain, mod_l, shift_idx, scale_idx, router=None):
    n_batch, seq, d = x3.shape
    ts = _pick(seq, (512, 256, 128))
    grid = (n_batch, seq // ts)
    x_spec = pl.BlockSpec((None, ts, d), lambda b, i: (b, i, 0))
    g_spec = pl.BlockSpec((1, d), lambda b, i: (0, 0))
    mod_spec = pl.BlockSpec((None, N_MOD, d), lambda b, i: (b, 0, 0))
    h_spec = pl.BlockSpec((None, ts, d), lambda b, i: (b, i, 0))
    vmem = 2 * ts * d * (4 + 2) + 4 * ts * d * 4
    if router is None:
        h = pl.pallas_call(
            functools.partial(_modulate_body, shift_idx=shift_idx, scale_idx=scale_idx),
            grid=grid,
            in_specs=[x_spec, g_spec, mod_spec],
            out_specs=h_spec,
            out_shape=jax.ShapeDtypeStruct((n_batch, seq, d), BF16),
            compiler_params=_cparams(("parallel", "parallel"), vmem),
            name="modulate",
        )(x3, gain.reshape(1, d), mod_l)
        return h.reshape(n_batch * seq, d)
    n_experts = router.shape[1]
    assert n_experts <= V7X_LANES
    r_pad = jnp.zeros((d, V7X_LANES), F32).at[:, :n_experts].set(router)
    h, route = pl.pallas_call(
        functools.partial(_modulate_route_body, shift_idx=shift_idx, scale_idx=scale_idx, n_experts=n_experts),
        grid=grid,
        in_specs=[x_spec, g_spec, mod_spec, pl.BlockSpec((d, V7X_LANES), lambda b, i: (0, 0))],
        out_specs=[h_spec, pl.BlockSpec((None, ts, V7X_LANES), lambda b, i: (b, i, 0))],
        out_shape=[jax.ShapeDtypeStruct((n_batch, seq, d), F32),
                   jax.ShapeDtypeStruct((n_batch, seq, V7X_LANES), F32)],
        compiler_params=_cparams(("parallel", "parallel"), vmem + 2 * ts * d * 2 + 6 * ts * d * 2),
        name="modulate_route",
    )(x3, gain.reshape(1, d), mod_l, r_pad)
    return h.reshape(n_batch * seq, d), route.reshape(n_batch * seq, V7X_LANES)


def _mm_cast_body(a_ref, w_ref, o_ref):
    o_ref[...] = _dot(a_ref[...], w_ref[...]).astype(o_ref.dtype)


def _mm_cast(a, w, out_dtype=BF16, name="mm"):
    m, k = a.shape
    n = w.shape[1]
    tn = n if n <= 2048 else _pick(n, (1024, 512, 256, 128))
    need = lambda tm_: 2 * (tm_ * k * 2 + k * tn * 2 + tm_ * tn * 2) + tm_ * tn * 4
    tm = next(t_ for t_ in (1024, 512, 256, 128) if m % t_ == 0 and need(t_) * 5 // 4 <= VMEM_CAP)
    vmem = need(tm)
    return pl.pallas_call(
        _mm_cast_body,
        grid=(m // tm, n // tn),
        in_specs=[pl.BlockSpec((tm, k), lambda i, j: (i, 0)),
                  pl.BlockSpec((k, tn), lambda i, j: (0, j))],
        out_specs=pl.BlockSpec((tm, tn), lambda i, j: (i, j)),
        out_shape=jax.ShapeDtypeStruct((m, n), out_dtype),
        compiler_params=_cparams(("parallel", "arbitrary"), vmem),
        name=name,
    )(a, w)


def _mm_swiglu_body(a_ref, wg_ref, wu_ref, o_ref):
    a = a_ref[...]
    g = _dot(a, wg_ref[...])
    u = _dot(a, wu_ref[...])
    o_ref[...] = (_silu(g) * u).astype(o_ref.dtype)


def _mm_swiglu(a, wg, wu, name="swiglu"):
    m, k = a.shape
    n = wg.shape[1]
    tn = _pick(n, (512, 256, 128))
    need = lambda tm_: 2 * (tm_ * k * 2 + 2 * k * tn * 2 + tm_ * tn * 2) + 3 * tm_ * tn * 4
    tm = next(t_ for t_ in (2048, 1024, 512, 256, 128) if m % t_ == 0 and need(t_) * 10 // 9 <= VMEM_CAP)
    vmem = need(tm)
    return pl.pallas_call(
        _mm_swiglu_body,
        grid=(m // tm, n // tn),
        in_specs=[pl.BlockSpec((tm, k), lambda i, j: (i, 0)),
                  pl.BlockSpec((k, tn), lambda i, j: (0, j)),
                  pl.BlockSpec((k, tn), lambda i, j: (0, j))],
        out_specs=pl.BlockSpec((tm, tn), lambda i, j: (i, j)),
        out_shape=jax.ShapeDtypeStruct((m, n), BF16),
        compiler_params=_cparams(("parallel", "arbitrary"), vmem),
        name=name,
    )(a, wg, wu)


def _mm_residual_body(a_ref, w_ref, x_ref, gate_ref, o_ref, *, nk):
    scale = gate_ref[...]
    part = _dot(a_ref[...], w_ref[...])
    if nk == 1:
        o_ref[...] = x_ref[...] + scale * part
    else:
        kk = pl.program_id(2)

        @pl.when(kk == 0)
        def _():
            o_ref[...] = part

        @pl.when(jnp.logical_and(kk > 0, kk < nk - 1))
        def _():
            o_ref[...] += part

        @pl.when(kk == nk - 1)
        def _():
            o_ref[...] = x_ref[...] + scale * (o_ref[...] + part)


def _mm_residual_ksplit_body(a_ref, w_ref, x_ref, gate_ref, o_ref, acc_ref, *, nk):
    kk = pl.program_id(1)
    j = pl.program_id(2)
    part = _dot(a_ref[...], w_ref[...])

    @pl.when(kk == 0)
    def _():
        acc_ref[j] = part

    @pl.when(jnp.logical_and(kk > 0, kk < nk - 1))
    def _():
        acc_ref[j] += part

    @pl.when(kk == nk - 1)
    def _():
        o_ref[...] = x_ref[...] + gate_ref[...] * (acc_ref[j] + part)


def _mm_residual_ksplit(a, w, x, gate, seq, tk, name):
    m, k = a.shape
    n = w.shape[1]
    tm = _pick(seq, (512, 256, 128))
    tn = _pick(n, (512, 256, 128))
    nk, nj = k // tk, n // tn
    per_batch = seq // tm

    def last(kk, j):
        return jnp.where(kk == nk - 1, j, 0)

    vmem = 2 * (tm * tk * 2 + tk * tn * 2 + 2 * tm * tn * 4) + nj * tm * tn * 4 + tm * tn * 4
    return pl.pallas_call(
        functools.partial(_mm_residual_ksplit_body, nk=nk),
        grid=(m // tm, nk, nj),
        in_specs=[pl.BlockSpec((tm, tk), lambda i, kk, j: (i, kk)),
                  pl.BlockSpec((tk, tn), lambda i, kk, j: (kk, j)),
                  pl.BlockSpec((tm, tn), lambda i, kk, j: (i, last(kk, j))),
                  pl.BlockSpec((None, 1, tn), lambda i, kk, j: (i // per_batch, 0, last(kk, j)))],
        out_specs=pl.BlockSpec((tm, tn), lambda i, kk, j: (i, last(kk, j))),
        out_shape=jax.ShapeDtypeStruct((m, n), F32),
        scratch_shapes=[pltpu.VMEM((nj, tm, tn), F32)],
        compiler_params=_cparams(("parallel", "arbitrary", "arbitrary"), vmem),
        name=name,
    )(a, w, x, gate)


def _mm_residual(a, w, x, gate, seq, name="mm_res"):
    m, k = a.shape
    n = w.shape[1]
    tm = _pick(seq, (1024, 512, 256, 128))
    tn = _pick(n, (1024, 512, 256, 128))
    if k <= 4096:
        tk = k
    else:
        tk = next(t for t in range(4096 // V7X_LANES * V7X_LANES, 0, -V7X_LANES) if k % t == 0)
        if tk < 512:
            tk = next(t for t in range(k // 2 // V7X_LANES * V7X_LANES, 0, -V7X_LANES) if k % t == 0)
    nk = k // tk
    if nk > 1:
        return _mm_residual_ksplit(a, w, x, gate, seq, tk, name)
    per_batch = seq // tm
    in_specs = [pl.BlockSpec((tm, tk), lambda i, j, kk: (i, kk)),
                pl.BlockSpec((tk, tn), lambda i, j, kk: (kk, j)),
                pl.BlockSpec((tm, tn), lambda i, j, kk: (i, j)),
                pl.BlockSpec((None, 1, tn), lambda i, j, kk: (i // per_batch, 0, j))]
    vmem = 2 * (tm * tk * 2 + tk * tn * 2 + 2 * tm * tn * 4) + 2 * tm * tn * 4
    return pl.pallas_call(
        functools.partial(_mm_residual_body, nk=nk),
        grid=(m // tm, n // tn, nk),
        in_specs=in_specs,
        out_specs=pl.BlockSpec((tm, tn), lambda i, j, kk: (i, j)),
        out_shape=jax.ShapeDtypeStruct((m, n), F32),
        compiler_params=_cparams(("parallel", "parallel", "arbitrary"), vmem),
        name=name,
    )(a, w, x, gate)


def _merge_body(a_ref, b_ref, c_ref, wa_ref, wb_ref, wc_ref, ga_ref, gb_ref, gc_ref, o_ref):
    ya = _dot(a_ref[...], wa_ref[...])
    yb = _dot(b_ref[...], wb_ref[...])
    yc = _dot(c_ref[...], wc_ref[...])
    out = (_sigmoid(ga_ref[...].astype(F32)) * ya + _sigmoid(gb_ref[...].astype(F32)) * yb
           + _sigmoid(gc_ref[...].astype(F32)) * yc)
    o_ref[...] = out.astype(o_ref.dtype)


def _merge(o_a, o_b, o_c, w_a, w_b, w_c, gates):
    m = o_a.shape[0]
    d = w_a.shape[1]
    tm = _pick(m, (1024, 512, 256, 128))
    tn = _pick(d, (512, 256, 128))
    nj = d // tn
    ka, kb, kc = o_a.shape[1], o_b.shape[1], o_c.shape[1]
    vmem = 2 * 2 * (tm * (ka + kb + kc) + (ka + kb + kc) * tn + 4 * tm * tn) + 6 * tm * tn * 4
    return pl.pallas_call(
        _merge_body,
        grid=(m // tm, nj),
        in_specs=[pl.BlockSpec((tm, ka), lambda i, j: (i, 0)),
                  pl.BlockSpec((tm, kb), lambda i, j: (i, 0)),
                  pl.BlockSpec((tm, kc), lambda i, j: (i, 0)),
                  pl.BlockSpec((ka, tn), lambda i, j: (0, j)),
                  pl.BlockSpec((kb, tn), lambda i, j: (0, j)),
                  pl.BlockSpec((kc, tn), lambda i, j: (0, j)),
                  pl.BlockSpec((tm, tn), lambda i, j: (i, j)),
                  pl.BlockSpec((tm, tn), lambda i, j: (i, nj + j)),
                  pl.BlockSpec((tm, tn), lambda i, j: (i, 2 * nj + j))],
        out_specs=pl.BlockSpec((tm, tn), lambda i, j: (i, j)),
        out_shape=jax.ShapeDtypeStruct((m, d), BF16),
        compiler_params=_cparams(("parallel", "arbitrary"), vmem),
        name="merge",
    )(o_a, o_b, o_c, w_a, w_b, w_c, gates, gates, gates)


def _route_metadata(route, n_experts):
    t = route.shape[0]
    a_tot = TOP_K * t
    a_pad = a_tot + n_experts * MOE_TILE
    e = jnp.concatenate([route[:, ROUTE_E1], route[:, ROUTE_E2]]).astype(jnp.int32)
    order = jnp.argsort(e, stable=True).astype(jnp.int32)
    counts = jnp.sum((e[:, None] == jnp.arange(n_experts, dtype=jnp.int32)[None, :]).astype(jnp.int32), axis=0)
    padded = (counts + MOE_TILE - 1) // MOE_TILE * MOE_TILE
    ends_u = jnp.cumsum(counts)
    ends_p = jnp.cumsum(padded)
    start_u = ends_u - counts
    start_p = ends_p - padded
    p = jnp.arange(a_pad, dtype=jnp.int32)
    ep = jnp.minimum(jnp.searchsorted(ends_p, p, side="right"), n_experts - 1).astype(jnp.int32)
    rank = p - start_p[ep]
    valid = jnp.logical_and(rank < counts[ep], p < ends_p[-1])
    src = jnp.clip(start_u[ep] + rank, 0, a_tot - 1)
    row_token = jnp.where(valid, order[src] % t, 0).astype(jnp.int32)
    rank_sorted = jnp.argsort(order).astype(jnp.int32)
    pos = rank_sorted + (start_p - start_u)[e]
    n_tiles = a_pad // MOE_TILE
    n_valid = (ends_p[-1] // MOE_TILE).astype(jnp.int32)
    tile_start = jnp.arange(n_tiles, dtype=jnp.int32) * MOE_TILE
    tile_e = jnp.minimum(jnp.searchsorted(ends_p, tile_start, side="right"), n_experts - 1).astype(jnp.int32)
    tile_e = jnp.where(tile_start < ends_p[-1], tile_e, tile_e[jnp.maximum(n_valid - 1, 0)])
    return row_token, tile_e, n_valid.reshape(1), pos[:t], pos[t:]


def _row_gather_start(idx_ref, base, src_ref, dst_ref, sem, rows):
    def issue(r, carry):
        pltpu.make_async_copy(src_ref.at[idx_ref[base + r]], dst_ref.at[r], sem).start()
        return carry

    lax.fori_loop(0, rows, issue, 0, unroll=8)


def _row_gather_wait(src_ref, dst_ref, sem, rows):
    pltpu.make_async_copy(src_ref.at[pl.ds(0, rows)], dst_ref, sem).wait()


def _prefetched_gather(starts, waits):
    i = pl.program_id(0)
    slot = i % 2

    @pl.when(i == 0)
    def _():
        starts(0, 0)

    @pl.when(i + 1 < pl.num_programs(0))
    def _():
        starts(i + 1, 1 - slot)

    waits(slot)
    return slot


def _gather_rows_body(idx_ref, src_ref, o_ref, buf, sem, *, rows):
    def starts(step, slot):
        _row_gather_start(idx_ref, step * rows, src_ref, buf.at[slot], sem.at[slot], rows)

    def waits(slot):
        _row_gather_wait(src_ref, buf.at[slot], sem.at[slot], rows)

    slot = _prefetched_gather(starts, waits)
    o_ref[...] = buf[slot].astype(o_ref.dtype)


def _gather_rows(src, idx):
    m = idx.shape[0]
    w = src.shape[1]
    rows = GATHER_ROWS
    assert m % rows == 0 and src.shape[0] >= rows
    return pl.pallas_call(
        functools.partial(_gather_rows_body, rows=rows),
        grid_spec=pltpu.PrefetchScalarGridSpec(
            num_scalar_prefetch=1,
            grid=(m // rows,),
            in_specs=[pl.BlockSpec(memory_space=pl.ANY)],
            out_specs=pl.BlockSpec((rows, w), lambda i, idx_ref: (i, 0)),
            scratch_shapes=[pltpu.VMEM((2, rows, w), src.dtype), pltpu.SemaphoreType.DMA((2,))]),
        out_shape=jax.ShapeDtypeStruct((m, w), BF16),
        compiler_params=_cparams(("arbitrary",), rows * w * (2 * 4 + 2 * 2 + 4)),
        name="moe_gather",
    )(idx, src)


def _weight_runs(tile_e):
    n_tiles = tile_e.shape[0]
    idx = jnp.arange(n_tiles, dtype=jnp.int32)
    start = jnp.logical_or(idx == 0, tile_e != jnp.roll(tile_e, 1))
    run_id = jnp.cumsum(start.astype(jnp.int32)) - 1
    start_or_end = jnp.where(start, idx, n_tiles)
    later = jnp.concatenate([start_or_end[1:], jnp.full((1,), n_tiles, jnp.int32)])
    nxt = jnp.flip(lax.cummin(jnp.flip(later)))
    wraps = nxt >= n_tiles
    next_e = jnp.where(wraps, tile_e[0], tile_e[jnp.minimum(nxt, n_tiles - 1)])
    return (start.astype(jnp.int32), run_id.astype(jnp.int32), next_e.astype(jnp.int32),
            wraps.astype(jnp.int32), (run_id[-1] + 1).reshape(1).astype(jnp.int32))


def _expert_weights(te_ref, start_ref, run_ref, nexte_ref, wraps_ref, nruns_ref, w_hbms, wbuf, w_bfs, sem,
                    *, tn, nj):
    j = pl.program_id(0)
    i = pl.program_id(1)

    def copies(e, jj, slot):
        cols = pl.ds(pl.multiple_of(jj * tn, tn), tn)
        return [pltpu.make_async_copy(w.at[e, :, cols], wbuf.at[slot, n], sem.at[slot, n])
                for n, w in enumerate(w_hbms)]

    @pl.when(start_ref[i] == 1)
    def _():
        run = j * nruns_ref[0] + run_ref[i]
        slot = run % 2

        @pl.when(run == 0)
        def _():
            for c in copies(te_ref[i], j, slot):
                c.start()

        for c in copies(te_ref[i], j, slot):
            c.wait()
        j_next = j + wraps_ref[i]

        @pl.when(j_next < nj)
        def _():
            for c in copies(nexte_ref[i], j_next, 1 - slot):
                c.start()

        for n, w_bf in enumerate(w_bfs):
            w_bf[...] = wbuf[slot, n].astype(BF16)


def _gmm_swiglu_body(te_ref, nv_ref, start_ref, run_ref, nexte_ref, wraps_ref, nruns_ref,
                     x_ref, wg_hbm, wu_hbm, o_ref, wbuf, wg_bf, wu_bf, sem, *, tn, nj):
    valid = pl.program_id(1) < nv_ref[0]
    _expert_weights(te_ref, start_ref, run_ref, nexte_ref, wraps_ref, nruns_ref,
                    (wg_hbm, wu_hbm), wbuf, (wg_bf, wu_bf), sem, tn=tn, nj=nj)

    @pl.when(valid)
    def _():
        a = x_ref[...]
        g = _dot(a, wg_bf[...])
        u = _dot(a, wu_bf[...])
        o_ref[...] = (_silu(g) * u).astype(o_ref.dtype)

    @pl.when(jnp.logical_not(valid))
    def _():
        o_ref[...] = jnp.zeros_like(o_ref)


def _gmm_down_body(te_ref, nv_ref, start_ref, run_ref, nexte_ref, wraps_ref, nruns_ref,
                   a_ref, w_hbm, o_ref, wbuf, w_bf, sem, *, tn, nj):
    valid = pl.program_id(1) < nv_ref[0]
    _expert_weights(te_ref, start_ref, run_ref, nexte_ref, wraps_ref, nruns_ref,
                    (w_hbm,), wbuf, (w_bf,), sem, tn=tn, nj=nj)

    @pl.when(valid)
    def _():
        o_ref[...] = _dot(a_ref[...], w_bf[...])

    @pl.when(jnp.logical_not(valid))
    def _():
        o_ref[...] = jnp.zeros_like(o_ref)


def _gmm_maps():
    def rows(j, i, te, nv, *_):
        return (jnp.minimum(i, nv[0] - 1), 0)

    def weights(j, i, te, nv, *_):
        return (te[i], 0, j)

    def out(j, i, te, nv, *_):
        return (i, j)

    return rows, weights, out


def _gmm_swiglu(x_sorted, wg, wu, tile_e, n_valid):
    m, k = x_sorted.shape
    n = wg.shape[2]
    tm = MOE_TILE
    tn = _pick(n, (512, 256, 128))
    nj = n // tn
    rows, _, out = _gmm_maps()
    vmem = 2 * (tm * k * 2 + 2 * k * tn * 4 + tm * tn * 2) + 2 * k * tn * 2 + 3 * tm * tn * 4
    return pl.pallas_call(
        functools.partial(_gmm_swiglu_body, tn=tn, nj=nj),
        grid_spec=pltpu.PrefetchScalarGridSpec(
            num_scalar_prefetch=7,
            grid=(nj, m // tm),
            in_specs=[pl.BlockSpec((tm, k), rows),
                      pl.BlockSpec(memory_space=pl.ANY),
                      pl.BlockSpec(memory_space=pl.ANY)],
            out_specs=pl.BlockSpec((tm, tn), out),
            scratch_shapes=[pltpu.VMEM((2, 2, k, tn), F32), pltpu.VMEM((k, tn), BF16), pltpu.VMEM((k, tn), BF16),
                            pltpu.SemaphoreType.DMA((2, 2))]),
        out_shape=jax.ShapeDtypeStruct((m, n), BF16),
        compiler_params=_cparams(("arbitrary", "arbitrary"), vmem),
        name="moe_up",
    )(tile_e, n_valid, *_weight_runs(tile_e), x_sorted, wg, wu)


def _gmm_down(a_sorted, wd, tile_e, n_valid):
    m, k = a_sorted.shape
    n = wd.shape[2]
    tm = MOE_TILE
    tn = _pick(n, (1024, 512, 256, 128))
    nj = n // tn
    rows, _, out = _gmm_maps()
    vmem = 2 * (tm * k * 2 + k * tn * 4 + tm * tn * 4) + k * tn * 2 + tm * tn * 4
    return pl.pallas_call(
        functools.partial(_gmm_down_body, tn=tn, nj=nj),
        grid_spec=pltpu.PrefetchScalarGridSpec(
            num_scalar_prefetch=7,
            grid=(nj, m // tm),
            in_specs=[pl.BlockSpec((tm, k), rows),
                      pl.BlockSpec(memory_space=pl.ANY)],
            out_specs=pl.BlockSpec((tm, tn), out),
            scratch_shapes=[pltpu.VMEM((2, 1, k, tn), F32), pltpu.VMEM((k, tn), BF16),
                            pltpu.SemaphoreType.DMA((2, 1))]),
        out_shape=jax.ShapeDtypeStruct((m, n), F32),
        compiler_params=_cparams(("arbitrary", "arbitrary"), vmem),
        name="moe_down",
    )(tile_e, n_valid, *_weight_runs(tile_e), a_sorted, wd)


def _moe_combine_body(p1_ref, p2_ref, y_ref, x_ref, gate_ref, route_ref, o_ref, buf1, buf2, sem, *, rows):
    def starts(step, slot):
        _row_gather_start(p1_ref, step * rows, y_ref, buf1.at[slot], sem.at[0, slot], rows)
        _row_gather_start(p2_ref, step * rows, y_ref, buf2.at[slot], sem.at[1, slot], rows)

    def waits(slot):
        _row_gather_wait(y_ref, buf1.at[slot], sem.at[0, slot], rows)
        _row_gather_wait(y_ref, buf2.at[slot], sem.at[1, slot], rows)

    slot = _prefetched_gather(starts, waits)
    route = route_ref[...]
    w1 = route[:, ROUTE_W1:ROUTE_W1 + 1]
    w2 = route[:, ROUTE_W2:ROUTE_W2 + 1]
    o_ref[...] = x_ref[...] + gate_ref[...] * (w1 * buf1[slot] + w2 * buf2[slot])


def _moe_combine(y_sorted, pos1, pos2, x, gate, route, seq):
    t, d = x.shape
    rows = _pick(seq, (128,))
    per_batch = seq // rows
    return pl.pallas_call(
        functools.partial(_moe_combine_body, rows=rows),
        grid_spec=pltpu.PrefetchScalarGridSpec(
            num_scalar_prefetch=2,
            grid=(t // rows,),
            in_specs=[pl.BlockSpec(memory_space=pl.ANY),
                      pl.BlockSpec((rows, d), lambda i, p1, p2: (i, 0)),
                      pl.BlockSpec((None, 1, d), lambda i, p1, p2: (i // per_batch, 0, 0)),
                      pl.BlockSpec((rows, V7X_LANES), lambda i, p1, p2: (i, 0))],
            out_specs=pl.BlockSpec((rows, d), lambda i, p1, p2: (i, 0)),
            scratch_shapes=[pltpu.VMEM((2, rows, d), F32), pltpu.VMEM((2, rows, d), F32),
                            pltpu.SemaphoreType.DMA((2, 2))]),
        out_shape=jax.ShapeDtypeStruct((t, d), F32),
        compiler_params=_cparams(("arbitrary",), 10 * rows * d * 4),
        name="moe_combine",
    )(pos1, pos2, y_sorted, x, gate, route)


_HG_LEVELS = (HG_CHUNK // HG_SUB).bit_length() - 1


def _hgrn_level_masks():
    ti = np.arange(HG_CHUNK)[:, None]
    si = np.arange(HG_CHUNK)[None, :]
    out = []
    for lvl in range(_HG_LEVELS):
        half = HG_SUB << lvl
        blk = 2 * half
        out.append((ti // blk == si // blk) & (ti % blk >= half) & (si % blk < half))
    return jnp.asarray(np.stack(out), F32)


def _hgrn_chunk(q_in, f_in, v, g_in, lb, gain, state_t, tri, ones, lvl_mask_ref):
    c = HG_CHUNK
    fg = lb + (1.0 - lb) * _sigmoid(f_in)
    log_f = jnp.log2(jnp.maximum(fg, MIN_FORGET))
    k = 1.0 - fg
    q = _silu(q_in)
    lf_hi, lf_mid, lf_lo = _split3(log_f)
    b = _dot(tri, lf_hi) + _dot(tri, lf_mid) + _dot(tri, lf_lo)
    b_last = b[c - 1:c, :]

    o = _dot_nt((q * jnp.exp2(b)).astype(BF16), state_t.astype(BF16))

    row = lax.broadcasted_iota(jnp.int32, (c, 1), 0)
    scores = jnp.zeros((c, c), F32)
    for lvl in range(_HG_LEVELS):
        half = HG_SUB << lvl
        blk = 2 * half
        bref = jnp.concatenate(
            [jnp.broadcast_to(b[p * blk + half - 1:p * blk + half, :], (blk, HG_DK)) for p in range(c // blk)],
            axis=0)
        is_q = (row & half) != 0
        e = jnp.exp2(-jnp.abs(b - bref))
        xk = jnp.where(is_q, q, k) * e
        qd = jnp.where(is_q, xk, 0.0).astype(BF16)
        kd = jnp.where(is_q, 0.0, xk).astype(BF16)
        scores = scores + _dot_nt(qd, kd) * lvl_mask_ref[lvl]
    o = o + _dot(scores.astype(BF16), v.astype(BF16))

    nb = c // HG_SUB
    b3 = b.reshape(nb, HG_SUB, HG_DK)
    q3 = q.reshape(nb, HG_SUB, HG_DK)
    k3 = k.reshape(nb, HG_SUB, HG_DK)
    v3 = v.reshape(nb, HG_SUB, HG_DK)
    t_in = lax.broadcasted_iota(jnp.int32, (nb, HG_SUB, HG_DK), 1)
    for s in range(HG_SUB):
        diff = b3 - b3[:, s:s + 1, :]
        dec = jnp.exp2(diff if s == 0 else jnp.where(t_in >= s, diff, NEG_BIG))
        m = (q3 * (k3[:, s:s + 1, :] * dec)).reshape(c, HG_DK)
        r = _dot(m.astype(BF16), ones)
        o = o + r * jnp.broadcast_to(v3[:, s:s + 1, :], (nb, HG_SUB, HG_DK)).reshape(c, HG_DK)

    kdec = (k * jnp.exp2(b_last - b)).astype(BF16)
    new_state_t = state_t * jnp.exp2(b_last) + _dot_tn(v.astype(BF16), kdec)

    ms = jnp.mean(o * o, axis=-1, keepdims=True)
    out = o * lax.rsqrt(ms + NORM_EPS) * gain * _silu(g_in)
    return out, new_state_t


def _hgrn_body(q_ref, f_ref, i_ref, g_ref, lbraw_ref, gain_ref, lvl_mask_ref, o_ref, state_ref,
               *, layer, n_chunks, group):
    @pl.when(pl.program_id(2) == 0)
    def _():
        state_ref[...] = jnp.zeros_like(state_ref)

    lbr = lbraw_ref[...]
    ex = jnp.exp(lbr - jnp.max(lbr, axis=0, keepdims=True))
    soft = ex / jnp.sum(ex, axis=0, keepdims=True)
    lb = jnp.zeros((1, group * HG_DK), F32)
    for j in range(1, layer + 1):
        lb = lb + soft[j:j + 1, :]
    gain = gain_ref[...]
    c = HG_CHUNK
    tri = (lax.broadcasted_iota(jnp.int32, (c, c), 0) >= lax.broadcasted_iota(jnp.int32, (c, c), 1)).astype(BF16)
    ones = jnp.ones((HG_DK, HG_DK), BF16)

    def chunk(ci, carry):
        sl = pl.ds(pl.multiple_of(ci * c, c), c)
        for hh in range(group):
            cs = slice(hh * HG_DK, (hh + 1) * HG_DK)
            out, new_state = _hgrn_chunk(q_ref[sl, cs].astype(F32), f_ref[sl, cs].astype(F32),
                                         i_ref[sl, cs].astype(F32), g_ref[sl, cs].astype(F32),
                                         lb[:, cs], gain, state_ref[hh], tri, ones, lvl_mask_ref)
            o_ref[sl, cs] = out.astype(o_ref.dtype)
            state_ref[hh] = new_state
        return carry

    lax.fori_loop(0, n_chunks, chunk, 0)


def _hgrn(proj, lb_raw, out_gain, layer, n_batch, seq, width):
    heads = width // HG_DK
    group = HG_GROUP if heads % HG_GROUP == 0 else 1
    hgroups = heads // group
    gw = group * HG_DK
    lc = _pick(seq, (512, 256, 128))
    per_batch = seq // lc
    depth = lb_raw.shape[0]

    def col(off):
        return pl.BlockSpec((lc, gw), lambda b, h, i: (b * per_batch + i, off * hgroups + h))

    return pl.pallas_call(
        functools.partial(_hgrn_body, layer=layer, n_chunks=lc // HG_CHUNK, group=group),
        grid=(n_batch, hgroups, per_batch),
        in_specs=[col(0), col(1), col(2), col(3),
                  pl.BlockSpec((depth, gw), lambda b, h, i: (0, h)),
                  pl.BlockSpec((1, HG_DK), lambda b, h, i: (0, 0)),
                  pl.BlockSpec((_HG_LEVELS, HG_CHUNK, HG_CHUNK), lambda b, h, i: (0, 0, 0))],
        out_specs=pl.BlockSpec((lc, gw), lambda b, h, i: (b * per_batch + i, h)),
        out_shape=jax.ShapeDtypeStruct((n_batch * seq, width), BF16),
        scratch_shapes=[pltpu.VMEM((group, HG_DK, HG_DK), F32)],
        compiler_params=_cparams(("parallel", "parallel", "arbitrary"), 16 * 1024 * 1024),
        name="hgrn2",
    )(proj, proj, proj, proj, lb_raw, out_gain.reshape(1, HG_DK), _hgrn_level_masks())


def _pool_body(u_ref, halo_ref, w_ref, scale_ref, o_ref, ext_ref, *, ts, gdim):
    i = pl.program_id(1)
    halo = halo_ref[...].astype(F32)
    ext_ref[0:POOL_HALO, :] = jnp.where(i > 0, halo, 0.0)
    ext_ref[POOL_HALO:, :] = u_ref[...].astype(F32)
    pos = (i * ts + lax.broadcasted_iota(jnp.int32, (ts, 1), 0) + 1).astype(F32)
    for g, win in enumerate(POOL_WINDOWS):
        cs = slice(g * gdim, (g + 1) * gdim)
        acc = ext_ref[POOL_HALO:, cs]
        for j in range(1, win):
            acc = acc + ext_ref[POOL_HALO - j:POOL_HALO - j + ts, cs]
        pooled = acc / jnp.minimum(pos, float(win)) - ext_ref[POOL_HALO:, cs]
        y = _dot(pooled.astype(BF16), w_ref[g])
        o_ref[:, cs] = (y * scale_ref[:, cs]).astype(o_ref.dtype)


def _pool(proj, col_off, pool_w, pool_scale, n_batch, seq, width):
    groups = len(POOL_WINDOWS)
    gdim = width // groups
    assert gdim % V7X_LANES == 0 and col_off % width == 0 and max(POOL_WINDOWS) <= POOL_HALO
    ts = _pick(seq, (512, 256, 128))
    per_batch = seq // ts
    cb = col_off // width
    hb = ts // POOL_HALO
    return pl.pallas_call(
        functools.partial(_pool_body, ts=ts, gdim=gdim),
        grid=(n_batch, per_batch),
        in_specs=[pl.BlockSpec((ts, width), lambda b, i: (b * per_batch + i, cb)),
                  pl.BlockSpec((POOL_HALO, width),
                               lambda b, i: (jnp.maximum((b * per_batch + i) * hb - 1, 0), cb)),
                  pl.BlockSpec((groups, gdim, gdim), lambda b, i: (0, 0, 0)),
                  pl.BlockSpec((1, width), lambda b, i: (0, 0))],
        out_specs=pl.BlockSpec((ts, width), lambda b, i: (b * per_batch + i, 0)),
        out_shape=jax.ShapeDtypeStruct((n_batch * seq, width), BF16),
        scratch_shapes=[pltpu.VMEM((ts + POOL_HALO, width), F32)],
        compiler_params=_cparams(("parallel", "parallel"), 16 * 1024 * 1024),
        name="pool",
    )(proj, proj, pool_w, pool_scale.reshape(1, width))


def _rope_table_body(pos_ref, cos_ref, sin_ref):
    pos = pos_ref[...].astype(F32)
    lane = lax.broadcasted_iota(jnp.int32, (1, V7X_LANES), 1)
    j = lane % MLA_ROPE
    fidx = (j % (MLA_ROPE // 2)).astype(F32)
    inv_freq = jnp.exp(fidx * (-2.0 / MLA_ROPE * math.log(ROPE_THETA)))
    ang = pos * inv_freq
    cos_ref[...] = jnp.cos(ang)
    sin_ref[...] = jnp.where(j < MLA_ROPE // 2, -1.0, 1.0) * jnp.sin(ang)


def _rope_table(positions):
    t = positions.size
    ts = _pick(t, (512, 256, 128))
    return pl.pallas_call(
        _rope_table_body,
        grid=(t // ts,),
        in_specs=[pl.BlockSpec((ts, 1), lambda i: (i, 0))],
        out_specs=[pl.BlockSpec((ts, V7X_LANES), lambda i: (i, 0))] * 2,
        out_shape=[jax.ShapeDtypeStruct((t, V7X_LANES), F32)] * 2,
        compiler_params=_cparams(("parallel",), 4 * 1024 * 1024),
        name="rope_table",
    )(positions.reshape(t, 1))


def _rms(x, gain):
    return x * lax.rsqrt(jnp.mean(x * x, axis=-1, keepdims=True) + NORM_EPS) * gain


def _prep_q_body(cq_ref, qn_ref, w_ref, gn_ref, gr_ref, grs_ref, seg_ref, cos_ref, sin_ref, o_ref,
                 *, heads, sm_scale):
    hn = heads * MLA_NOPE
    hr = heads * MLA_ROPE
    hq = _rms(cq_ref[...].astype(F32), qn_ref[...]).astype(BF16)
    y = _dot(hq, w_ref[...])
    yr = y[:, hn:hn + hr]
    ys = y[:, hn + hr:]
    sq_hi, sq_lo = _split2(yr * yr)
    seg = seg_ref[...]
    ss = _dot(sq_hi, seg) + _dot(sq_lo, seg)
    inv = lax.rsqrt(ss * (1.0 / MLA_ROPE) + NORM_EPS)
    reps = hr // V7X_LANES
    cosf = jnp.concatenate([cos_ref[...]] * reps, axis=1)
    sinf = jnp.concatenate([sin_ref[...]] * reps, axis=1)
    qr = (yr * inv * gr_ref[...]) * cosf + (ys * inv * grs_ref[...]) * sinf
    gn = gn_ref[...] * sm_scale
    for h in range(heads):
        qn = _rms(y[:, h * MLA_NOPE:(h + 1) * MLA_NOPE], gn)
        o_ref[h, :, 0:MLA_NOPE] = qn.astype(o_ref.dtype)
        o_ref[h, :, MLA_NOPE:] = (qr[:, h * MLA_ROPE:(h + 1) * MLA_ROPE] * sm_scale).astype(o_ref.dtype)


def _prep_kv_body(ckv_ref, kpe_ref, kvn_ref, w_ref, gn_ref, gr2_ref, cos_ref, sin_ref, k_ref, v_ref, *, heads):
    hk = _rms(ckv_ref[...].astype(F32), kvn_ref[...]).astype(BF16)
    y = _dot(hk, w_ref[...])
    kp = kpe_ref[...].astype(F32)
    kpe = kp[:, :MLA_ROPE]
    inv = lax.rsqrt(jnp.mean(kpe * kpe, axis=-1, keepdims=True) + NORM_EPS)
    kn = kp * inv * gr2_ref[...]
    kr = (kn[:, :MLA_ROPE] * cos_ref[:, :MLA_ROPE] + kn[:, MLA_ROPE:] * sin_ref[:, :MLA_ROPE]).astype(k_ref.dtype)
    per = MLA_NOPE + MLA_V
    for h in range(heads):
        k_ref[h, :, 0:MLA_NOPE] = _rms(y[:, h * per:h * per + MLA_NOPE], gn_ref[...]).astype(k_ref.dtype)
        k_ref[h, :, MLA_NOPE:] = kr
        v_ref[h, 0, :MLA_V, :] = y[:, h * per + MLA_NOPE:(h + 1) * per].T.astype(v_ref.dtype)
        v_ref[h, 0, MLA_V:, :] = jnp.ones((V7X_BF16_ROWS, y.shape[0]), v_ref.dtype)


def _flash_body(q_ref, k_ref, vt_ref, o_ref, acc_ref, st_a, st_b, *, tile, group):
    qi = pl.program_id(2)
    acc_ref[...] = jnp.zeros_like(acc_ref)

    def scores(j, dst):
        ks = pl.ds(pl.multiple_of(j * tile, tile), tile)
        for g in range(group):
            dst[g] = _dot_nt(k_ref[g, ks, :], q_ref[g])

    def consume(j, src, ms, masked):
        m_new = []
        for g in range(group):
            st = src[g]
            if masked:
                kidx = lax.broadcasted_iota(jnp.int32, (tile, tile), 0)
                qidx = lax.broadcasted_iota(jnp.int32, (tile, tile), 1)
                st = jnp.where(kidx <= qidx, st, NEG_BIG)
            m = jnp.maximum(ms[g], jnp.max(st, axis=0, keepdims=True))
            p = jnp.exp2(st - m).astype(BF16)
            acc_ref[g] = jnp.exp2(ms[g] - m) * acc_ref[g] + _dot(vt_ref[g, j], p)
            m_new.append(m)
        return tuple(m_new)

    def finish():
        for g in range(group):
            o = acc_ref[g, :MLA_V, :] / acc_ref[g, MLA_V:MLA_V + 1, :]
            o_ref[:, g * MLA_V:(g + 1) * MLA_V] = o.T.astype(o_ref.dtype)

    scores(0, st_a)

    def pair(pi, ms):
        j = 2 * pi
        scores(j + 1, st_b)
        ms = consume(j, st_a, ms, False)
        scores(j + 2, st_a)
        return consume(j + 1, st_b, ms, False)

    m0 = tuple(jnp.full((1, tile), NEG_BIG, F32) for _ in range(group))
    ms = lax.fori_loop(0, qi // 2, pair, m0)
    odd = qi % 2 == 1

    @pl.when(odd)
    def _():
        scores(qi, st_b)
        consume(qi, st_b, consume(qi - 1, st_a, ms, False), True)
        finish()

    @pl.when(jnp.logical_not(odd))
    def _():
        consume(qi, st_a, ms, True)
        finish()


def _mla(proj_c, cos_t, sin_t, q_norm, w_q, kv_norm, w_kv, g_qn, g_qr, g_kn, g_kr,
         n_batch, seq, heads, q_rank, kv_rank):
    t = n_batch * seq
    half = MLA_ROPE // 2
    hn, hr = heads * MLA_NOPE, heads * MLA_ROPE
    dq = MLA_NOPE + MLA_ROPE
    sm_scale = float(dq) ** -0.5 * math.log2(math.e)
    assert q_rank % kv_rank == 0 and (q_rank + kv_rank) % V7X_LANES == 0 and hr % V7X_LANES == 0
    tile = ATTN_TILE
    assert seq % tile == 0
    tm = tile
    vrows = MLA_V + V7X_BF16_ROWS

    def swap(g):
        return jnp.concatenate([g[half:], g[:half]])

    gr = jnp.tile(g_qr, heads).reshape(1, hr)
    grs = jnp.tile(swap(g_qr), heads).reshape(1, hr)
    lane = jnp.arange(hr) // MLA_ROPE
    seg = (lane[:, None] == lane[None, :]).astype(BF16)
    row = lambda i: (i, 0)
    const2 = lambda i: (0, 0)
    qcat = pl.pallas_call(
        functools.partial(_prep_q_body, heads=heads, sm_scale=sm_scale),
        grid=(t // tm,),
        in_specs=[pl.BlockSpec((tm, q_rank), row),
                  pl.BlockSpec((1, q_rank), const2),
                  pl.BlockSpec((q_rank, hn + 2 * hr), const2),
                  pl.BlockSpec((1, MLA_NOPE), const2),
                  pl.BlockSpec((1, hr), const2),
                  pl.BlockSpec((1, hr), const2),
                  pl.BlockSpec((hr, hr), const2),
                  pl.BlockSpec((tm, V7X_LANES), row),
                  pl.BlockSpec((tm, V7X_LANES), row)],
        out_specs=pl.BlockSpec((heads, tm, dq), lambda i: (0, i, 0)),
        out_shape=jax.ShapeDtypeStruct((heads, t, dq), BF16),
        compiler_params=_cparams(("parallel",), 40 * 1024 * 1024),
        name="mla_prep_q",
    )(proj_c, q_norm.reshape(1, q_rank), w_q, g_qn.reshape(1, MLA_NOPE), gr, grs, seg, cos_t, sin_t)

    gr2 = jnp.concatenate([g_kr, swap(g_kr)]).reshape(1, 2 * MLA_ROPE)
    kcat, vt = pl.pallas_call(
        functools.partial(_prep_kv_body, heads=heads),
        grid=(t // tm,),
        in_specs=[pl.BlockSpec((tm, kv_rank), lambda i: (i, q_rank // kv_rank)),
                  pl.BlockSpec((tm, V7X_LANES), lambda i: (i, (q_rank + kv_rank) // V7X_LANES)),
                  pl.BlockSpec((1, kv_rank), const2),
                  pl.BlockSpec((kv_rank, heads * (MLA_NOPE + MLA_V)), const2),
                  pl.BlockSpec((1, MLA_NOPE), const2),
                  pl.BlockSpec((1, 2 * MLA_ROPE), const2),
                  pl.BlockSpec((tm, V7X_LANES), row),
                  pl.BlockSpec((tm, V7X_LANES), row)],
        out_specs=[pl.BlockSpec((heads, tm, dq), lambda i: (0, i, 0)),
                   pl.BlockSpec((heads, 1, vrows, tile), lambda i: (0, i, 0, 0))],
        out_shape=[jax.ShapeDtypeStruct((heads, t, dq), BF16),
                   jax.ShapeDtypeStruct((heads, t // tile, vrows, tile), BF16)],
        compiler_params=_cparams(("parallel",), 40 * 1024 * 1024),
        name="mla_prep_kv",
    )(proj_c, proj_c, kv_norm.reshape(1, kv_rank), w_kv, g_kn.reshape(1, MLA_NOPE), gr2, cos_t, sin_t)

    nq = seq // tile
    group = ATTN_GROUP if heads % ATTN_GROUP == 0 else 1
    return pl.pallas_call(
        functools.partial(_flash_body, tile=tile, group=group),
        grid=(heads // group, n_batch, nq),
        in_specs=[pl.BlockSpec((group, tile, dq), lambda h, b, i: (h, b * nq + i, 0)),
                  pl.BlockSpec((group, seq, dq), lambda h, b, i: (h, b, 0)),
                  pl.BlockSpec((group, nq, vrows, tile), lambda h, b, i: (h, b, 0, 0))],
        out_specs=pl.BlockSpec((tile, group * MLA_V), lambda h, b, i: (b * nq + i, h)),
        out_shape=jax.ShapeDtypeStruct((t, heads * MLA_V), BF16),
        scratch_shapes=[pltpu.VMEM((group, vrows, tile), F32),
                        pltpu.VMEM((group, tile, tile), F32), pltpu.VMEM((group, tile, tile), F32)],
        compiler_params=_cparams(("parallel", "parallel", "arbitrary"), 48 * 1024 * 1024),
        name="mla_flash",
    )(qcat, kcat, vt)


def kernel(x, c, positions, ada_w, ada_b, ada_layer, mix_norm, ffn_norm, w_in, hgrn_lower_bounds,
           hgrn_out_norm, pool_w, pool_scale, mla_q_norm, mla_w_uq, mla_kv_norm, mla_w_ukv,
           mla_qk_norm_q_nope, mla_qk_norm_q_rope, mla_qk_norm_k_nope, mla_qk_norm_k_rope,
           w_branch_a, w_branch_b, w_branch_c, w_o, ffn_w_gate, ffn_w_up, ffn_w_down,
           moe_router, moe_w_gate, moe_w_up, moe_w_down):
    n_batch, seq, d = x.shape
    depth = w_in.shape[0]
    t = n_batch * seq
    hg_width = hgrn_lower_bounds.shape[1]
    pool_width = pool_scale.shape[1]
    q_rank = mla_q_norm.shape[1]
    kv_rank = mla_kv_norm.shape[1]
    heads = mla_w_ukv.shape[2] // (MLA_NOPE + MLA_V)
    half = MLA_ROPE // 2
    n_a = 4 * hg_width + pool_width
    n_c = q_rank + kv_rank + MLA_ROPE
    assert w_in.shape[2] == n_a + n_c + 3 * d

    mod = _ada(c, ada_w, ada_b, ada_layer)
    cos_t, sin_t = _rope_table(positions)
    xf = x.reshape(t, d)

    for l in range(depth):
        mod_l = mod[l]
        gate1 = mod_l[:, 2:3, :]
        gate2 = mod_l[:, 5:6, :]

        w_l = w_in[l]
        w_a = w_l[:, :n_a].astype(BF16)
        kpe_w = w_l[:, n_a + q_rank + kv_rank:n_a + n_c]
        w_c = jnp.concatenate([w_l[:, n_a:n_a + n_c], kpe_w[:, half:], kpe_w[:, :half]], axis=1).astype(BF16)
        w_g = w_l[:, n_a + n_c:].astype(BF16)
        wq = mla_w_uq[l].reshape(q_rank, heads, MLA_NOPE + MLA_ROPE)
        wq_r = wq[:, :, MLA_NOPE:]
        w_q = jnp.concatenate([
            wq[:, :, :MLA_NOPE].reshape(q_rank, heads * MLA_NOPE),
            wq_r.reshape(q_rank, heads * MLA_ROPE),
            jnp.concatenate([wq_r[:, :, half:], wq_r[:, :, :half]], axis=2).reshape(q_rank, heads * MLA_ROPE),
        ], axis=1).astype(BF16)

        h = _modulate(xf.reshape(n_batch, seq, d), mix_norm[l], mod_l, 0, 1)
        proj_a = _mm_cast(h, w_a, name="proj_a")
        proj_c = _mm_cast(h, w_c, name="proj_c")
        gates = _mm_cast(h, w_g, name="proj_gates")

        o_a = _hgrn(proj_a, hgrn_lower_bounds, hgrn_out_norm[l], l, n_batch, seq, hg_width)
        o_b = _pool(proj_a, 4 * hg_width, pool_w[l].astype(BF16), pool_scale[l], n_batch, seq, pool_width)
        o_c = _mla(proj_c, cos_t, sin_t, mla_q_norm[l], w_q, mla_kv_norm[l], mla_w_ukv[l].astype(BF16),
                   mla_qk_norm_q_nope[l], mla_qk_norm_q_rope[l], mla_qk_norm_k_nope[l], mla_qk_norm_k_rope[l],
                   n_batch, seq, heads, q_rank, kv_rank)
        merged = _merge(o_a, o_b, o_c, w_branch_a[l].astype(BF16), w_branch_b[l].astype(BF16),
                        w_branch_c[l].astype(BF16), gates)
        xf = _mm_residual(merged, w_o[l].astype(BF16), xf, gate1, seq, name="out_proj")

        j = l // 2
        if l % 2 == 0:
            h = _modulate(xf.reshape(n_batch, seq, d), ffn_norm[l], mod_l, 3, 4)
            act = _mm_swiglu(h, ffn_w_gate[j].astype(BF16), ffn_w_up[j].astype(BF16), name="ffn_up")
            xf = _mm_residual(act, ffn_w_down[j].astype(BF16), xf, gate2, seq, name="ffn_down")
        else:
            n_experts = moe_w_gate.shape[1]
            h, route = _modulate(xf.reshape(n_batch, seq, d), ffn_norm[l], mod_l, 3, 4, router=moe_router[j])
            row_token, tile_e, n_valid, pos1, pos2 = _route_metadata(route, n_experts)
            h_sorted = _gather_rows(h, row_token)
            act = _gmm_swiglu(h_sorted, moe_w_gate[j], moe_w_up[j], tile_e, n_valid)
            y_sorted = _gmm_down(act, moe_w_down[j], tile_e, n_valid)
            xf = _moe_combine(y_sorted, pos1, pos2, xf, gate2, route, seq)
    return xf.reshape(n_batch, seq, d)
```
